```python
import numpy as np
import jax
import jax.numpy as jnp
from jax import lax

D_MODEL = 2048
BATCH = 2
SEQ = 4096
DEPTH = 2
DEC_BATCH = 16
DEC_SEQ = 32
PAST_LEN = 1024

CHUNK = 64
N_EVEN = (DEPTH + 1) // 2
N_ODD = DEPTH // 2
HEAD_DIM = 64
N_Q_HEADS = 16
N_KV_HEADS = 4
Q_PER_KV = N_Q_HEADS // N_KV_HEADS
WINDOW = 128
N_RET_HEADS = 8
RET_DK = 128
RET_DV = 128
ROPE_BASE = 10000.0
GM_CHUNK = 128
GM_WIDTH = D_MODEL
GM_GROUPS = 8
GM_GROUP_DIM = GM_WIDTH // GM_GROUPS
D_FF = 5632
CONV_W = 3
EPS = 1e-6
EVEN_SIZES = (N_Q_HEADS * HEAD_DIM, N_KV_HEADS * HEAD_DIM, N_KV_HEADS * HEAD_DIM,
              N_RET_HEADS * RET_DK, N_RET_HEADS * RET_DK, N_RET_HEADS * RET_DV, N_RET_HEADS * RET_DV)
EVEN_IN = 1024 + 256 + 256 + 1024 + 1024 + 1024 + 1024
EVEN_MIX = N_Q_HEADS * HEAD_DIM + N_RET_HEADS * RET_DV

kernel_name = "hybrid_swa_retention_gmlp_convffn_stream_step"

F32 = jnp.float32


def rms_norm(x, g):
    xf = x.astype(F32)
    y = xf * lax.rsqrt(jnp.mean(xf * xf, axis=-1, keepdims=True) + EPS)
    return (y * g.astype(F32)).astype(x.dtype)


def layer_norm(x, g, b):
    xf = x.astype(F32)
    mu = jnp.mean(xf, axis=-1, keepdims=True)
    xc = xf - mu
    y = xc * lax.rsqrt(jnp.mean(xc * xc, axis=-1, keepdims=True) + EPS)
    return y * g.astype(F32) + b.astype(F32)


def head_norm(y, g):
    B, T = y.shape[0], y.shape[1]
    mu = jnp.mean(y, axis=-1, keepdims=True)
    yc = y - mu
    yn = yc * lax.rsqrt(jnp.mean(yc * yc, axis=-1, keepdims=True) + EPS)
    return yn.reshape(B, T, N_RET_HEADS * RET_DV) * g.astype(F32)


def rope(x, pos):
    half = x.shape[-1] // 2
    freq = 1.0 / (ROPE_BASE ** (jnp.arange(half, dtype=F32) / half))
    ang = pos.astype(F32)[:, None] * freq[None, :]
    cos = jnp.cos(ang)[None, :, None, :]
    sin = jnp.sin(ang)[None, :, None, :]
    xf = x.astype(F32)
    x1, x2 = xf[..., :half], xf[..., half:]
    return jnp.concatenate([x1 * cos - x2 * sin, x2 * cos + x1 * sin], axis=-1)


def ret_log_gamma():
    return jnp.log1p(-jnp.exp2(-5.0 - jnp.arange(N_RET_HEADS, dtype=F32)))


def decay_terms(log_g, n):
    idx = jnp.arange(n, dtype=F32)
    diff = idx[:, None] - idx[None, :]
    intra = jnp.where(diff[None] >= 0.0,
                      jnp.exp(log_g[:, None, None] * jnp.maximum(diff, 0.0)[None]), 0.0)
    q_decay = jnp.exp(log_g[:, None] * (idx[None, :] + 1.0))
    k_decay = jnp.exp(log_g[:, None] * (n - 1.0 - idx)[None, :])
    chunk_decay = jnp.exp(log_g * n)
    return intra, q_decay, k_decay, chunk_decay


def retention_prompt(q, k, v, log_g):
    B, S = q.shape[0], q.shape[1]
    nc = S // CHUNK
    qc = q.reshape(B, nc, CHUNK, N_RET_HEADS, RET_DK)
    kc = k.reshape(B, nc, CHUNK, N_RET_HEADS, RET_DK)
    vc = v.reshape(B, nc, CHUNK, N_RET_HEADS, RET_DV)
    intra, q_decay, k_decay, chunk_decay = decay_terms(log_g, CHUNK)
    scores = jnp.einsum('bnihd,bnjhd->bnhij', qc, kc) * intra
    inner = jnp.einsum('bnhij,bnjhe->bnihe', scores, vc)
    kv = jnp.einsum('bnjhd,hj,bnjhe->bnhde', kc, k_decay, vc)

    def step(state, kv_n):
        return chunk_decay[None, :, None, None] * state + kv_n, state

    s0 = jnp.zeros((B, N_RET_HEADS, RET_DK, RET_DV), kv.dtype)
    s_final, s_prev = lax.scan(step, s0, jnp.moveaxis(kv, 1, 0))
    s_prev = jnp.moveaxis(s_prev, 0, 1)
    cross = jnp.einsum('bnihd,bnhde->bnihe', qc, s_prev) * q_decay.T[None, None, :, :, None]
    return (inner + cross).reshape(B, S, N_RET_HEADS, RET_DV), s_final


def retention_sample(q, k, v, state, log_g):
    T = q.shape[1]
    intra, q_decay, k_decay, chunk_decay = decay_terms(log_g, T)
    st = state.astype(F32)
    scores = jnp.einsum('bihd,bjhd->bhij', q, k) * intra
    inner = jnp.einsum('bhij,bjhe->bihe', scores, v)
    cross = jnp.einsum('bihd,bhde->bihe', q, st) * q_decay.T[None, :, :, None]
    new_state = chunk_decay[None, :, None, None] * st + jnp.einsum('bjhd,hj,bjhe->bhde', k, k_decay, v)
    return inner + cross, new_state


def sink_softmax(s, sink):
    sk = sink[:, :, None, None]
    m = jnp.maximum(jnp.max(s, axis=-1, keepdims=True), sk)
    p = jnp.exp(s - m)
    return p / (jnp.sum(p, axis=-1, keepdims=True) + jnp.exp(sk - m))


def swa_prompt(q, k, v, sink):
    B, S = q.shape[0], q.shape[1]
    nc, nb = S // CHUNK, WINDOW // CHUNK
    pad = ((0, 0), (WINDOW, 0), (0, 0), (0, 0))
    kp = jnp.pad(k, pad).reshape(B, nc + nb, CHUNK, N_KV_HEADS, HEAD_DIM)
    vp = jnp.pad(v, pad).reshape(B, nc + nb, CHUNK, N_KV_HEADS, HEAD_DIM)
    kb = jnp.concatenate([kp[:, i:i + nc] for i in range(nb + 1)], axis=2)
    vb = jnp.concatenate([vp[:, i:i + nc] for i in range(nb + 1)], axis=2)
    key_pos = jnp.arange(nc)[:, None] * CHUNK - WINDOW + jnp.arange((nb + 1) * CHUNK)[None, :]
    valid = key_pos >= 0
    qb = q.reshape(B, nc, CHUNK, N_KV_HEADS, Q_PER_KV, HEAD_DIM)
    s = jnp.einsum('bnqhgd,bnkhd->bnhgqk', qb.astype(F32), kb.astype(F32)) * (HEAD_DIM ** -0.5)
    s = jnp.where(valid[None, :, None, None, None, :], s, -jnp.inf)
    p = sink_softmax(s, sink)
    o = jnp.einsum('bnhgqk,bnkhd->bnqhgd', p, vb.astype(F32))
    return o.reshape(B, S, N_Q_HEADS * HEAD_DIM)


def swa_sample(q, k, v, cache_k, cache_v, sink):
    B, T = q.shape[0], q.shape[1]
    kf = jnp.concatenate([cache_k.astype(k.dtype), k], axis=1)
    vf = jnp.concatenate([cache_v.astype(v.dtype), v], axis=1)
    qh = q.reshape(B, T, N_KV_HEADS, Q_PER_KV, HEAD_DIM)
    s = jnp.einsum('bqhgd,bkhd->bhgqk', qh.astype(F32), kf.astype(F32)) * (HEAD_DIM ** -0.5)
    p = sink_softmax(s, sink)
    o = jnp.einsum('bhgqk,bkhd->bqhgd', p, vf.astype(F32)).reshape(B, T, N_Q_HEADS * HEAD_DIM)
    n_keep = cache_k.shape[1]
    return o, kf[:, -n_keep:], vf[:, -n_keep:]


def even_mixer(h, pos, w_in, w_out, sinks, ret_g, cache_k=None, cache_v=None, ret_state=None):
    B, T = h.shape[0], h.shape[1]
    z = h @ w_in
    split_at = [int(i) for i in np.cumsum(EVEN_SIZES)[:-1]]
    q, k, v, rq, rk, rv, rg = jnp.split(z, split_at, axis=-1)
    q = q.reshape(B, T, N_Q_HEADS, HEAD_DIM)
    k = k.reshape(B, T, N_KV_HEADS, HEAD_DIM)
    v = v.reshape(B, T, N_KV_HEADS, HEAD_DIM)
    rq = rope(rq.reshape(B, T, N_RET_HEADS, RET_DK), pos)
    rk = rope(rk.reshape(B, T, N_RET_HEADS, RET_DK), pos) * (RET_DK ** -0.5)
    rv = rv.reshape(B, T, N_RET_HEADS, RET_DV).astype(F32)
    sink = sinks.reshape(N_KV_HEADS, Q_PER_KV).astype(F32)
    log_g = ret_log_gamma()
    if cache_k is None:
        attn = swa_prompt(q, k, v, sink)
        new_k, new_v = k[:, -WINDOW:], v[:, -WINDOW:]
        ret, new_state = retention_prompt(rq, rk, rv, log_g)
        new_state = new_state.astype(h.dtype)
    else:
        attn, new_k, new_v = swa_sample(q, k, v, cache_k, cache_v, sink)
        ret, new_state = retention_sample(rq, rk, rv, ret_state, log_g)
        new_state = new_state.astype(ret_state.dtype)
    ret = head_norm(ret, ret_g) * jax.nn.silu(rg.astype(F32))
    mixed = jnp.concatenate([attn.astype(h.dtype), ret.astype(h.dtype)], axis=-1)
    return mixed @ w_out, new_k, new_v, new_state


def spatial_gate(u, v, ws, bs):
    B, T = v.shape[0], v.shape[1]
    L = min(T, GM_CHUNK)
    vc = v.reshape(B, T // L, L, GM_GROUPS, GM_GROUP_DIM)
    w = (ws.astype(F32) * jnp.tril(jnp.ones((GM_CHUNK, GM_CHUNK), F32)))[:, :L, :L]
    mixed = jnp.einsum('gij,bnjgc->bnigc', w, vc) + bs[:, :L].astype(F32).T[None, None, :, :, None]
    return u.astype(F32) * mixed.reshape(B, T, GM_WIDTH)


def odd_mixer(h, w_in, w_out, ln_g, ln_b, ws, bs):
    uv = jax.nn.gelu(h @ w_in, approximate=True)
    u, v = jnp.split(uv, 2, axis=-1)
    vn = layer_norm(v, ln_g, ln_b)
    y = spatial_gate(u, vn, ws, bs).astype(h.dtype) @ w_out
    return y, vn.astype(h.dtype)


def conv_ffn(x, w_up, conv_w, conv_b, w_down, conv_state):
    T = x.shape[1]
    h = x @ w_up
    hp = jnp.concatenate([conv_state.astype(h.dtype), h], axis=1)
    hc = conv_b + hp[:, 0:T] * conv_w[0]
    for i in range(1, CONV_W):
        hc = hc + hp[:, i:i + T] * conv_w[i]
    a, g = jnp.split(hc, 2, axis=-1)
    y = (jax.nn.gelu(g, approximate=True) * a) @ w_down
    return y, hp[:, T:]


def setup_inputs(seed: int = 0) -> dict:
    key = jax.random.key(seed)
    ks = jax.random.split(key, 24)

    def nrm(k, shape, scale):
        return jax.random.normal(k, shape, F32) * scale

    win_cache = min(WINDOW, PAST_LEN)
    return {
        'x_prompt': nrm(ks[0], (BATCH, SEQ, D_MODEL), 1.0),
        'x_sample': nrm(ks[1], (DEC_BATCH, DEC_SEQ, D_MODEL), 1.0),
        'cache_swa_k': nrm(ks[2], (N_EVEN, DEC_BATCH, win_cache, N_KV_HEADS, HEAD_DIM), 1.0),
        'cache_swa_v': nrm(ks[3], (N_EVEN, DEC_BATCH, win_cache, N_KV_HEADS, HEAD_DIM), 1.0),
        'state_ret': nrm(ks[4], (N_EVEN, DEC_BATCH, N_RET_HEADS, RET_DK, RET_DV), 1.0),
        'state_ffn_conv': nrm(ks[5], (DEPTH, DEC_BATCH, CONV_W - 1, 2 * D_FF), 1.0),
        'norm_g': 1.0 + nrm(ks[6], (DEPTH, 4, D_MODEL), 0.02),
        'w_in_even': nrm(ks[7], (N_EVEN, D_MODEL, EVEN_IN), D_MODEL ** -0.5),
        'w_out_even': nrm(ks[8], (N_EVEN, EVEN_MIX, D_MODEL), EVEN_MIX ** -0.5),
        'attn_sinks': nrm(ks[9], (N_EVEN, N_Q_HEADS), 1.0),
        'ret_norm_g': 1.0 + nrm(ks[10], (N_EVEN, N_RET_HEADS * RET_DV), 0.02),
        'w_in_odd': nrm(ks[11], (N_ODD, D_MODEL, 2 * GM_WIDTH), D_MODEL ** -0.5),
        'w_out_odd': nrm(ks[12], (N_ODD, GM_WIDTH, D_MODEL), GM_WIDTH ** -0.5),
        'gm_ln_g': 1.0 + nrm(ks[13], (N_ODD, GM_WIDTH), 0.02),
        'gm_ln_b': nrm(ks[14], (N_ODD, GM_WIDTH), 0.02),
        'gm_ws': nrm(ks[15], (N_ODD, GM_GROUPS, GM_CHUNK, GM_CHUNK), GM_CHUNK ** -0.5),
        'gm_bs': 1.0 + nrm(ks[16], (N_ODD, GM_GROUPS, GM_CHUNK), 0.1),
        'ffn_w_up': nrm(ks[17], (DEPTH, D_MODEL, 2 * D_FF), D_MODEL ** -0.5),
        'ffn_conv_w': nrm(ks[18], (DEPTH, CONV_W, 2 * D_FF), CONV_W ** -0.5),
        'ffn_conv_b': nrm(ks[19], (DEPTH, 2 * D_FF), 0.02),
        'ffn_w_down': nrm(ks[20], (DEPTH, D_FF, D_MODEL), D_FF ** -0.5),
    }


def reference(x_prompt, x_sample, cache_swa_k, cache_swa_v, state_ret, state_ffn_conv,
              norm_g, w_in_even, w_out_even, attn_sinks, ret_norm_g,
              w_in_odd, w_out_odd, gm_ln_g, gm_ln_b, gm_ws, gm_bs,
              ffn_w_up, ffn_conv_w, ffn_conv_b, ffn_w_down):
    pos_p = jnp.arange(x_prompt.shape[1])
    pos_s = PAST_LEN + jnp.arange(x_sample.shape[1])
    hp, hs = x_prompt, x_sample
    kp_l, vp_l, rp_l, cp_l = [], [], [], []
    ks_l, vs_l, rs_l, cs_l, gv_l = [], [], [], [], []
    for layer in range(DEPTH):
        g = norm_g[layer]
        if layer % 2 == 0:
            e = layer // 2
            mp, k_p, v_p, r_p = even_mixer(rms_norm(hp, g[0]), pos_p, w_in_even[e], w_out_even[e],
                                           attn_sinks[e], ret_norm_g[e])
            ms, k_s, v_s, r_s = even_mixer(rms_norm(hs, g[0]), pos_s, w_in_even[e], w_out_even[e],
                                           attn_sinks[e], ret_norm_g[e],
                                           cache_swa_k[e], cache_swa_v[e], state_ret[e])
            kp_l.append(k_p); vp_l.append(v_p); rp_l.append(r_p)
            ks_l.append(k_s); vs_l.append(v_s); rs_l.append(r_s)
        else:
            o = layer // 2
            mp, _ = odd_mixer(rms_norm(hp, g[0]), w_in_odd[o], w_out_odd[o], gm_ln_g[o], gm_ln_b[o],
                              gm_ws[o], gm_bs[o])
            ms, gv_s = odd_mixer(rms_norm(hs, g[0]), w_in_odd[o], w_out_odd[o], gm_ln_g[o], gm_ln_b[o],
                                 gm_ws[o], gm_bs[o])
            gv_l.append(gv_s)
        hp = hp + rms_norm(mp, g[1])
        hs = hs + rms_norm(ms, g[1])
        zero_state = jnp.zeros((hp.shape[0], CONV_W - 1, 2 * D_FF), hp.dtype)
        fp, c_p = conv_ffn(rms_norm(hp, g[2]), ffn_w_up[layer], ffn_conv_w[layer], ffn_conv_b[layer],
                           ffn_w_down[layer], zero_state)
        fs, c_s = conv_ffn(rms_norm(hs, g[2]), ffn_w_up[layer], ffn_conv_w[layer], ffn_conv_b[layer],
                           ffn_w_down[layer], state_ffn_conv[layer])
        cp_l.append(c_p); cs_l.append(c_s)
        hp = hp + rms_norm(fp, g[3])
        hs = hs + rms_norm(fs, g[3])
    new_swa_k_prompt = jnp.stack(kp_l)
    new_swa_v_prompt = jnp.stack(vp_l)
    new_ret_prompt = jnp.stack(rp_l)
    new_conv_prompt = jnp.stack(cp_l)
    new_swa_k_sample = jnp.stack(ks_l)
    new_swa_v_sample = jnp.stack(vs_l)
    new_ret_sample = jnp.stack(rs_l)
    new_conv_sample = jnp.stack(cs_l)
    new_gmlp_v_sample = jnp.stack(gv_l)
    return (hp, hs, new_swa_k_prompt, new_swa_v_prompt, new_ret_prompt, new_conv_prompt,
            new_swa_k_sample, new_swa_v_sample, new_ret_sample, new_conv_sample, new_gmlp_v_sample)
```

```python
import functools

import jax
import jax.numpy as jnp
from jax import lax
from jax.experimental import pallas as pl
from jax.experimental.pallas import tpu as pltpu

F32 = jnp.float32
BF16 = jnp.bfloat16

D_MODEL = 2048
CHUNK = 64
HEAD_DIM = 64
N_Q_HEADS = 16
N_KV_HEADS = 4
Q_PER_KV = N_Q_HEADS // N_KV_HEADS
WINDOW = 128
N_RET_HEADS = 8
RET_DK = 128
RET_DV = 128
ROPE_BASE = 10000.0
GM_CHUNK = 128
GM_GROUPS = 8
GM_GROUP_DIM = D_MODEL // GM_GROUPS
D_FF = 5632
CONV_W = 3
EPS = 1e-6
PAST_LEN = 1024
EVEN_IN = 5632

SUBLANES = 8
VMEM_LIMIT_BYTES = 56 * 1024 * 1024

COL_BLK = 512
KV_BLK = 2
RQ_BLK, RK_BLK, RV_BLK, RG_BLK = 3, 5, 7, 9
RET_HEADS_PER_BLK = COL_BLK // RET_DK
RET_CHUNK = 256


def _params(*sem):
    return pltpu.CompilerParams(dimension_semantics=sem, vmem_limit_bytes=VMEM_LIMIT_BYTES)


def _rmsnorm_rows(x_ref, g_ref, out_ref, rows, chunk):
    g = g_ref[...]

    def body(c, carry):
        r0 = pl.multiple_of(c * chunk, chunk)
        x = x_ref[pl.ds(r0, chunk), :]
        ms = jnp.mean(x * x, axis=-1, keepdims=True)
        out_ref[pl.ds(r0, chunk), :] = ((x * lax.rsqrt(ms + EPS)) * g).astype(out_ref.dtype)
        return carry

    lax.fori_loop(0, rows // chunk, body, 0)


def _cast_rows(src_ref, dst_ref, rows, chunk):
    def body(c, carry):
        r0 = pl.multiple_of(c * chunk, chunk)
        dst_ref[pl.ds(r0, chunk), :] = src_ref[pl.ds(r0, chunk), :].astype(dst_ref.dtype)
        return carry

    lax.fori_loop(0, rows // chunk, body, 0)


def _norm_matmul_kernel(x_ref, g_ref, w_ref, o_ref, xn_ref, wb_ref, *, bm, rc, gelu):
    j = pl.program_id(1)

    @pl.when(j == 0)
    def _():
        _rmsnorm_rows(x_ref, g_ref, xn_ref, bm, min(bm, 128))

    _cast_rows(w_ref, wb_ref, w_ref.shape[0], 256)
    for c in range(bm // rc):
        y = jnp.dot(xn_ref[c * rc:(c + 1) * rc, :], wb_ref[...], preferred_element_type=F32)
        if gelu:
            y = jax.nn.gelu(y, approximate=True)
        o_ref[c * rc:(c + 1) * rc, :] = y.astype(o_ref.dtype)


def _norm_matmul(x, g, w, *, bm, bn, gelu, name):
    m, k = x.shape
    n = w.shape[1]
    rc = min(bm, 256)
    return pl.pallas_call(
        functools.partial(_norm_matmul_kernel, bm=bm, rc=rc, gelu=gelu),
        out_shape=jax.ShapeDtypeStruct((m, n), F32),
        grid=(m // bm, n // bn),
        in_specs=[
            pl.BlockSpec((bm, k), lambda i, j: (i, 0)),
            pl.BlockSpec((1, k), lambda i, j: (0, 0)),
            pl.BlockSpec((k, bn), lambda i, j: (0, j)),
        ],
        out_specs=pl.BlockSpec((bm, bn), lambda i, j: (i, j)),
        scratch_shapes=[pltpu.VMEM((bm, k), BF16), pltpu.VMEM((k, bn), BF16)],
        compiler_params=_params("arbitrary", "arbitrary"),
        name=name,
    )(x, g.reshape(1, k), w)


def _matmul_norm_res_kernel(a_ref, w_ref, g_ref, r_ref, o_ref, wb_ref, *, nk, bm, rc):
    k = pl.program_id(1)
    _cast_rows(w_ref, wb_ref, w_ref.shape[0], 128)

    @pl.when(k == 0)
    def _():
        def zero(c, carry):
            r0 = pl.multiple_of(c * 64, 64)
            o_ref[pl.ds(r0, 64), :] = jnp.zeros((64, o_ref.shape[1]), F32)
            return carry

        lax.fori_loop(0, bm // 64, zero, 0)

    for c in range(bm // rc):
        rows = slice(c * rc, (c + 1) * rc)
        o_ref[rows, :] += jnp.dot(a_ref[rows, :], wb_ref[...], preferred_element_type=F32)

    @pl.when(k == nk - 1)
    def _():
        g = g_ref[...]

        def fin(c, carry):
            r0 = pl.multiple_of(c * 64, 64)
            y = o_ref[pl.ds(r0, 64), :]
            ms = jnp.mean(y * y, axis=-1, keepdims=True)
            o_ref[pl.ds(r0, 64), :] = r_ref[pl.ds(r0, 64), :] + (y * lax.rsqrt(ms + EPS)) * g
            return carry

        lax.fori_loop(0, bm // 64, fin, 0)


def _matmul_norm_res(a, w, g, resid, *, bm, bk, name):
    m, kdim = a.shape
    n = w.shape[1]
    nk = kdim // bk
    rc = min(bm, 256)
    return pl.pallas_call(
        functools.partial(_matmul_norm_res_kernel, nk=nk, bm=bm, rc=rc),
        out_shape=jax.ShapeDtypeStruct((m, n), F32),
        grid=(m // bm, nk),
        in_specs=[
            pl.BlockSpec((bm, bk), lambda i, k: (i, k)),
            pl.BlockSpec((bk, n), lambda i, k: (k, 0)),
            pl.BlockSpec((1, n), lambda i, k: (0, 0)),
            pl.BlockSpec((bm, n), lambda i, k: (i, 0)),
        ],
        out_specs=pl.BlockSpec((bm, n), lambda i, k: (i, 0)),
        scratch_shapes=[pltpu.VMEM((bk, n), BF16)],
        compiler_params=_params("arbitrary", "arbitrary"),
        name=name,
    )(a, w, g.reshape(1, n), resid)


def _ffn_up_kernel(x_ref, g_ref, wa_ref, wg_ref, cwa_ref, cwg_ref, sa_ref, sg_ref,
                   act_ref, la_ref, lg_ref, xn_ref, wb_ref, carry_ref, *, groups, rows_per_group, rc):
    t = pl.program_id(1)
    j = pl.program_id(2)
    bm = groups * rows_per_group

    @pl.when(j == 0)
    def _():
        _rmsnorm_rows(x_ref, g_ref, xn_ref, bm, min(bm, 128))

    _cast_rows(wa_ref, wb_ref.at[0], wa_ref.shape[0], 256)
    _cast_rows(wg_ref, wb_ref.at[1], wg_ref.shape[0], 256)

    @pl.when(t == 0)
    def _():
        carry_ref[j, 0] = sa_ref[...]
        carry_ref[j, 1] = sg_ref[...]

    cws = (cwa_ref[...], cwg_ref[...])
    piece = min(rc, rows_per_group)
    prev = [None, None]
    last = [[None] * groups, [None] * groups]
    for c in range(bm // rc):
        xc = xn_ref[c * rc:(c + 1) * rc, :]
        hs = [jnp.dot(xc, wb_ref[i], preferred_element_type=F32) for i in range(2)]
        for q in range(rc // piece):
            row = c * rc + q * piece
            grp = row // rows_per_group
            conv = []
            for i in range(2):
                hcur = hs[i][q * piece:(q + 1) * piece]
                if row % rows_per_group == 0:
                    prev8 = carry_ref[j, i, grp]
                else:
                    prev8 = prev[i]
                hext = jnp.concatenate([prev8, hcur], axis=0)
                s1 = pltpu.roll(hext, 1, 0)[SUBLANES:]
                s2 = pltpu.roll(hext, 2, 0)[SUBLANES:]
                cw = cws[i]
                conv.append(cw[3:4] + s2 * cw[0:1] + s1 * cw[1:2] + hcur * cw[2:3])
                prev[i] = hcur[piece - SUBLANES:]
                if (row + piece) % rows_per_group == 0:
                    last[i][grp] = prev[i]
            act_ref[row:row + piece, :] = (jax.nn.gelu(conv[1], approximate=True) * conv[0]).astype(BF16)
    for i, l_ref in enumerate((la_ref, lg_ref)):
        for grp in range(groups):
            l_ref[grp] = last[i][grp]
            carry_ref[j, i, grp] = last[i][grp]


def _ffn_up(x, g, w_up, conv_w, conv_b, state, *, groups, rows_per_group, bn, name):
    m, k = x.shape
    nseq = state.shape[0]
    bm = groups * rows_per_group
    seq_groups = nseq // groups
    tiles = m // (bm * seq_groups)
    nj = D_FF // bn
    rc = min(bm, 256)
    cw = jnp.concatenate([conv_w, conv_b[None], jnp.zeros((SUBLANES - CONV_W - 1, 2 * D_FF), F32)], axis=0)
    st = jnp.concatenate([jnp.zeros((nseq, SUBLANES - (CONV_W - 1), 2 * D_FF), F32), state], axis=1)
    act, la, lg = pl.pallas_call(
        functools.partial(_ffn_up_kernel, groups=groups, rows_per_group=rows_per_group, rc=rc),
        out_shape=(jax.ShapeDtypeStruct((m, D_FF), BF16),
                   jax.ShapeDtypeStruct((tiles, nseq, SUBLANES, D_FF), F32),
                   jax.ShapeDtypeStruct((tiles, nseq, SUBLANES, D_FF), F32)),
        grid=(seq_groups, tiles, nj),
        in_specs=[
            pl.BlockSpec((bm, k), lambda s, t, j: (s * tiles + t, 0)),
            pl.BlockSpec((1, k), lambda s, t, j: (0, 0)),
            pl.BlockSpec((k, bn), lambda s, t, j: (0, j)),
            pl.BlockSpec((k, bn), lambda s, t, j: (0, nj + j)),
            pl.BlockSpec((SUBLANES, bn), lambda s, t, j: (0, j)),
            pl.BlockSpec((SUBLANES, bn), lambda s, t, j: (0, nj + j)),
            pl.BlockSpec((groups, SUBLANES, bn), lambda s, t, j: (s, 0, j)),
            pl.BlockSpec((groups, SUBLANES, bn), lambda s, t, j: (s, 0, nj + j)),
        ],
        out_specs=(
            pl.BlockSpec((bm, bn), lambda s, t, j: (s * tiles + t, j)),
            pl.BlockSpec((None, groups, SUBLANES, bn), lambda s, t, j: (t, s, 0, j)),
            pl.BlockSpec((None, groups, SUBLANES, bn), lambda s, t, j: (t, s, 0, j)),
        ),
        scratch_shapes=[pltpu.VMEM((bm, k), BF16), pltpu.VMEM((2, k, bn), BF16),
                        pltpu.VMEM((nj, 2, groups, SUBLANES, bn), F32)],
        compiler_params=_params("arbitrary", "arbitrary", "arbitrary"),
        name=name,
    )(x, g.reshape(1, k), w_up, w_up, cw, cw, st, st)
    keep = SUBLANES - (CONV_W - 1)
    new_state = jnp.concatenate([la[tiles - 1, :, keep:], lg[tiles - 1, :, keep:]], axis=-1)
    return act, new_state


def _attn_heads(q, k, v, sink_ref, valid):
    nq = q.shape[0]
    outs = []
    for h in range(N_KV_HEADS):
        kh = k[:, h * HEAD_DIM:(h + 1) * HEAD_DIM].astype(BF16)
        vh = v[:, h * HEAD_DIM:(h + 1) * HEAD_DIM].astype(BF16)
        qs, sk = [], []
        for gq in range(Q_PER_KV):
            c0 = (h * Q_PER_KV + gq) * HEAD_DIM
            qs.append(q[:, c0:c0 + HEAD_DIM])
            sk.append(jnp.full((nq, 1), sink_ref[h * Q_PER_KV + gq], F32))
        qh = (jnp.concatenate(qs, axis=0) * (HEAD_DIM ** -0.5)).astype(BF16)
        sk = jnp.concatenate(sk, axis=0)
        s = lax.dot_general(qh, kh, (((1,), (1,)), ((), ())), preferred_element_type=F32)
        if valid is not None:
            s = jnp.where(valid, s, -jnp.inf)
        mx = jnp.maximum(jnp.max(s, axis=-1, keepdims=True), sk)
        p = jnp.exp(s - mx)
        den = jnp.sum(p, axis=-1, keepdims=True) + jnp.exp(sk - mx)
        o = jnp.dot((p / den).astype(BF16), vh, preferred_element_type=F32)
        outs.append(jnp.concatenate([o[gq * nq:(gq + 1) * nq] for gq in range(Q_PER_KV)], axis=1))
    return jnp.concatenate(outs, axis=1)


def _attn_prompt_kernel(sink_ref, q_ref, kv_ref, kvp_ref, o_ref, *, tq):
    i = pl.program_id(1)
    kv = jnp.concatenate([kvp_ref[0], kv_ref[0]], axis=0)
    nkeys = WINDOW + CHUNK
    kidx = lax.broadcasted_iota(jnp.int32, (1, nkeys), 1)
    for c in range(tq // CHUNK):
        rows = slice(c * CHUNK, (c + 1) * CHUNK)
        keys = kv[c * CHUNK:c * CHUNK + nkeys]
        valid = (i * tq + c * CHUNK - WINDOW + kidx) >= 0
        o = _attn_heads(q_ref[0, rows, :], keys[:, :256], keys[:, 256:], sink_ref, valid)
        o_ref[0, rows, :] = o.astype(BF16)


def _attn_prompt(z, sinks, *, tq):
    b, t, _ = z.shape
    per = tq // WINDOW
    return pl.pallas_call(
        functools.partial(_attn_prompt_kernel, tq=tq),
        out_shape=jax.ShapeDtypeStruct((b, t, N_Q_HEADS * HEAD_DIM), BF16),
        grid=(b, t // tq),
        in_specs=[
            pl.BlockSpec(memory_space=pltpu.SMEM),
            pl.BlockSpec((1, tq, 1024), lambda bb, i: (bb, i, 0)),
            pl.BlockSpec((1, tq, COL_BLK), lambda bb, i: (bb, i, KV_BLK)),
            pl.BlockSpec((1, WINDOW, COL_BLK), lambda bb, i: (bb, jnp.maximum(i * per - 1, 0), KV_BLK)),
        ],
        out_specs=pl.BlockSpec((1, tq, 1024), lambda bb, i: (bb, i, 0)),
        compiler_params=_params("arbitrary", "arbitrary"),
        name="attn_prompt",
    )(sinks, z, z, z)


def _attn_sample_kernel(sink_ref, q_ref, kv_ref, ck_ref, cv_ref, o_ref):
    kv = kv_ref[0]
    k = jnp.concatenate([ck_ref[0], kv[:, :256]], axis=0)
    v = jnp.concatenate([cv_ref[0], kv[:, 256:]], axis=0)
    o_ref[0] = _attn_heads(q_ref[0], k, v, sink_ref, None).astype(BF16)


def _attn_sample(z, cache_k, cache_v, sinks):
    b, t, _ = z.shape
    nc = cache_k.shape[1]
    return pl.pallas_call(
        _attn_sample_kernel,
        out_shape=jax.ShapeDtypeStruct((b, t, N_Q_HEADS * HEAD_DIM), BF16),
        grid=(b,),
        in_specs=[
            pl.BlockSpec(memory_space=pltpu.SMEM),
            pl.BlockSpec((1, t, 1024), lambda bb: (bb, 0, 0)),
            pl.BlockSpec((1, t, COL_BLK), lambda bb: (bb, 0, KV_BLK)),
            pl.BlockSpec((1, nc, 256), lambda bb: (bb, 0, 0)),
            pl.BlockSpec((1, nc, 256), lambda bb: (bb, 0, 0)),
        ],
        out_specs=pl.BlockSpec((1, t, 1024), lambda bb: (bb, 0, 0)),
        compiler_params=_params("arbitrary"),
        name="attn_sample",
    )(sinks, z, z, cache_k.reshape(b, nc, 256), cache_v.reshape(b, nc, 256))


def _retention_kernel(rq_ref, rk_ref, rv_ref, rg_ref, cs_ref, sn_ref, intra_ref, qd_ref, kd_ref, cd_ref,
                      ng_ref, s0_ref, o_ref, st_ref):
    c = pl.program_id(2)

    @pl.when(c == 0)
    def _():
        st_ref[...] = s0_ref[...]

    cs = cs_ref[...]
    sn = sn_ref[...]
    for hh in range(RET_HEADS_PER_BLK):
        lanes = slice(hh * RET_DK, (hh + 1) * RET_DK)
        q = rq_ref[0, :, lanes]
        k = rk_ref[0, :, lanes]
        qr = q * cs + pltpu.roll(q, RET_DK // 2, 1) * sn
        kr = (k * cs + pltpu.roll(k, RET_DK // 2, 1) * sn) * (RET_DK ** -0.5)
        qb = qr.astype(BF16)
        kb = kr.astype(BF16)
        vb = rv_ref[0, :, lanes].astype(BF16)
        sc = lax.dot_general(qb, kb, (((1,), (1,)), ((), ())), preferred_element_type=F32) * intra_ref[hh]
        inner = jnp.dot(sc.astype(BF16), vb, preferred_element_type=F32)
        state = st_ref[0, hh]
        cross = jnp.dot(qb, state.astype(BF16), preferred_element_type=F32) * qd_ref[hh]
        kdt = jnp.transpose(kr * kd_ref[hh]).astype(BF16)
        st_ref[0, hh] = cd_ref[hh, 0:1, :] * state + jnp.dot(kdt, vb, preferred_element_type=F32)
        r = inner + cross
        mu = jnp.mean(r, axis=-1, keepdims=True)
        yc = r - mu
        yn = yc * lax.rsqrt(jnp.mean(yc * yc, axis=-1, keepdims=True) + EPS)
        o_ref[0, :, lanes] = ((yn * ng_ref[:, lanes]) * jax.nn.silu(rg_ref[0, :, lanes])).astype(BF16)


def _ret_log_gamma():
    return jnp.log1p(-jnp.exp2(-5.0 - jnp.arange(N_RET_HEADS, dtype=F32)))


def _retention(z, pos, state0, norm_g, *, chunk):
    b, t, _ = z.shape
    log_g = _ret_log_gamma()
    idx = jnp.arange(chunk, dtype=F32)
    diff = idx[:, None] - idx[None, :]
    intra = jnp.where(diff[None] >= 0.0,
                      jnp.exp(log_g[:, None, None] * jnp.maximum(diff, 0.0)[None]), 0.0)
    ones = jnp.ones((1, 1, RET_DV), F32)
    q_decay = jnp.exp(log_g[:, None] * (idx[None, :] + 1.0))[:, :, None] * ones
    k_decay = jnp.exp(log_g[:, None] * (chunk - 1.0 - idx)[None, :])[:, :, None] * ones
    c_decay = jnp.exp(log_g * chunk)[:, None, None] * jnp.ones((1, SUBLANES, RET_DV), F32)
    half = RET_DK // 2
    freq = 1.0 / (ROPE_BASE ** (jnp.arange(half, dtype=F32) / half))
    ang = pos.astype(F32)[:, None] * freq[None, :]
    cos, sin = jnp.cos(ang), jnp.sin(ang)
    cs = jnp.concatenate([cos, cos], axis=-1)
    sn = jnp.concatenate([-sin, sin], axis=-1)
    nhb = N_RET_HEADS // RET_HEADS_PER_BLK
    hb = RET_HEADS_PER_BLK

    def zspec(blk):
        return pl.BlockSpec((1, chunk, COL_BLK), lambda bb, g, c: (bb, c, blk + g))

    def tab(rows):
        return pl.BlockSpec((hb, rows, RET_DV), lambda bb, g, c: (g, 0, 0))

    out, st = pl.pallas_call(
        _retention_kernel,
        out_shape=(jax.ShapeDtypeStruct((b, t, N_RET_HEADS * RET_DV), BF16),
                   jax.ShapeDtypeStruct((b, N_RET_HEADS, RET_DK, RET_DV), F32)),
        grid=(b, nhb, t // chunk),
        in_specs=[
            zspec(RQ_BLK), zspec(RK_BLK), zspec(RV_BLK), zspec(RG_BLK),
            pl.BlockSpec((chunk, RET_DK), lambda bb, g, c: (c, 0)),
            pl.BlockSpec((chunk, RET_DK), lambda bb, g, c: (c, 0)),
            pl.BlockSpec((hb, chunk, chunk), lambda bb, g, c: (g, 0, 0)),
            tab(chunk), tab(chunk), tab(SUBLANES),
            pl.BlockSpec((1, COL_BLK), lambda bb, g, c: (0, g)),
            pl.BlockSpec((1, hb, RET_DK, RET_DV), lambda bb, g, c: (bb, g, 0, 0)),
        ],
        out_specs=(
            pl.BlockSpec((1, chunk, COL_BLK), lambda bb, g, c: (bb, c, g)),
            pl.BlockSpec((1, hb, RET_DK, RET_DV), lambda bb, g, c: (bb, g, 0, 0)),
        ),
        compiler_params=_params("arbitrary", "arbitrary", "arbitrary"),
        name="retention",
    )(z, z, z, z, cs, sn, intra, q_decay, k_decay, c_decay, norm_g.reshape(1, -1), state0)
    return out, st


def _gate_kernel(u_ref, v_ref, lg_ref, lb_ref, ws_ref, bs_ref, y_ref, vn_ref, *, rows, span):
    ri = lax.broadcasted_iota(jnp.int32, (span, span), 0)
    ci = lax.broadcasted_iota(jnp.int32, (span, span), 1)
    wt = [jnp.where(ri >= ci, ws_ref[g], 0.0).astype(BF16) for g in range(GM_GROUPS)]
    bs = bs_ref[...]
    lg = lg_ref[...]
    lb = lb_ref[...]
    for c in range(rows // span):
        rs = slice(c * span, (c + 1) * span)
        v = v_ref[0, rs, :]
        mu = jnp.mean(v, axis=-1, keepdims=True)
        xc = v - mu
        vn = (xc * lax.rsqrt(jnp.mean(xc * xc, axis=-1, keepdims=True) + EPS)) * lg + lb
        vn_ref[0, rs, :] = vn
        vb = vn.astype(BF16)
        for g in range(GM_GROUPS):
            cols = slice(g * GM_GROUP_DIM, (g + 1) * GM_GROUP_DIM)
            mixed = jnp.dot(wt[g], vb[:, cols], preferred_element_type=F32) + bs[:, g:g + 1]
            y_ref[0, rs, cols] = (u_ref[0, rs, cols] * mixed).astype(BF16)


def _spatial_gate(uv, ln_g, ln_b, ws, bs, *, rows):
    b, t, _ = uv.shape
    span = min(t, GM_CHUNK)
    ws_l = ws[:, :span, :span]
    bs_t = jnp.transpose(bs[:, :span])
    return pl.pallas_call(
        functools.partial(_gate_kernel, rows=rows, span=span),
        out_shape=(jax.ShapeDtypeStruct((b, t, D_MODEL), BF16),
                   jax.ShapeDtypeStruct((b, t, D_MODEL), F32)),
        grid=(b, t // rows),
        in_specs=[
            pl.BlockSpec((1, rows, D_MODEL), lambda bb, i: (bb, i, 0)),
            pl.BlockSpec((1, rows, D_MODEL), lambda bb, i: (bb, i, 1)),
            pl.BlockSpec((1, D_MODEL), lambda bb, i: (0, 0)),
            pl.BlockSpec((1, D_MODEL), lambda bb, i: (0, 0)),
            pl.BlockSpec((GM_GROUPS, span, span), lambda bb, i: (0, 0, 0)),
            pl.BlockSpec((span, GM_GROUPS), lambda bb, i: (0, 0)),
        ],
        out_specs=(
            pl.BlockSpec((1, rows, D_MODEL), lambda bb, i: (bb, i, 0)),
            pl.BlockSpec((1, rows, D_MODEL), lambda bb, i: (bb, i, 0)),
        ),
        compiler_params=_params("arbitrary", "arbitrary"),
        name="spatial_gate",
    )(uv, uv, ln_g.reshape(1, -1), ln_b.reshape(1, -1), ws_l, bs_t)


def _conv_ffn(h, g2, g3, w_up, conv_w, conv_b, w_down, state, *, groups, rows_per_group, bm, tag):
    act, new_state = _ffn_up(h, g2, w_up, conv_w, conv_b, state, groups=groups,
                             rows_per_group=rows_per_group, bn=512, name="ffn_up_" + tag)
    out = _matmul_norm_res(act, w_down, g3, h, bm=bm, bk=512, name="ffn_down_" + tag)
    return out, new_state


def kernel(x_prompt, x_sample, cache_swa_k, cache_swa_v, state_ret, state_ffn_conv, norm_g, w_in_even,
           w_out_even, attn_sinks, ret_norm_g, w_in_odd, w_out_odd, gm_ln_g, gm_ln_b, gm_ws, gm_bs,
           ffn_w_up, ffn_conv_w, ffn_conv_b, ffn_w_down):
    nb, seq, d = x_prompt.shape
    db, dseq, _ = x_sample.shape
    depth = norm_g.shape[0]
    hp = x_prompt.reshape(nb * seq, d)
    hs = x_sample.reshape(db * dseq, d)
    bm_p, bm_s = 1024, db * dseq
    pos_p = jnp.arange(seq)
    pos_s = PAST_LEN + jnp.arange(dseq)
    kp_l, vp_l, rp_l, cp_l = [], [], [], []
    ks_l, vs_l, rs_l, cs_l, gv_l = [], [], [], [], []
    for layer in range(depth):
        g = norm_g[layer]
        if layer % 2 == 0:
            e = layer // 2
            zp = _norm_matmul(hp, g[0], w_in_even[e], bm=bm_p, bn=COL_BLK, gelu=False, name="in_even_p")
            zs = _norm_matmul(hs, g[0], w_in_even[e], bm=bm_s, bn=COL_BLK, gelu=False, name="in_even_s")
            zp = zp.reshape(nb, seq, EVEN_IN)
            zs = zs.reshape(db, dseq, EVEN_IN)
            attn_p = _attn_prompt(zp, attn_sinks[e], tq=256)
            attn_s = _attn_sample(zs, cache_swa_k[e], cache_swa_v[e], attn_sinks[e])
            ret_p, r_p = _retention(zp, pos_p, jnp.zeros((nb, N_RET_HEADS, RET_DK, RET_DV), F32),
                                    ret_norm_g[e], chunk=RET_CHUNK)
            ret_s, r_s = _retention(zs, pos_s, state_ret[e].astype(F32), ret_norm_g[e], chunk=dseq)
            mixed_p = jnp.concatenate([attn_p, ret_p], axis=-1).reshape(nb * seq, -1)
            mixed_s = jnp.concatenate([attn_s, ret_s], axis=-1).reshape(db * dseq, -1)
            hp = _matmul_norm_res(mixed_p, w_out_even[e], g[1], hp, bm=bm_p, bk=512, name="out_even_p")
            hs = _matmul_norm_res(mixed_s, w_out_even[e], g[1], hs, bm=bm_s, bk=512, name="out_even_s")
            k_new = zs[:, :, 1024:1280].reshape(db, dseq, N_KV_HEADS, HEAD_DIM)
            v_new = zs[:, :, 1280:1536].reshape(db, dseq, N_KV_HEADS, HEAD_DIM)
            n_keep = cache_swa_k.shape[2]
            kp_l.append(zp[:, seq - WINDOW:, 1024:1280].reshape(nb, WINDOW, N_KV_HEADS, HEAD_DIM))
            vp_l.append(zp[:, seq - WINDOW:, 1280:1536].reshape(nb, WINDOW, N_KV_HEADS, HEAD_DIM))
            ks_l.append(jnp.concatenate([cache_swa_k[e], k_new], axis=1)[:, -n_keep:])
            vs_l.append(jnp.concatenate([cache_swa_v[e], v_new], axis=1)[:, -n_keep:])
            rp_l.append(r_p)
            rs_l.append(r_s.astype(state_ret.dtype))
        else:
            o = layer // 2
            uvp = _norm_matmul(hp, g[0], w_in_odd[o], bm=bm_p, bn=512, gelu=True, name="in_odd_p")
            uvs = _norm_matmul(hs, g[0], w_in_odd[o], bm=bm_s, bn=512, gelu=True, name="in_odd_s")
            yp, _ = _spatial_gate(uvp.reshape(nb, seq, -1), gm_ln_g[o], gm_ln_b[o], gm_ws[o], gm_bs[o], rows=512)
            ys, gv = _spatial_gate(uvs.reshape(db, dseq, -1), gm_ln_g[o], gm_ln_b[o], gm_ws[o], gm_bs[o],
                                   rows=dseq)
            hp = _matmul_norm_res(yp.reshape(nb * seq, -1), w_out_odd[o], g[1], hp, bm=bm_p, bk=512,
                                  name="out_odd_p")
            hs = _matmul_norm_res(ys.reshape(db * dseq, -1), w_out_odd[o], g[1], hs, bm=bm_s, bk=512,
                                  name="out_odd_s")
            gv_l.append(gv)
        zero_state = jnp.zeros((nb, CONV_W - 1, 2 * D_FF), F32)
        hp, c_p = _conv_ffn(hp, g[2], g[3], ffn_w_up[layer], ffn_conv_w[layer], ffn_conv_b[layer],
                            ffn_w_down[layer], zero_state, groups=1, rows_per_group=bm_p, bm=bm_p, tag="p")
        hs, c_s = _conv_ffn(hs, g[2], g[3], ffn_w_up[layer], ffn_conv_w[layer], ffn_conv_b[layer],
                            ffn_w_down[layer], state_ffn_conv[layer], groups=db, rows_per_group=dseq,
                            bm=bm_s, tag="s")
        cp_l.append(c_p)
        cs_l.append(c_s)
    return (hp.reshape(nb, seq, d), hs.reshape(db, dseq, d),
            jnp.stack(kp_l), jnp.stack(vp_l), jnp.stack(rp_l), jnp.stack(cp_l),
            jnp.stack(ks_l), jnp.stack(vs_l), jnp.stack(rs_l), jnp.stack(cs_l), jnp.stack(gv_l))
```

```python
import functools

import jax
import jax.numpy as jnp
from jax import lax
from jax.experimental import pallas as pl
from jax.experimental.pallas import tpu as pltpu

F32 = jnp.float32
BF16 = jnp.bfloat16

D_MODEL = 2048
CHUNK = 64
HEAD_DIM = 64
N_Q_HEADS = 16
N_KV_HEADS = 4
Q_PER_KV = N_Q_HEADS // N_KV_HEADS
WINDOW = 128
N_RET_HEADS = 8
RET_DK = 128
RET_DV = 128
ROPE_BASE = 10000.0
GM_CHUNK = 128
GM_GROUPS = 8
GM_GROUP_DIM = D_MODEL // GM_GROUPS
D_FF = 5632
CONV_W = 3
EPS = 1e-6
PAST_LEN = 1024
EVEN_IN = 5632

SUBLANES = 8
VMEM_LIMIT_BYTES = 56 * 1024 * 1024

COL_BLK = 512
KV_BLK = 2
RQ_BLK, RK_BLK, RV_BLK, RG_BLK = 3, 5, 7, 9
RET_HEADS_PER_BLK = COL_BLK // RET_DK
RET_CHUNK = 256


def _params(*sem):
    return pltpu.CompilerParams(dimension_semantics=sem, vmem_limit_bytes=VMEM_LIMIT_BYTES)


def _rmsnorm_rows(x_ref, g_ref, out_ref, rows, chunk):
    g = g_ref[...]

    def body(c, carry):
        r0 = pl.multiple_of(c * chunk, chunk)
        x = x_ref[pl.ds(r0, chunk), :]
        ms = jnp.mean(x * x, axis=-1, keepdims=True)
        out_ref[pl.ds(r0, chunk), :] = ((x * lax.rsqrt(ms + EPS)) * g).astype(out_ref.dtype)
        return carry

    lax.fori_loop(0, rows // chunk, body, 0)


def _norm_matmul_kernel(x_ref, g_ref, w_ref, o_ref, xn_ref, wb_ref, *, bm, rc, gelu):
    j = pl.program_id(1)

    @pl.when(j == 0)
    def _():
        _rmsnorm_rows(x_ref, g_ref, xn_ref, bm, min(bm, 128))

    wb_ref[...] = w_ref[...].astype(BF16)
    for c in range(bm // rc):
        y = jnp.dot(xn_ref[c * rc:(c + 1) * rc, :], wb_ref[...], preferred_element_type=F32)
        if gelu:
            y = jax.nn.gelu(y, approximate=True)
        o_ref[c * rc:(c + 1) * rc, :] = y.astype(o_ref.dtype)


def _norm_matmul(x, g, w, layer, *, bm, bn, gelu, name):
    m, k = x.shape
    n = w.shape[2]
    rc = min(bm, 256)
    return pl.pallas_call(
        functools.partial(_norm_matmul_kernel, bm=bm, rc=rc, gelu=gelu),
        out_shape=jax.ShapeDtypeStruct((m, n), F32),
        grid=(m // bm, n // bn),
        in_specs=[
            pl.BlockSpec((bm, k), lambda i, j: (i, 0)),
            pl.BlockSpec((1, k), lambda i, j: (0, 0)),
            pl.BlockSpec((None, k, bn), lambda i, j: (layer, 0, j)),
        ],
        out_specs=pl.BlockSpec((bm, bn), lambda i, j: (i, j)),
        scratch_shapes=[pltpu.VMEM((bm, k), BF16), pltpu.VMEM((k, bn), BF16)],
        compiler_params=_params("arbitrary", "arbitrary"),
        name=name,
    )(x, g.reshape(1, k), w)


def _matmul_norm_res_kernel(*refs, splits, nk, bm, rc):
    na = len(splits)
    a_refs = refs[:na]
    w_ref, g_ref, r_ref, o_ref, wb_ref = refs[na:]
    k = pl.program_id(1)
    wb_ref[...] = w_ref[...].astype(BF16)

    @pl.when(k == 0)
    def _():
        def zero(c, carry):
            r0 = pl.multiple_of(c * 64, 64)
            o_ref[pl.ds(r0, 64), :] = jnp.zeros((64, o_ref.shape[1]), F32)
            return carry

        lax.fori_loop(0, bm // 64, zero, 0)

    def accumulate(a_ref):
        for c in range(bm // rc):
            rows = slice(c * rc, (c + 1) * rc)
            o_ref[rows, :] += jnp.dot(a_ref[rows, :], wb_ref[...], preferred_element_type=F32)

    if na == 1:
        accumulate(a_refs[0])
    else:
        for a_ref, (k0, k1) in zip(a_refs, splits):
            pl.when((k >= k0) & (k < k1))(functools.partial(accumulate, a_ref))

    @pl.when(k == nk - 1)
    def _():
        g = g_ref[...]

        def fin(c, carry):
            r0 = pl.multiple_of(c * 64, 64)
            y = o_ref[pl.ds(r0, 64), :]
            ms = jnp.mean(y * y, axis=-1, keepdims=True)
            o_ref[pl.ds(r0, 64), :] = r_ref[pl.ds(r0, 64), :] + (y * lax.rsqrt(ms + EPS)) * g
            return carry

        lax.fori_loop(0, bm // 64, fin, 0)


def _matmul_norm_res(a_list, w, layer, g, resid, *, bm, bk, name):
    m = a_list[0].shape[0]
    n = w.shape[2]
    splits, k0 = [], 0
    for a in a_list:
        splits.append((k0, k0 + a.shape[1] // bk))
        k0 = splits[-1][1]
    nk = k0
    rc = min(bm, 256)

    def a_spec(k0, k1):
        return pl.BlockSpec((bm, bk), lambda i, k: (i, jnp.clip(k - k0, 0, k1 - k0 - 1)))

    return pl.pallas_call(
        functools.partial(_matmul_norm_res_kernel, splits=tuple(splits), nk=nk, bm=bm, rc=rc),
        out_shape=jax.ShapeDtypeStruct((m, n), F32),
        grid=(m // bm, nk),
        in_specs=[a_spec(*sp) for sp in splits] + [
            pl.BlockSpec((None, bk, n), lambda i, k: (layer, k, 0)),
            pl.BlockSpec((1, n), lambda i, k: (0, 0)),
            pl.BlockSpec((bm, n), lambda i, k: (i, 0)),
        ],
        out_specs=pl.BlockSpec((bm, n), lambda i, k: (i, 0)),
        scratch_shapes=[pltpu.VMEM((bk, n), BF16)],
        compiler_params=_params("arbitrary", "arbitrary"),
        name=name,
    )(*a_list, w, g.reshape(1, n), resid)


def _ffn_up_kernel(x_ref, g_ref, wa_ref, wg_ref, cwa_ref, cwg_ref, sa_ref, sg_ref,
                   act_ref, la_ref, lg_ref, xn_ref, wb_ref, carry_ref, *, groups, rows_per_group, rc):
    t = pl.program_id(1)
    j = pl.program_id(2)
    bm = groups * rows_per_group

    @pl.when(j == 0)
    def _():
        _rmsnorm_rows(x_ref, g_ref, xn_ref, bm, min(bm, 128))

    wb_ref[0] = wa_ref[...].astype(BF16)
    wb_ref[1] = wg_ref[...].astype(BF16)

    @pl.when(t == 0)
    def _():
        carry_ref[j, 0] = sa_ref[...]
        carry_ref[j, 1] = sg_ref[...]

    cws = (cwa_ref[...], cwg_ref[...])
    piece = min(rc, rows_per_group)
    prev = [None, None]
    last = [[None] * groups, [None] * groups]
    for c in range(bm // rc):
        xc = xn_ref[c * rc:(c + 1) * rc, :]
        hs = [jnp.dot(xc, wb_ref[i], preferred_element_type=F32) for i in range(2)]
        for q in range(rc // piece):
            row = c * rc + q * piece
            grp = row // rows_per_group
            conv = []
            for i in range(2):
                hcur = hs[i][q * piece:(q + 1) * piece]
                if row % rows_per_group == 0:
                    prev8 = carry_ref[j, i, grp]
                else:
                    prev8 = prev[i]
                hext = jnp.concatenate([prev8, hcur], axis=0)
                s1 = pltpu.roll(hext, 1, 0)[SUBLANES:]
                s2 = pltpu.roll(hext, 2, 0)[SUBLANES:]
                cw = cws[i]
                conv.append(cw[3:4] + s2 * cw[0:1] + s1 * cw[1:2] + hcur * cw[2:3])
                prev[i] = hcur[piece - SUBLANES:]
                if (row + piece) % rows_per_group == 0:
                    last[i][grp] = prev[i]
            act_ref[row:row + piece, :] = (jax.nn.gelu(conv[1], approximate=True) * conv[0]).astype(BF16)
    for i, l_ref in enumerate((la_ref, lg_ref)):
        for grp in range(groups):
            l_ref[grp] = last[i][grp]
            carry_ref[j, i, grp] = last[i][grp]


def _ffn_up(x, g, w_up, layer, conv_w, conv_b, state, *, groups, rows_per_group, bn, name):
    m, k = x.shape
    nseq = state.shape[0]
    bm = groups * rows_per_group
    seq_groups = nseq // groups
    tiles = m // (bm * seq_groups)
    nj = D_FF // bn
    rc = min(bm, 256)
    cw = jnp.concatenate([conv_w, conv_b[None], jnp.zeros((SUBLANES - CONV_W - 1, 2 * D_FF), F32)], axis=0)
    st = jnp.concatenate([jnp.zeros((nseq, SUBLANES - (CONV_W - 1), 2 * D_FF), F32), state], axis=1)
    act, la, lg = pl.pallas_call(
        functools.partial(_ffn_up_kernel, groups=groups, rows_per_group=rows_per_group, rc=rc),
        out_shape=(jax.ShapeDtypeStruct((m, D_FF), BF16),
                   jax.ShapeDtypeStruct((tiles, nseq, SUBLANES, D_FF), F32),
                   jax.ShapeDtypeStruct((tiles, nseq, SUBLANES, D_FF), F32)),
        grid=(seq_groups, tiles, nj),
        in_specs=[
            pl.BlockSpec((bm, k), lambda s, t, j: (s * tiles + t, 0)),
            pl.BlockSpec((1, k), lambda s, t, j: (0, 0)),
            pl.BlockSpec((None, k, bn), lambda s, t, j: (layer, 0, j)),
            pl.BlockSpec((None, k, bn), lambda s, t, j: (layer, 0, nj + j)),
            pl.BlockSpec((SUBLANES, bn), lambda s, t, j: (0, j)),
            pl.BlockSpec((SUBLANES, bn), lambda s, t, j: (0, nj + j)),
            pl.BlockSpec((groups, SUBLANES, bn), lambda s, t, j: (s, 0, j)),
            pl.BlockSpec((groups, SUBLANES, bn), lambda s, t, j: (s, 0, nj + j)),
        ],
        out_specs=(
            pl.BlockSpec((bm, bn), lambda s, t, j: (s * tiles + t, j)),
            pl.BlockSpec((None, groups, SUBLANES, bn), lambda s, t, j: (t, s, 0, j)),
            pl.BlockSpec((None, groups, SUBLANES, bn), lambda s, t, j: (t, s, 0, j)),
        ),
        scratch_shapes=[pltpu.VMEM((bm, k), BF16), pltpu.VMEM((2, k, bn), BF16),
                        pltpu.VMEM((nj, 2, groups, SUBLANES, bn), F32)],
        compiler_params=_params("arbitrary", "arbitrary", "arbitrary"),
        name=name,
    )(x, g.reshape(1, k), w_up, w_up, cw, cw, st, st)
    keep = SUBLANES - (CONV_W - 1)
    new_state = jnp.concatenate([la[tiles - 1, :, keep:], lg[tiles - 1, :, keep:]], axis=-1)
    return act, new_state


def _attn_head(q, k, v, sink_ref, h, bias):
    nq = q.shape[0]
    kh = k[:, h * HEAD_DIM:(h + 1) * HEAD_DIM].astype(BF16)
    vh = v[:, h * HEAD_DIM:(h + 1) * HEAD_DIM].astype(BF16)
    qs, sk = [], []
    for gq in range(Q_PER_KV):
        c0 = (h * Q_PER_KV + gq) * HEAD_DIM
        qs.append(q[:, c0:c0 + HEAD_DIM])
        sk.append(jnp.full((nq, 1), sink_ref[h * Q_PER_KV + gq], F32))
    qh = (jnp.concatenate(qs, axis=0) * (HEAD_DIM ** -0.5)).astype(BF16)
    sk = jnp.concatenate(sk, axis=0)
    s = lax.dot_general(qh, kh, (((1,), (1,)), ((), ())), preferred_element_type=F32)
    if bias is not None:
        s = s + bias
    mx = jnp.maximum(jnp.max(s, axis=-1, keepdims=True), sk)
    p = jnp.exp(s - mx)
    den = jnp.sum(p, axis=-1, keepdims=True) + jnp.exp(sk - mx)
    o = jnp.dot((p / den).astype(BF16), vh, preferred_element_type=F32)
    return jnp.concatenate([o[gq * nq:(gq + 1) * nq] for gq in range(Q_PER_KV)], axis=1)


ATTN_UNIT = 2 * CHUNK


def _attn_prompt_kernel(sink_ref, q_ref, kv_ref, kvp_ref, o_ref, bias_ref, *, tq):
    i = pl.program_id(1)
    nk = ATTN_UNIT + WINDOW
    cols = Q_PER_KV * ATTN_UNIT
    kvw = N_KV_HEADS * HEAD_DIM
    r_k = lax.broadcasted_iota(jnp.int32, (nk, cols), 0)
    r_q = lax.broadcasted_iota(jnp.int32, (nk, cols), 1)
    lo = (r_q & (ATTN_UNIT - 1)) & ~(CHUNK - 1)
    band = (r_k >= lo) & (r_k < lo + WINDOW + CHUNK)
    bias_ref[1] = jnp.where(band, 0.0, -jnp.inf)
    first_lo = jnp.where(i == 0, WINDOW, 0)
    bias_ref[0] = jnp.where(band & (r_k >= first_lo), 0.0, -jnp.inf)

    kv_all = jnp.concatenate([kvp_ref[0], kv_ref[0]], axis=0)
    k_all = kv_all[:, :kvw].astype(BF16)
    vt_all = jnp.transpose(kv_all[:, kvw:]).astype(BF16)
    for u in range(tq // ATTN_UNIT):
        rs = slice(u * ATTN_UNIT, (u + 1) * ATTN_UNIT)
        keys = slice(u * ATTN_UNIT, u * ATTN_UNIT + nk)
        k_u = k_all[keys]
        qt = jnp.transpose(q_ref[0, rs, :] * (HEAD_DIM ** -0.5)).astype(BF16)
        bias = bias_ref[min(u, 1)]
        pieces = []
        for h in range(N_KV_HEADS):
            heads = [h * Q_PER_KV + gq for gq in range(Q_PER_KV)]
            qt_h = jnp.concatenate([qt[n * HEAD_DIM:(n + 1) * HEAD_DIM] for n in heads], axis=1)
            parts = [qt_h]
            if h > 0:
                parts.insert(0, jnp.zeros((h * HEAD_DIM, cols), BF16))
            if h < N_KV_HEADS - 1:
                parts.append(jnp.zeros(((N_KV_HEADS - 1 - h) * HEAD_DIM, cols), BF16))
            st = jnp.dot(k_u, jnp.concatenate(parts, axis=0), preferred_element_type=F32) + bias
            sk = jnp.concatenate([jnp.full((1, ATTN_UNIT), sink_ref[n], F32) for n in heads], axis=1)
            mx = jnp.maximum(jnp.max(st, axis=0, keepdims=True), sk)
            p = jnp.exp(st - mx)
            den = jnp.sum(p, axis=0, keepdims=True) + jnp.exp(sk - mx)
            ot = jnp.dot(vt_all[h * HEAD_DIM:(h + 1) * HEAD_DIM, keys], (p / den).astype(BF16),
                         preferred_element_type=F32)
            pieces += [ot[:, gq * ATTN_UNIT:(gq + 1) * ATTN_UNIT] for gq in range(Q_PER_KV)]
        o_ref[0, rs, :] = jnp.transpose(jnp.concatenate(pieces, axis=0)).astype(BF16)


def _attn_prompt(z, sinks, *, tq):
    b, t, _ = z.shape
    per = tq // WINDOW
    return pl.pallas_call(
        functools.partial(_attn_prompt_kernel, tq=tq),
        out_shape=jax.ShapeDtypeStruct((b, t, N_Q_HEADS * HEAD_DIM), BF16),
        grid=(b, t // tq),
        in_specs=[
            pl.BlockSpec(memory_space=pltpu.SMEM),
            pl.BlockSpec((1, tq, 1024), lambda bb, i: (bb, i, 0)),
            pl.BlockSpec((1, tq, COL_BLK), lambda bb, i: (bb, i, KV_BLK)),
            pl.BlockSpec((1, WINDOW, COL_BLK), lambda bb, i: (bb, jnp.maximum(i * per - 1, 0), KV_BLK)),
        ],
        out_specs=pl.BlockSpec((1, tq, 1024), lambda bb, i: (bb, i, 0)),
        scratch_shapes=[pltpu.VMEM((2, ATTN_UNIT + WINDOW, Q_PER_KV * ATTN_UNIT), F32)],
        compiler_params=_params("arbitrary", "arbitrary"),
        name="attn_prompt",
    )(sinks, z, z, z)


def _attn_sample_kernel(sink_ref, q_ref, kv_ref, ck_ref, cv_ref, o_ref):
    kv = kv_ref[0]
    k = jnp.concatenate([ck_ref[0], kv[:, :256]], axis=0)
    v = jnp.concatenate([cv_ref[0], kv[:, 256:]], axis=0)
    q = q_ref[0]
    for h in range(N_KV_HEADS):
        o_ref[0, :, h * 256:(h + 1) * 256] = _attn_head(q, k, v, sink_ref, h, None).astype(BF16)


def _attn_sample(z, cache_k, cache_v, sinks):
    b, t, _ = z.shape
    nc = cache_k.shape[1]
    return pl.pallas_call(
        _attn_sample_kernel,
        out_shape=jax.ShapeDtypeStruct((b, t, N_Q_HEADS * HEAD_DIM), BF16),
        grid=(b,),
        in_specs=[
            pl.BlockSpec(memory_space=pltpu.SMEM),
            pl.BlockSpec((1, t, 1024), lambda bb: (bb, 0, 0)),
            pl.BlockSpec((1, t, COL_BLK), lambda bb: (bb, 0, KV_BLK)),
            pl.BlockSpec((1, nc, 256), lambda bb: (bb, 0, 0)),
            pl.BlockSpec((1, nc, 256), lambda bb: (bb, 0, 0)),
        ],
        out_specs=pl.BlockSpec((1, t, 1024), lambda bb: (bb, 0, 0)),
        compiler_params=_params("arbitrary"),
        name="attn_sample",
    )(sinks, z, z, cache_k.reshape(b, nc, 256), cache_v.reshape(b, nc, 256))


def _retention_kernel(rq_ref, rk_ref, rv_ref, rg_ref, cs_ref, sn_ref, intra_ref, qd_ref, kd_ref, cd_ref,
                      ng_ref, s0_ref, o_ref, st_ref):
    c = pl.program_id(2)

    @pl.when(c == 0)
    def _():
        st_ref[...] = s0_ref[...]

    cs = cs_ref[...]
    sn = sn_ref[...]
    for hh in range(RET_HEADS_PER_BLK):
        lanes = slice(hh * RET_DK, (hh + 1) * RET_DK)
        q = rq_ref[0, :, lanes]
        k = rk_ref[0, :, lanes]
        qr = q * cs + pltpu.roll(q, RET_DK // 2, 1) * sn
        kr = (k * cs + pltpu.roll(k, RET_DK // 2, 1) * sn) * (RET_DK ** -0.5)
        qb = qr.astype(BF16)
        kb = kr.astype(BF16)
        vb = rv_ref[0, :, lanes].astype(BF16)
        sc = lax.dot_general(qb, kb, (((1,), (1,)), ((), ())), preferred_element_type=F32) * intra_ref[hh]
        inner = jnp.dot(sc.astype(BF16), vb, preferred_element_type=F32)
        state = st_ref[0, hh]
        cross = jnp.dot(qb, state.astype(BF16), preferred_element_type=F32) * qd_ref[hh]
        kdt = jnp.transpose(kr * kd_ref[hh]).astype(BF16)
        st_ref[0, hh] = cd_ref[hh, 0:1, :] * state + jnp.dot(kdt, vb, preferred_element_type=F32)
        r = inner + cross
        mu = jnp.mean(r, axis=-1, keepdims=True)
        yc = r - mu
        yn = yc * lax.rsqrt(jnp.mean(yc * yc, axis=-1, keepdims=True) + EPS)
        o_ref[0, :, lanes] = ((yn * ng_ref[:, lanes]) * jax.nn.silu(rg_ref[0, :, lanes])).astype(BF16)


def _ret_log_gamma():
    return jnp.log1p(-jnp.exp2(-5.0 - jnp.arange(N_RET_HEADS, dtype=F32)))


def _retention(z, pos, state0, norm_g, *, chunk):
    b, t, _ = z.shape
    log_g = _ret_log_gamma()
    idx = jnp.arange(chunk, dtype=F32)
    diff = idx[:, None] - idx[None, :]
    intra = jnp.where(diff[None] >= 0.0,
                      jnp.exp(log_g[:, None, None] * jnp.maximum(diff, 0.0)[None]), 0.0)
    ones = jnp.ones((1, 1, RET_DV), F32)
    q_decay = jnp.exp(log_g[:, None] * (idx[None, :] + 1.0))[:, :, None] * ones
    k_decay = jnp.exp(log_g[:, None] * (chunk - 1.0 - idx)[None, :])[:, :, None] * ones
    c_decay = jnp.exp(log_g * chunk)[:, None, None] * jnp.ones((1, SUBLANES, RET_DV), F32)
    half = RET_DK // 2
    freq = 1.0 / (ROPE_BASE ** (jnp.arange(half, dtype=F32) / half))
    ang = pos.astype(F32)[:, None] * freq[None, :]
    cos, sin = jnp.cos(ang), jnp.sin(ang)
    cs = jnp.concatenate([cos, cos], axis=-1)
    sn = jnp.concatenate([-sin, sin], axis=-1)
    nhb = N_RET_HEADS // RET_HEADS_PER_BLK
    hb = RET_HEADS_PER_BLK

    def zspec(blk):
        return pl.BlockSpec((1, chunk, COL_BLK), lambda bb, g, c: (bb, c, blk + g))

    def tab(rows):
        return pl.BlockSpec((hb, rows, RET_DV), lambda bb, g, c: (g, 0, 0))

    out, st = pl.pallas_call(
        _retention_kernel,
        out_shape=(jax.ShapeDtypeStruct((b, t, N_RET_HEADS * RET_DV), BF16),
                   jax.ShapeDtypeStruct((b, N_RET_HEADS, RET_DK, RET_DV), F32)),
        grid=(b, nhb, t // chunk),
        in_specs=[
            zspec(RQ_BLK), zspec(RK_BLK), zspec(RV_BLK), zspec(RG_BLK),
            pl.BlockSpec((chunk, RET_DK), lambda bb, g, c: (c, 0)),
            pl.BlockSpec((chunk, RET_DK), lambda bb, g, c: (c, 0)),
            pl.BlockSpec((hb, chunk, chunk), lambda bb, g, c: (g, 0, 0)),
            tab(chunk), tab(chunk), tab(SUBLANES),
            pl.BlockSpec((1, COL_BLK), lambda bb, g, c: (0, g)),
            pl.BlockSpec((1, hb, RET_DK, RET_DV), lambda bb, g, c: (bb, g, 0, 0)),
        ],
        out_specs=(
            pl.BlockSpec((1, chunk, COL_BLK), lambda bb, g, c: (bb, c, g)),
            pl.BlockSpec((1, hb, RET_DK, RET_DV), lambda bb, g, c: (bb, g, 0, 0)),
        ),
        compiler_params=_params("arbitrary", "arbitrary", "arbitrary"),
        name="retention",
    )(z, z, z, z, cs, sn, intra, q_decay, k_decay, c_decay, norm_g.reshape(1, -1), state0)
    return out, st


def _gate_kernel(u_ref, v_ref, lg_ref, lb_ref, ws_ref, bs_ref, y_ref, vn_ref, *, rows, span):
    ri = lax.broadcasted_iota(jnp.int32, (span, span), 0)
    ci = lax.broadcasted_iota(jnp.int32, (span, span), 1)
    wt = [jnp.where(ri >= ci, ws_ref[g], 0.0).astype(BF16) for g in range(GM_GROUPS)]
    bs = bs_ref[...]
    lg = lg_ref[...]
    lb = lb_ref[...]
    for c in range(rows // span):
        rs = slice(c * span, (c + 1) * span)
        v = v_ref[0, rs, :]
        mu = jnp.mean(v, axis=-1, keepdims=True)
        xc = v - mu
        vn = (xc * lax.rsqrt(jnp.mean(xc * xc, axis=-1, keepdims=True) + EPS)) * lg + lb
        vn_ref[0, rs, :] = vn
        vb = vn.astype(BF16)
        for g in range(GM_GROUPS):
            cols = slice(g * GM_GROUP_DIM, (g + 1) * GM_GROUP_DIM)
            mixed = jnp.dot(wt[g], vb[:, cols], preferred_element_type=F32) + bs[:, g:g + 1]
            y_ref[0, rs, cols] = (u_ref[0, rs, cols] * mixed).astype(BF16)


def _spatial_gate(uv, ln_g, ln_b, ws, bs, *, rows):
    b, t, _ = uv.shape
    span = min(t, GM_CHUNK)
    ws_l = ws[:, :span, :span]
    bs_t = jnp.transpose(bs[:, :span])
    return pl.pallas_call(
        functools.partial(_gate_kernel, rows=rows, span=span),
        out_shape=(jax.ShapeDtypeStruct((b, t, D_MODEL), BF16),
                   jax.ShapeDtypeStruct((b, t, D_MODEL), F32)),
        grid=(b, t // rows),
        in_specs=[
            pl.BlockSpec((1, rows, D_MODEL), lambda bb, i: (bb, i, 0)),
            pl.BlockSpec((1, rows, D_MODEL), lambda bb, i: (bb, i, 1)),
            pl.BlockSpec((1, D_MODEL), lambda bb, i: (0, 0)),
            pl.BlockSpec((1, D_MODEL), lambda bb, i: (0, 0)),
            pl.BlockSpec((GM_GROUPS, span, span), lambda bb, i: (0, 0, 0)),
            pl.BlockSpec((span, GM_GROUPS), lambda bb, i: (0, 0)),
        ],
        out_specs=(
            pl.BlockSpec((1, rows, D_MODEL), lambda bb, i: (bb, i, 0)),
            pl.BlockSpec((1, rows, D_MODEL), lambda bb, i: (bb, i, 0)),
        ),
        compiler_params=_params("arbitrary", "arbitrary"),
        name="spatial_gate",
    )(uv, uv, ln_g.reshape(1, -1), ln_b.reshape(1, -1), ws_l, bs_t)


def _conv_ffn(h, g2, g3, w_up, layer, conv_w, conv_b, w_down, state, *, groups, rows_per_group, bm, tag):
    act, new_state = _ffn_up(h, g2, w_up, layer, conv_w, conv_b, state, groups=groups,
                             rows_per_group=rows_per_group, bn=512, name="ffn_up_" + tag)
    out = _matmul_norm_res([act], w_down, layer, g3, h, bm=bm, bk=512, name="ffn_down_" + tag)
    return out, new_state


def kernel(x_prompt, x_sample, cache_swa_k, cache_swa_v, state_ret, state_ffn_conv, norm_g, w_in_even,
           w_out_even, attn_sinks, ret_norm_g, w_in_odd, w_out_odd, gm_ln_g, gm_ln_b, gm_ws, gm_bs,
           ffn_w_up, ffn_conv_w, ffn_conv_b, ffn_w_down):
    nb, seq, d = x_prompt.shape
    db, dseq, _ = x_sample.shape
    depth = norm_g.shape[0]
    hp = x_prompt.reshape(nb * seq, d)
    hs = x_sample.reshape(db * dseq, d)
    bm_p, bm_s = 1024, db * dseq
    pos_p = jnp.arange(seq)
    pos_s = PAST_LEN + jnp.arange(dseq)
    kp_l, vp_l, rp_l, cp_l = [], [], [], []
    ks_l, vs_l, rs_l, cs_l, gv_l = [], [], [], [], []
    for layer in range(depth):
        g = norm_g[layer]
        if layer % 2 == 0:
            e = layer // 2
            zp = _norm_matmul(hp, g[0], w_in_even, e, bm=bm_p, bn=COL_BLK, gelu=False, name="in_even_p")
            zs = _norm_matmul(hs, g[0], w_in_even, e, bm=bm_s, bn=COL_BLK, gelu=False, name="in_even_s")
            zp = zp.reshape(nb, seq, EVEN_IN)
            zs = zs.reshape(db, dseq, EVEN_IN)
            attn_p = _attn_prompt(zp, attn_sinks[e], tq=256)
            attn_s = _attn_sample(zs, cache_swa_k[e], cache_swa_v[e], attn_sinks[e])
            ret_p, r_p = _retention(zp, pos_p, jnp.zeros((nb, N_RET_HEADS, RET_DK, RET_DV), F32),
                                    ret_norm_g[e], chunk=RET_CHUNK)
            ret_s, r_s = _retention(zs, pos_s, state_ret[e].astype(F32), ret_norm_g[e], chunk=dseq)
            mixed_p = [attn_p.reshape(nb * seq, -1), ret_p.reshape(nb * seq, -1)]
            mixed_s = [attn_s.reshape(db * dseq, -1), ret_s.reshape(db * dseq, -1)]
            hp = _matmul_norm_res(mixed_p, w_out_even, e, g[1], hp, bm=bm_p, bk=512, name="out_even_p")
            hs = _matmul_norm_res(mixed_s, w_out_even, e, g[1], hs, bm=bm_s, bk=512, name="out_even_s")
            k_new = zs[:, :, 1024:1280].reshape(db, dseq, N_KV_HEADS, HEAD_DIM)
            v_new = zs[:, :, 1280:1536].reshape(db, dseq, N_KV_HEADS, HEAD_DIM)
            n_keep = cache_swa_k.shape[2]
            kp_l.append(zp[:, seq - WINDOW:, 1024:1280].reshape(nb, WINDOW, N_KV_HEADS, HEAD_DIM))
            vp_l.append(zp[:, seq - WINDOW:, 1280:1536].reshape(nb, WINDOW, N_KV_HEADS, HEAD_DIM))
            ks_l.append(jnp.concatenate([cache_swa_k[e], k_new], axis=1)[:, -n_keep:])
            vs_l.append(jnp.concatenate([cache_swa_v[e], v_new], axis=1)[:, -n_keep:])
            rp_l.append(r_p)
            rs_l.append(r_s.astype(state_ret.dtype))
        else:
            o = layer // 2
            uvp = _norm_matmul(hp, g[0], w_in_odd, o, bm=bm_p, bn=512, gelu=True, name="in_odd_p")
            uvs = _norm_matmul(hs, g[0], w_in_odd, o, bm=bm_s, bn=512, gelu=True, name="in_odd_s")
            yp, _ = _spatial_gate(uvp.reshape(nb, seq, -1), gm_ln_g[o], gm_ln_b[o], gm_ws[o], gm_bs[o], rows=512)
            ys, gv = _spatial_gate(uvs.reshape(db, dseq, -1), gm_ln_g[o], gm_ln_b[o], gm_ws[o], gm_bs[o],
                                   rows=dseq)
            hp = _matmul_norm_res([yp.reshape(nb * seq, -1)], w_out_odd, o, g[1], hp, bm=bm_p, bk=512,
                                  name="out_odd_p")
            hs = _matmul_norm_res([ys.reshape(db * dseq, -1)], w_out_odd, o, g[1], hs, bm=bm_s, bk=512,
                                  name="out_odd_s")
            gv_l.append(gv)
        zero_state = jnp.zeros((nb, CONV_W - 1, 2 * D_FF), F32)
        hp, c_p = _conv_ffn(hp, g[2], g[3], ffn_w_up, layer, ffn_conv_w[layer], ffn_conv_b[layer],
                            ffn_w_down, zero_state, groups=1, rows_per_group=bm_p, bm=bm_p, tag="p")
        hs, c_s = _conv_ffn(hs, g[2], g[3], ffn_w_up, layer, ffn_conv_w[layer], ffn_conv_b[layer],
                            ffn_w_down, state_ffn_conv[layer], groups=db, rows_per_group=dseq,
                            bm=bm_s, tag="s")
        cp_l.append(c_p)
        cs_l.append(c_s)
    return (hp.reshape(nb, seq, d), hs.reshape(db, dseq, d),
            jnp.stack(kp_l), jnp.stack(vp_l), jnp.stack(rp_l), jnp.stack(cp_l),
            jnp.stack(ks_l), jnp.stack(vs_l), jnp.stack(rs_l), jnp.stack(cs_l), jnp.stack(gv_l))
```

```python
import functools

import jax
import jax.numpy as jnp
from jax import lax
from jax.experimental import pallas as pl
from jax.experimental.pallas import tpu as pltpu

F32 = jnp.float32
BF16 = jnp.bfloat16

D_MODEL = 2048
CHUNK = 64
HEAD_DIM = 64
N_Q_HEADS = 16
N_KV_HEADS = 4
Q_PER_KV = N_Q_HEADS // N_KV_HEADS
WINDOW = 128
N_RET_HEADS = 8
RET_DK = 128
RET_DV = 128
ROPE_BASE = 10000.0
GM_CHUNK = 128
GM_GROUPS = 8
GM_GROUP_DIM = D_MODEL // GM_GROUPS
D_FF = 5632
CONV_W = 3
EPS = 1e-6
PAST_LEN = 1024
EVEN_IN = 5632

SUBLANES = 8
VMEM_LIMIT_BYTES = 56 * 1024 * 1024

COL_BLK = 512
KV_BLK = 2
RQ_BLK, RK_BLK, RV_BLK, RG_BLK = 3, 5, 7, 9
RET_HEADS_PER_BLK = COL_BLK // RET_DK
RET_CHUNK = 256
SUB_COLS = 256
EPI_ROWS = 64


def _params(*sem):
    return pltpu.CompilerParams(dimension_semantics=sem, vmem_limit_bytes=VMEM_LIMIT_BYTES)


def _rmsnorm_rows(x_ref, g_ref, out_ref, rows, chunk):
    g = g_ref[...]

    def body(c, carry):
        r0 = pl.multiple_of(c * chunk, chunk)
        x = x_ref[pl.ds(r0, chunk), :]
        ms = jnp.mean(x * x, axis=-1, keepdims=True)
        out_ref[pl.ds(r0, chunk), :] = ((x * lax.rsqrt(ms + EPS)) * g).astype(out_ref.dtype)
        return carry

    lax.fori_loop(0, rows // chunk, body, 0)


def _rmsnorm_kernel(x_ref, g_ref, o_ref, *, bm):
    _rmsnorm_rows(x_ref, g_ref, o_ref, bm, min(bm, 128))


def _rmsnorm(x, g, *, bm, name):
    m, k = x.shape
    return pl.pallas_call(
        functools.partial(_rmsnorm_kernel, bm=bm),
        out_shape=jax.ShapeDtypeStruct((m, k), BF16),
        grid=(m // bm,),
        in_specs=[pl.BlockSpec((bm, k), lambda i: (i, 0)), pl.BlockSpec((1, k), lambda i: (0, 0))],
        out_specs=pl.BlockSpec((bm, k), lambda i: (i, 0)),
        compiler_params=_params("arbitrary"),
        name=name,
    )(x, g.reshape(1, k))


def _xw_kernel(xn_ref, w_ref, o_ref, wb_ref, *, bm, rc, gelu):
    @pl.when(pl.program_id(1) == 0)
    def _():
        wb_ref[...] = w_ref[...].astype(BF16)

    for c in range(bm // rc):
        y = jnp.dot(xn_ref[c * rc:(c + 1) * rc, :], wb_ref[...], preferred_element_type=F32)
        if gelu:
            y = jax.nn.gelu(y, approximate=True)
        o_ref[c * rc:(c + 1) * rc, :] = y.astype(o_ref.dtype)


def _xw(xn, w, layer, *, bm, bn, gelu, name):
    m, k = xn.shape
    n = w.shape[2]
    rc = min(bm, 256)
    return pl.pallas_call(
        functools.partial(_xw_kernel, bm=bm, rc=rc, gelu=gelu),
        out_shape=jax.ShapeDtypeStruct((m, n), F32),
        grid=(n // bn, m // bm),
        in_specs=[
            pl.BlockSpec((bm, k), lambda j, i: (i, 0)),
            pl.BlockSpec((None, k, bn), lambda j, i: (layer, 0, j)),
        ],
        out_specs=pl.BlockSpec((bm, bn), lambda j, i: (i, j)),
        scratch_shapes=[pltpu.VMEM((k, bn), BF16)],
        compiler_params=_params("arbitrary", "arbitrary"),
        name=name,
    )(xn, w)


def _matmul_norm_res_kernel(*refs, splits, nk, bm, rc, emit_next):
    na = len(splits)
    a_refs = refs[:na]
    if emit_next:
        w_ref, g_ref, r_ref, gn_ref, o_ref, xn_ref, wb_ref = refs[na:]
    else:
        w_ref, g_ref, r_ref, o_ref, wb_ref = refs[na:]
    k = pl.program_id(1)
    wb_ref[...] = w_ref[...].astype(BF16)

    @pl.when(k == 0)
    def _():
        def zero(c, carry):
            r0 = pl.multiple_of(c * 64, 64)
            o_ref[pl.ds(r0, 64), :] = jnp.zeros((64, o_ref.shape[1]), F32)
            return carry

        lax.fori_loop(0, bm // 64, zero, 0)

    def accumulate(a_ref):
        for c in range(bm // rc):
            rows = slice(c * rc, (c + 1) * rc)
            o_ref[rows, :] += jnp.dot(a_ref[rows, :], wb_ref[...], preferred_element_type=F32)

    if na == 1:
        accumulate(a_refs[0])
    else:
        for a_ref, (k0, k1) in zip(a_refs, splits):
            pl.when((k >= k0) & (k < k1))(functools.partial(accumulate, a_ref))

    @pl.when(k == nk - 1)
    def _():
        g = g_ref[...]

        def fin(c, carry):
            r0 = pl.multiple_of(c * 64, 64)
            y = o_ref[pl.ds(r0, 64), :]
            ms = jnp.mean(y * y, axis=-1, keepdims=True)
            h = r_ref[pl.ds(r0, 64), :] + (y * lax.rsqrt(ms + EPS)) * g
            o_ref[pl.ds(r0, 64), :] = h
            if emit_next:
                ms2 = jnp.mean(h * h, axis=-1, keepdims=True)
                xn_ref[pl.ds(r0, 64), :] = ((h * lax.rsqrt(ms2 + EPS)) * gn_ref[...]).astype(BF16)
            return carry

        lax.fori_loop(0, bm // 64, fin, 0)


def _matmul_norm_res(a_list, w, layer, g, resid, g_next, *, bm, bk, name):
    m = a_list[0].shape[0]
    n = w.shape[2]
    splits, k0 = [], 0
    for a in a_list:
        splits.append((k0, k0 + a.shape[1] // bk))
        k0 = splits[-1][1]
    nk = k0
    rc = min(bm, 256)
    emit_next = g_next is not None

    def a_spec(k0, k1):
        return pl.BlockSpec((bm, bk), lambda i, k: (i, jnp.clip(k - k0, 0, k1 - k0 - 1)))

    row_vec = pl.BlockSpec((1, n), lambda i, k: (0, 0))
    tile = pl.BlockSpec((bm, n), lambda i, k: (i, 0))
    out = pl.pallas_call(
        functools.partial(_matmul_norm_res_kernel, splits=tuple(splits), nk=nk, bm=bm, rc=rc,
                          emit_next=emit_next),
        out_shape=((jax.ShapeDtypeStruct((m, n), F32), jax.ShapeDtypeStruct((m, n), BF16)) if emit_next
                   else jax.ShapeDtypeStruct((m, n), F32)),
        grid=(m // bm, nk),
        in_specs=[a_spec(*sp) for sp in splits]
        + [pl.BlockSpec((None, bk, n), lambda i, k: (layer, k, 0)), row_vec, tile]
        + ([row_vec] if emit_next else []),
        out_specs=(tile, tile) if emit_next else tile,
        scratch_shapes=[pltpu.VMEM((bk, n), BF16)],
        compiler_params=_params("arbitrary", "arbitrary"),
        name=name,
    )(*a_list, w, g.reshape(1, n), resid, *([g_next.reshape(1, n)] if emit_next else []))
    return out if emit_next else (out, None)


def _ffn_up_kernel(xn_ref, wa_ref, wg_ref, cwa_ref, cwg_ref, sa_ref, sg_ref,
                   act_ref, la_ref, lg_ref, wb_ref, carry_ref,
                   *, groups, rows_per_group, rc, tiles_per_seq):
    i = pl.program_id(1)
    bm = groups * rows_per_group
    bn = act_ref.shape[1]
    nsub = bn // SUB_COLS
    subs = [slice(u * SUB_COLS, (u + 1) * SUB_COLS) for u in range(nsub)]

    @pl.when(i == 0)
    def _():
        for u in range(nsub):
            wb_ref[u, :, :SUB_COLS] = wa_ref[:, subs[u]].astype(BF16)
            wb_ref[u, :, SUB_COLS:] = wg_ref[:, subs[u]].astype(BF16)

    @pl.when(lax.rem(i, tiles_per_seq) == 0)
    def _():
        carry_ref[0] = sa_ref[...]
        carry_ref[1] = sg_ref[...]

    cws = [jnp.concatenate([cwa_ref[:, subs[u]], cwg_ref[:, subs[u]]], axis=1) for u in range(nsub)]
    piece = min(EPI_ROWS, rows_per_group)
    prev = [None] * nsub
    for c in range(bm // rc):
        xc = xn_ref[c * rc:(c + 1) * rc, :]
        for u in range(nsub):
            h = jnp.dot(xc, wb_ref[u], preferred_element_type=F32)
            for q in range(rc // piece):
                row = c * rc + q * piece
                grp = row // rows_per_group
                hcur = h[q * piece:(q + 1) * piece]
                if row % rows_per_group == 0:
                    prev8 = jnp.concatenate([carry_ref[0, grp, :, subs[u]], carry_ref[1, grp, :, subs[u]]], axis=1)
                else:
                    prev8 = prev[u]
                hext = jnp.concatenate([prev8, hcur], axis=0)
                s1 = pltpu.roll(hext, 1, 0)[SUBLANES:]
                s2 = pltpu.roll(hext, 2, 0)[SUBLANES:]
                cw = cws[u]
                conv = cw[3:4] + s2 * cw[0:1] + s1 * cw[1:2] + hcur * cw[2:3]
                act_ref[row:row + piece, subs[u]] = (
                    jax.nn.gelu(conv[:, SUB_COLS:], approximate=True) * conv[:, :SUB_COLS]).astype(BF16)
                prev[u] = hcur[piece - SUBLANES:]
                if (row + piece) % rows_per_group == 0:
                    la_ref[grp, :, subs[u]] = prev[u][:, :SUB_COLS]
                    lg_ref[grp, :, subs[u]] = prev[u][:, SUB_COLS:]
                    carry_ref[0, grp, :, subs[u]] = prev[u][:, :SUB_COLS]
                    carry_ref[1, grp, :, subs[u]] = prev[u][:, SUB_COLS:]


def _ffn_up(xn, w_up, layer, conv_w, conv_b, state, *, groups, rows_per_group, bn, name):
    m, k = xn.shape
    nseq = state.shape[0]
    bm = groups * rows_per_group
    ni = m // bm
    tiles_per_seq = ni * groups // nseq
    nj = D_FF // bn
    rc = min(bm, 256)
    cw = jnp.concatenate([conv_w, conv_b[None], jnp.zeros((SUBLANES - CONV_W - 1, 2 * D_FF), F32)], axis=0)
    st = jnp.concatenate([jnp.zeros((nseq, SUBLANES - (CONV_W - 1), 2 * D_FF), F32), state], axis=1)
    act, la, lg = pl.pallas_call(
        functools.partial(_ffn_up_kernel, groups=groups, rows_per_group=rows_per_group, rc=rc,
                          tiles_per_seq=tiles_per_seq),
        out_shape=(jax.ShapeDtypeStruct((m, D_FF), BF16),
                   jax.ShapeDtypeStruct((ni, groups, SUBLANES, D_FF), F32),
                   jax.ShapeDtypeStruct((ni, groups, SUBLANES, D_FF), F32)),
        grid=(nj, ni),
        in_specs=[
            pl.BlockSpec((bm, k), lambda j, i: (i, 0)),
            pl.BlockSpec((None, k, bn), lambda j, i: (layer, 0, j)),
            pl.BlockSpec((None, k, bn), lambda j, i: (layer, 0, nj + j)),
            pl.BlockSpec((SUBLANES, bn), lambda j, i: (0, j)),
            pl.BlockSpec((SUBLANES, bn), lambda j, i: (0, nj + j)),
            pl.BlockSpec((groups, SUBLANES, bn), lambda j, i: (i // tiles_per_seq, 0, j)),
            pl.BlockSpec((groups, SUBLANES, bn), lambda j, i: (i // tiles_per_seq, 0, nj + j)),
        ],
        out_specs=(
            pl.BlockSpec((bm, bn), lambda j, i: (i, j)),
            pl.BlockSpec((None, groups, SUBLANES, bn), lambda j, i: (i, 0, 0, j)),
            pl.BlockSpec((None, groups, SUBLANES, bn), lambda j, i: (i, 0, 0, j)),
        ),
        scratch_shapes=[pltpu.VMEM((bn // SUB_COLS, k, 2 * SUB_COLS), BF16),
                        pltpu.VMEM((2, groups, SUBLANES, bn), F32)],
        compiler_params=_params("arbitrary", "arbitrary"),
        name=name,
    )(xn, w_up, w_up, cw, cw, st, st)
    keep = SUBLANES - (CONV_W - 1)
    ends = slice(tiles_per_seq - 1, ni, tiles_per_seq)
    new_state = jnp.concatenate([la[ends, :, keep:], lg[ends, :, keep:]], axis=-1)
    return act, new_state.reshape(nseq, CONV_W - 1, 2 * D_FF)


def _attn_head(q, k, v, sink_ref, h, bias):
    nq = q.shape[0]
    kh = k[:, h * HEAD_DIM:(h + 1) * HEAD_DIM].astype(BF16)
    vh = v[:, h * HEAD_DIM:(h + 1) * HEAD_DIM].astype(BF16)
    qs, sk = [], []
    for gq in range(Q_PER_KV):
        c0 = (h * Q_PER_KV + gq) * HEAD_DIM
        qs.append(q[:, c0:c0 + HEAD_DIM])
        sk.append(jnp.full((nq, 1), sink_ref[h * Q_PER_KV + gq], F32))
    qh = (jnp.concatenate(qs, axis=0) * (HEAD_DIM ** -0.5)).astype(BF16)
    sk = jnp.concatenate(sk, axis=0)
    s = lax.dot_general(qh, kh, (((1,), (1,)), ((), ())), preferred_element_type=F32)
    if bias is not None:
        s = s + bias
    mx = jnp.maximum(jnp.max(s, axis=-1, keepdims=True), sk)
    p = jnp.exp(s - mx)
    den = jnp.sum(p, axis=-1, keepdims=True) + jnp.exp(sk - mx)
    o = jnp.dot((p / den).astype(BF16), vh, preferred_element_type=F32)
    return jnp.concatenate([o[gq * nq:(gq + 1) * nq] for gq in range(Q_PER_KV)], axis=1)


ATTN_UNIT = 2 * CHUNK


def _attn_prompt_kernel(sink_ref, q_ref, kv_ref, kvp_ref, o_ref, bias_ref, *, tq):
    i = pl.program_id(1)
    nk = ATTN_UNIT + WINDOW
    cols = Q_PER_KV * ATTN_UNIT
    kvw = N_KV_HEADS * HEAD_DIM
    r_k = lax.broadcasted_iota(jnp.int32, (nk, cols), 0)
    r_q = lax.broadcasted_iota(jnp.int32, (nk, cols), 1)
    lo = (r_q & (ATTN_UNIT - 1)) & ~(CHUNK - 1)
    band = (r_k >= lo) & (r_k < lo + WINDOW + CHUNK)
    bias_ref[1] = jnp.where(band, 0.0, -jnp.inf)
    first_lo = jnp.where(i == 0, WINDOW, 0)
    bias_ref[0] = jnp.where(band & (r_k >= first_lo), 0.0, -jnp.inf)

    kv_all = jnp.concatenate([kvp_ref[0], kv_ref[0]], axis=0)
    k_all = kv_all[:, :kvw].astype(BF16)
    vt_all = jnp.transpose(kv_all[:, kvw:]).astype(BF16)
    for u in range(tq // ATTN_UNIT):
        rs = slice(u * ATTN_UNIT, (u + 1) * ATTN_UNIT)
        keys = slice(u * ATTN_UNIT, u * ATTN_UNIT + nk)
        k_u = k_all[keys]
        qt = jnp.transpose(q_ref[0, rs, :] * (HEAD_DIM ** -0.5)).astype(BF16)
        bias = bias_ref[min(u, 1)]
        pieces = []
        for h in range(N_KV_HEADS):
            heads = [h * Q_PER_KV + gq for gq in range(Q_PER_KV)]
            qt_h = jnp.concatenate([qt[n * HEAD_DIM:(n + 1) * HEAD_DIM] for n in heads], axis=1)
            parts = [qt_h]
            if h > 0:
                parts.insert(0, jnp.zeros((h * HEAD_DIM, cols), BF16))
            if h < N_KV_HEADS - 1:
                parts.append(jnp.zeros(((N_KV_HEADS - 1 - h) * HEAD_DIM, cols), BF16))
            st = jnp.dot(k_u, jnp.concatenate(parts, axis=0), preferred_element_type=F32) + bias
            sk = jnp.concatenate([jnp.full((1, ATTN_UNIT), sink_ref[n], F32) for n in heads], axis=1)
            mx = jnp.maximum(jnp.max(st, axis=0, keepdims=True), sk)
            p = jnp.exp(st - mx)
            den = jnp.sum(p, axis=0, keepdims=True) + jnp.exp(sk - mx)
            ot = jnp.dot(vt_all[h * HEAD_DIM:(h + 1) * HEAD_DIM, keys], (p / den).astype(BF16),
                         preferred_element_type=F32)
            pieces += [ot[:, gq * ATTN_UNIT:(gq + 1) * ATTN_UNIT] for gq in range(Q_PER_KV)]
        o_ref[0, rs, :] = jnp.transpose(jnp.concatenate(pieces, axis=0)).astype(BF16)


def _attn_prompt(z, sinks, *, tq):
    b, t, _ = z.shape
    per = tq // WINDOW
    return pl.pallas_call(
        functools.partial(_attn_prompt_kernel, tq=tq),
        out_shape=jax.ShapeDtypeStruct((b, t, N_Q_HEADS * HEAD_DIM), BF16),
        grid=(b, t // tq),
        in_specs=[
            pl.BlockSpec(memory_space=pltpu.SMEM),
            pl.BlockSpec((1, tq, 1024), lambda bb, i: (bb, i, 0)),
            pl.BlockSpec((1, tq, COL_BLK), lambda bb, i: (bb, i, KV_BLK)),
            pl.BlockSpec((1, WINDOW, COL_BLK), lambda bb, i: (bb, jnp.maximum(i * per - 1, 0), KV_BLK)),
        ],
        out_specs=pl.BlockSpec((1, tq, 1024), lambda bb, i: (bb, i, 0)),
        scratch_shapes=[pltpu.VMEM((2, ATTN_UNIT + WINDOW, Q_PER_KV * ATTN_UNIT), F32)],
        compiler_params=_params("arbitrary", "arbitrary"),
        name="attn_prompt",
    )(sinks, z, z, z)


def _attn_sample_kernel(sink_ref, q_ref, kv_ref, ck_ref, cv_ref, o_ref):
    kv = kv_ref[0]
    k = jnp.concatenate([ck_ref[0], kv[:, :256]], axis=0)
    v = jnp.concatenate([cv_ref[0], kv[:, 256:]], axis=0)
    q = q_ref[0]
    for h in range(N_KV_HEADS):
        o_ref[0, :, h * 256:(h + 1) * 256] = _attn_head(q, k, v, sink_ref, h, None).astype(BF16)


def _attn_sample(z, cache_k, cache_v, sinks):
    b, t, _ = z.shape
    nc = cache_k.shape[1]
    return pl.pallas_call(
        _attn_sample_kernel,
        out_shape=jax.ShapeDtypeStruct((b, t, N_Q_HEADS * HEAD_DIM), BF16),
        grid=(b,),
        in_specs=[
            pl.BlockSpec(memory_space=pltpu.SMEM),
            pl.BlockSpec((1, t, 1024), lambda bb: (bb, 0, 0)),
            pl.BlockSpec((1, t, COL_BLK), lambda bb: (bb, 0, KV_BLK)),
            pl.BlockSpec((1, nc, 256), lambda bb: (bb, 0, 0)),
            pl.BlockSpec((1, nc, 256), lambda bb: (bb, 0, 0)),
        ],
        out_specs=pl.BlockSpec((1, t, 1024), lambda bb: (bb, 0, 0)),
        compiler_params=_params("arbitrary"),
        name="attn_sample",
    )(sinks, z, z, cache_k.reshape(b, nc, 256), cache_v.reshape(b, nc, 256))


def _retention_kernel(rq_ref, rk_ref, rv_ref, rg_ref, cs_ref, sn_ref, intra_ref, qd_ref, kd_ref, cd_ref,
                      ng_ref, s0_ref, o_ref, st_ref):
    c = pl.program_id(2)

    @pl.when(c == 0)
    def _():
        st_ref[...] = s0_ref[...]

    cs = cs_ref[...]
    sn = sn_ref[...]
    for hh in range(RET_HEADS_PER_BLK):
        lanes = slice(hh * RET_DK, (hh + 1) * RET_DK)
        q = rq_ref[0, :, lanes]
        k = rk_ref[0, :, lanes]
        qr = q * cs + pltpu.roll(q, RET_DK // 2, 1) * sn
        kr = (k * cs + pltpu.roll(k, RET_DK // 2, 1) * sn) * (RET_DK ** -0.5)
        qb = qr.astype(BF16)
        kb = kr.astype(BF16)
        vb = rv_ref[0, :, lanes].astype(BF16)
        sc = lax.dot_general(qb, kb, (((1,), (1,)), ((), ())), preferred_element_type=F32) * intra_ref[hh]
        inner = jnp.dot(sc.astype(BF16), vb, preferred_element_type=F32)
        state = st_ref[0, hh]
        cross = jnp.dot(qb, state.astype(BF16), preferred_element_type=F32) * qd_ref[hh]
        kdt = jnp.transpose(kr * kd_ref[hh]).astype(BF16)
        st_ref[0, hh] = cd_ref[hh, 0:1, :] * state + jnp.dot(kdt, vb, preferred_element_type=F32)
        r = inner + cross
        mu = jnp.mean(r, axis=-1, keepdims=True)
        yc = r - mu
        yn = yc * lax.rsqrt(jnp.mean(yc * yc, axis=-1, keepdims=True) + EPS)
        o_ref[0, :, lanes] = ((yn * ng_ref[:, lanes]) * jax.nn.silu(rg_ref[0, :, lanes])).astype(BF16)


def _ret_log_gamma():
    return jnp.log1p(-jnp.exp2(-5.0 - jnp.arange(N_RET_HEADS, dtype=F32)))


def _retention(z, pos, state0, norm_g, *, chunk):
    b, t, _ = z.shape
    log_g = _ret_log_gamma()
    idx = jnp.arange(chunk, dtype=F32)
    diff = idx[:, None] - idx[None, :]
    intra = jnp.where(diff[None] >= 0.0,
                      jnp.exp(log_g[:, None, None] * jnp.maximum(diff, 0.0)[None]), 0.0)
    ones = jnp.ones((1, 1, RET_DV), F32)
    q_decay = jnp.exp(log_g[:, None] * (idx[None, :] + 1.0))[:, :, None] * ones
    k_decay = jnp.exp(log_g[:, None] * (chunk - 1.0 - idx)[None, :])[:, :, None] * ones
    c_decay = jnp.exp(log_g * chunk)[:, None, None] * jnp.ones((1, SUBLANES, RET_DV), F32)
    half = RET_DK // 2
    freq = 1.0 / (ROPE_BASE ** (jnp.arange(half, dtype=F32) / half))
    ang = pos.astype(F32)[:, None] * freq[None, :]
    cos, sin = jnp.cos(ang), jnp.sin(ang)
    cs = jnp.concatenate([cos, cos], axis=-1)
    sn = jnp.concatenate([-sin, sin], axis=-1)
    nhb = N_RET_HEADS // RET_HEADS_PER_BLK
    hb = RET_HEADS_PER_BLK

    def zspec(blk):
        return pl.BlockSpec((1, chunk, COL_BLK), lambda bb, g, c: (bb, c, blk + g))

    def tab(rows):
        return pl.BlockSpec((hb, rows, RET_DV), lambda bb, g, c: (g, 0, 0))

    out, st = pl.pallas_call(
        _retention_kernel,
        out_shape=(jax.ShapeDtypeStruct((b, t, N_RET_HEADS * RET_DV), BF16),
                   jax.ShapeDtypeStruct((b, N_RET_HEADS, RET_DK, RET_DV), F32)),
        grid=(b, nhb, t // chunk),
        in_specs=[
            zspec(RQ_BLK), zspec(RK_BLK), zspec(RV_BLK), zspec(RG_BLK),
            pl.BlockSpec((chunk, RET_DK), lambda bb, g, c: (c, 0)),
            pl.BlockSpec((chunk, RET_DK), lambda bb, g, c: (c, 0)),
            pl.BlockSpec((hb, chunk, chunk), lambda bb, g, c: (g, 0, 0)),
            tab(chunk), tab(chunk), tab(SUBLANES),
            pl.BlockSpec((1, COL_BLK), lambda bb, g, c: (0, g)),
            pl.BlockSpec((1, hb, RET_DK, RET_DV), lambda bb, g, c: (bb, g, 0, 0)),
        ],
        out_specs=(
            pl.BlockSpec((1, chunk, COL_BLK), lambda bb, g, c: (bb, c, g)),
            pl.BlockSpec((1, hb, RET_DK, RET_DV), lambda bb, g, c: (bb, g, 0, 0)),
        ),
        compiler_params=_params("arbitrary", "arbitrary", "arbitrary"),
        name="retention",
    )(z, z, z, z, cs, sn, intra, q_decay, k_decay, c_decay, norm_g.reshape(1, -1), state0)
    return out, st


def _gate_kernel(u_ref, v_ref, lg_ref, lb_ref, ws_ref, bs_ref, y_ref, vn_ref, *, rows, span):
    ri = lax.broadcasted_iota(jnp.int32, (span, span), 0)
    ci = lax.broadcasted_iota(jnp.int32, (span, span), 1)
    wt = [jnp.where(ri >= ci, ws_ref[g], 0.0).astype(BF16) for g in range(GM_GROUPS)]
    bs = bs_ref[...]
    lg = lg_ref[...]
    lb = lb_ref[...]
    for c in range(rows // span):
        rs = slice(c * span, (c + 1) * span)
        v = v_ref[0, rs, :]
        mu = jnp.mean(v, axis=-1, keepdims=True)
        xc = v - mu
        vn = (xc * lax.rsqrt(jnp.mean(xc * xc, axis=-1, keepdims=True) + EPS)) * lg + lb
        vn_ref[0, rs, :] = vn
        vb = vn.astype(BF16)
        for g in range(GM_GROUPS):
            cols = slice(g * GM_GROUP_DIM, (g + 1) * GM_GROUP_DIM)
            mixed = jnp.dot(wt[g], vb[:, cols], preferred_element_type=F32) + bs[:, g:g + 1]
            y_ref[0, rs, cols] = (u_ref[0, rs, cols] * mixed).astype(BF16)


def _spatial_gate(uv, ln_g, ln_b, ws, bs, *, rows):
    b, t, _ = uv.shape
    span = min(t, GM_CHUNK)
    ws_l = ws[:, :span, :span]
    bs_t = jnp.transpose(bs[:, :span])
    return pl.pallas_call(
        functools.partial(_gate_kernel, rows=rows, span=span),
        out_shape=(jax.ShapeDtypeStruct((b, t, D_MODEL), BF16),
                   jax.ShapeDtypeStruct((b, t, D_MODEL), F32)),
        grid=(b, t // rows),
        in_specs=[
            pl.BlockSpec((1, rows, D_MODEL), lambda bb, i: (bb, i, 0)),
            pl.BlockSpec((1, rows, D_MODEL), lambda bb, i: (bb, i, 1)),
            pl.BlockSpec((1, D_MODEL), lambda bb, i: (0, 0)),
            pl.BlockSpec((1, D_MODEL), lambda bb, i: (0, 0)),
            pl.BlockSpec((GM_GROUPS, span, span), lambda bb, i: (0, 0, 0)),
            pl.BlockSpec((span, GM_GROUPS), lambda bb, i: (0, 0)),
        ],
        out_specs=(
            pl.BlockSpec((1, rows, D_MODEL), lambda bb, i: (bb, i, 0)),
            pl.BlockSpec((1, rows, D_MODEL), lambda bb, i: (bb, i, 0)),
        ),
        compiler_params=_params("arbitrary", "arbitrary"),
        name="spatial_gate",
    )(uv, uv, ln_g.reshape(1, -1), ln_b.reshape(1, -1), ws_l, bs_t)


def _conv_ffn(h, xn, g3, g_next, w_up, layer, conv_w, conv_b, w_down, state, *, groups, rows_per_group, bm, tag):
    act, new_state = _ffn_up(xn, w_up, layer, conv_w, conv_b, state, groups=groups,
                             rows_per_group=rows_per_group, bn=512, name="ffn_up_" + tag)
    out, xn_next = _matmul_norm_res([act], w_down, layer, g3, h, g_next, bm=bm, bk=512, name="ffn_down_" + tag)
    return out, xn_next, new_state


def kernel(x_prompt, x_sample, cache_swa_k, cache_swa_v, state_ret, state_ffn_conv, norm_g, w_in_even,
           w_out_even, attn_sinks, ret_norm_g, w_in_odd, w_out_odd, gm_ln_g, gm_ln_b, gm_ws, gm_bs,
           ffn_w_up, ffn_conv_w, ffn_conv_b, ffn_w_down):
    nb, seq, d = x_prompt.shape
    db, dseq, _ = x_sample.shape
    depth = norm_g.shape[0]
    hp = x_prompt.reshape(nb * seq, d)
    hs = x_sample.reshape(db * dseq, d)
    bm_p, bm_s = 1024, db * dseq
    pos_p = jnp.arange(seq)
    pos_s = PAST_LEN + jnp.arange(dseq)
    kp_l, vp_l, rp_l, cp_l = [], [], [], []
    ks_l, vs_l, rs_l, cs_l, gv_l = [], [], [], [], []
    xp = _rmsnorm(hp, norm_g[0, 0], bm=512, name="norm_in_p")
    xs = _rmsnorm(hs, norm_g[0, 0], bm=bm_s, name="norm_in_s")
    for layer in range(depth):
        g = norm_g[layer]
        g_next = norm_g[layer + 1, 0] if layer + 1 < depth else None
        if layer % 2 == 0:
            e = layer // 2
            zp = _xw(xp, w_in_even, e, bm=bm_p, bn=COL_BLK, gelu=False, name="in_even_p")
            zs = _xw(xs, w_in_even, e, bm=bm_s, bn=COL_BLK, gelu=False, name="in_even_s")
            zp = zp.reshape(nb, seq, EVEN_IN)
            zs = zs.reshape(db, dseq, EVEN_IN)
            attn_p = _attn_prompt(zp, attn_sinks[e], tq=256)
            attn_s = _attn_sample(zs, cache_swa_k[e], cache_swa_v[e], attn_sinks[e])
            ret_p, r_p = _retention(zp, pos_p, jnp.zeros((nb, N_RET_HEADS, RET_DK, RET_DV), F32),
                                    ret_norm_g[e], chunk=RET_CHUNK)
            ret_s, r_s = _retention(zs, pos_s, state_ret[e].astype(F32), ret_norm_g[e], chunk=dseq)
            mixed_p = [attn_p.reshape(nb * seq, -1), ret_p.reshape(nb * seq, -1)]
            mixed_s = [attn_s.reshape(db * dseq, -1), ret_s.reshape(db * dseq, -1)]
            hp, xp = _matmul_norm_res(mixed_p, w_out_even, e, g[1], hp, g[2], bm=bm_p, bk=512, name="out_even_p")
            hs, xs = _matmul_norm_res(mixed_s, w_out_even, e, g[1], hs, g[2], bm=bm_s, bk=512, name="out_even_s")
            k_new = zs[:, :, 1024:1280].reshape(db, dseq, N_KV_HEADS, HEAD_DIM)
            v_new = zs[:, :, 1280:1536].reshape(db, dseq, N_KV_HEADS, HEAD_DIM)
            n_keep = cache_swa_k.shape[2]
            kp_l.append(zp[:, seq - WINDOW:, 1024:1280].reshape(nb, WINDOW, N_KV_HEADS, HEAD_DIM))
            vp_l.append(zp[:, seq - WINDOW:, 1280:1536].reshape(nb, WINDOW, N_KV_HEADS, HEAD_DIM))
            ks_l.append(jnp.concatenate([cache_swa_k[e], k_new], axis=1)[:, -n_keep:])
            vs_l.append(jnp.concatenate([cache_swa_v[e], v_new], axis=1)[:, -n_keep:])
            rp_l.append(r_p)
            rs_l.append(r_s.astype(state_ret.dtype))
        else:
            o = layer // 2
            uvp = _xw(xp, w_in_odd, o, bm=bm_p, bn=512, gelu=True, name="in_odd_p")
            uvs = _xw(xs, w_in_odd, o, bm=bm_s, bn=512, gelu=True, name="in_odd_s")
            yp, _ = _spatial_gate(uvp.reshape(nb, seq, -1), gm_ln_g[o], gm_ln_b[o], gm_ws[o], gm_bs[o], rows=512)
            ys, gv = _spatial_gate(uvs.reshape(db, dseq, -1), gm_ln_g[o], gm_ln_b[o], gm_ws[o], gm_bs[o],
                                   rows=dseq)
            hp, xp = _matmul_norm_res([yp.reshape(nb * seq, -1)], w_out_odd, o, g[1], hp, g[2], bm=bm_p, bk=512,
                                      name="out_odd_p")
            hs, xs = _matmul_norm_res([ys.reshape(db * dseq, -1)], w_out_odd, o, g[1], hs, g[2], bm=bm_s, bk=512,
                                      name="out_odd_s")
            gv_l.append(gv)
        zero_state = jnp.zeros((nb, CONV_W - 1, 2 * D_FF), F32)
        hp, xp, c_p = _conv_ffn(hp, xp, g[3], g_next, ffn_w_up, layer, ffn_conv_w[layer], ffn_conv_b[layer],
                                ffn_w_down, zero_state, groups=1, rows_per_group=bm_p, bm=bm_p, tag="p")
        hs, xs, c_s = _conv_ffn(hs, xs, g[3], g_next, ffn_w_up, layer, ffn_conv_w[layer], ffn_conv_b[layer],
                                ffn_w_down, state_ffn_conv[layer], groups=db, rows_per_group=dseq,
                                bm=bm_s, tag="s")
        cp_l.append(c_p)
        cs_l.append(c_s)
    return (hp.reshape(nb, seq, d), hs.reshape(db, dseq, d),
            jnp.stack(kp_l), jnp.stack(vp_l), jnp.stack(rp_l), jnp.stack(cp_l),
            jnp.stack(ks_l), jnp.stack(vs_l), jnp.stack(rs_l), jnp.stack(cs_l), jnp.stack(gv_l))
```

```python
import functools

import jax
import jax.numpy as jnp
from jax import lax
from jax.experimental import pallas as pl
from jax.experimental.pallas import tpu as pltpu

F32 = jnp.float32
BF16 = jnp.bfloat16

D_MODEL = 2048
CHUNK = 64
HEAD_DIM = 64
N_Q_HEADS = 16
N_KV_HEADS = 4
Q_PER_KV = N_Q_HEADS // N_KV_HEADS
WINDOW = 128
N_RET_HEADS = 8
RET_DK = 128
RET_DV = 128
ROPE_BASE = 10000.0
GM_CHUNK = 128
GM_GROUPS = 8
GM_GROUP_DIM = D_MODEL // GM_GROUPS
D_FF = 5632
CONV_W = 3
EPS = 1e-6
PAST_LEN = 1024
EVEN_IN = 5632

SUBLANES = 8
VMEM_LIMIT_BYTES = 56 * 1024 * 1024

COL_BLK = 512
KV_BLK = 2
RQ_BLK, RK_BLK, RV_BLK, RG_BLK = 3, 5, 7, 9
RET_HEADS_PER_BLK = COL_BLK // RET_DK
RET_CHUNK = 256
SUB_COLS = 256
EPI_ROWS = 64


def _params(*sem):
    return pltpu.CompilerParams(dimension_semantics=sem, vmem_limit_bytes=VMEM_LIMIT_BYTES)


def _rmsnorm_rows(x_ref, g_ref, out_ref, rows, chunk):
    g = g_ref[...]

    def body(c, carry):
        r0 = pl.multiple_of(c * chunk, chunk)
        x = x_ref[pl.ds(r0, chunk), :]
        ms = jnp.mean(x * x, axis=-1, keepdims=True)
        out_ref[pl.ds(r0, chunk), :] = ((x * lax.rsqrt(ms + EPS)) * g).astype(out_ref.dtype)
        return carry

    lax.fori_loop(0, rows // chunk, body, 0)


def _rmsnorm_kernel(x_ref, g_ref, o_ref, *, bm):
    _rmsnorm_rows(x_ref, g_ref, o_ref, bm, min(bm, 128))


def _rmsnorm(x, g, *, bm, name):
    m, k = x.shape
    return pl.pallas_call(
        functools.partial(_rmsnorm_kernel, bm=bm),
        out_shape=jax.ShapeDtypeStruct((m, k), BF16),
        grid=(m // bm,),
        in_specs=[pl.BlockSpec((bm, k), lambda i: (i, 0)), pl.BlockSpec((1, k), lambda i: (0, 0))],
        out_specs=pl.BlockSpec((bm, k), lambda i: (i, 0)),
        compiler_params=_params("arbitrary"),
        name=name,
    )(x, g.reshape(1, k))


def _xw_kernel(xn_ref, w_ref, o_ref, wb_ref, *, bm, rc, gelu):
    @pl.when(pl.program_id(1) == 0)
    def _():
        wb_ref[...] = w_ref[...].astype(BF16)

    for c in range(bm // rc):
        y = jnp.dot(xn_ref[c * rc:(c + 1) * rc, :], wb_ref[...], preferred_element_type=F32)
        if gelu:
            y = jax.nn.gelu(y, approximate=True)
        o_ref[c * rc:(c + 1) * rc, :] = y.astype(o_ref.dtype)


def _xw(xn, w, layer, *, bm, bn, gelu, name):
    m, k = xn.shape
    n = w.shape[2]
    rc = min(bm, 256)
    return pl.pallas_call(
        functools.partial(_xw_kernel, bm=bm, rc=rc, gelu=gelu),
        out_shape=jax.ShapeDtypeStruct((m, n), F32),
        grid=(n // bn, m // bm),
        in_specs=[
            pl.BlockSpec((bm, k), lambda j, i: (i, 0)),
            pl.BlockSpec((None, k, bn), lambda j, i: (layer, 0, j)),
        ],
        out_specs=pl.BlockSpec((bm, bn), lambda j, i: (i, j)),
        scratch_shapes=[pltpu.VMEM((k, bn), BF16)],
        compiler_params=_params("arbitrary", "arbitrary"),
        name=name,
    )(xn, w)


def _matmul_norm_res_kernel(*refs, splits, nk, bm, rc, emit_next, weights):
    na = len(splits)
    a_refs = refs[:na]
    rest = list(refs[na:])
    wb_ref = rest.pop() if weights != "bf16" else None
    if emit_next:
        w_ref, g_ref, r_ref, gn_ref, o_ref, xn_ref = rest
    else:
        w_ref, g_ref, r_ref, o_ref = rest
    i = pl.program_id(0)
    k = pl.program_id(1)
    if weights == "stream":
        wb_ref[...] = w_ref[...].astype(BF16)
        w_blk = wb_ref
    elif weights == "resident":
        @pl.when(i == 0)
        def _():
            wb_ref[k] = w_ref[...].astype(BF16)

        w_blk = wb_ref.at[k]
    else:
        w_blk = w_ref

    @pl.when(k == 0)
    def _():
        def zero(c, carry):
            r0 = pl.multiple_of(c * 64, 64)
            o_ref[pl.ds(r0, 64), :] = jnp.zeros((64, o_ref.shape[1]), F32)
            return carry

        lax.fori_loop(0, bm // 64, zero, 0)

    def accumulate(a_ref):
        for c in range(bm // rc):
            rows = slice(c * rc, (c + 1) * rc)
            o_ref[rows, :] += jnp.dot(a_ref[rows, :], w_blk[...], preferred_element_type=F32)

    if na == 1:
        accumulate(a_refs[0])
    else:
        for a_ref, (k0, k1) in zip(a_refs, splits):
            pl.when((k >= k0) & (k < k1))(functools.partial(accumulate, a_ref))

    @pl.when(k == nk - 1)
    def _():
        g = g_ref[...]

        def fin(c, carry):
            r0 = pl.multiple_of(c * 64, 64)
            y = o_ref[pl.ds(r0, 64), :]
            ms = jnp.mean(y * y, axis=-1, keepdims=True)
            h = r_ref[pl.ds(r0, 64), :] + (y * lax.rsqrt(ms + EPS)) * g
            o_ref[pl.ds(r0, 64), :] = h
            if emit_next:
                ms2 = jnp.mean(h * h, axis=-1, keepdims=True)
                xn_ref[pl.ds(r0, 64), :] = ((h * lax.rsqrt(ms2 + EPS)) * gn_ref[...]).astype(BF16)
            return carry

        lax.fori_loop(0, bm // 64, fin, 0)


def _matmul_norm_res(a_list, w, layer, g, resid, g_next, *, bm, bk, resident, name):
    m = a_list[0].shape[0]
    n = w.shape[2]
    splits, k0 = [], 0
    for a in a_list:
        splits.append((k0, k0 + a.shape[1] // bk))
        k0 = splits[-1][1]
    nk = k0
    rc = min(bm, 256)
    emit_next = g_next is not None
    weights = "bf16" if w.dtype == BF16 else ("resident" if resident else "stream")

    def a_spec(k0, k1):
        return pl.BlockSpec((bm, bk), lambda i, k: (i, jnp.clip(k - k0, 0, k1 - k0 - 1)))

    if weights == "resident":
        w_spec = pl.BlockSpec((None, bk, n), lambda i, k: (layer, jnp.where(i == 0, k, nk - 1), 0))
        scratch = [pltpu.VMEM((nk, bk, n), BF16)]
    else:
        w_spec = pl.BlockSpec((None, bk, n), lambda i, k: (layer, k, 0))
        scratch = [pltpu.VMEM((bk, n), BF16)] if weights == "stream" else []
    row_vec = pl.BlockSpec((1, n), lambda i, k: (0, 0))
    tile = pl.BlockSpec((bm, n), lambda i, k: (i, 0))
    out = pl.pallas_call(
        functools.partial(_matmul_norm_res_kernel, splits=tuple(splits), nk=nk, bm=bm, rc=rc,
                          emit_next=emit_next, weights=weights),
        out_shape=((jax.ShapeDtypeStruct((m, n), F32), jax.ShapeDtypeStruct((m, n), BF16)) if emit_next
                   else jax.ShapeDtypeStruct((m, n), F32)),
        grid=(m // bm, nk),
        in_specs=[a_spec(*sp) for sp in splits] + [w_spec, row_vec, tile] + ([row_vec] if emit_next else []),
        out_specs=(tile, tile) if emit_next else tile,
        scratch_shapes=scratch,
        compiler_params=_params("arbitrary", "arbitrary"),
        name=name,
    )(*a_list, w, g.reshape(1, n), resid, *([g_next.reshape(1, n)] if emit_next else []))
    return out if emit_next else (out, None)


def _ffn_up_kernel(*refs, groups, rows_per_group, rc, tiles_per_seq, cast_down):
    if cast_down:
        (xn_ref, wa_ref, wg_ref, cwa_ref, cwg_ref, sa_ref, sg_ref, wd_ref,
         act_ref, la_ref, lg_ref, wdb_ref, wb_ref, carry_ref) = refs
    else:
        (xn_ref, wa_ref, wg_ref, cwa_ref, cwg_ref, sa_ref, sg_ref,
         act_ref, la_ref, lg_ref, wb_ref, carry_ref) = refs
    i = pl.program_id(1)
    bm = groups * rows_per_group
    bn = act_ref.shape[1]
    nsub = bn // SUB_COLS
    subs = [slice(u * SUB_COLS, (u + 1) * SUB_COLS) for u in range(nsub)]

    @pl.when(i == 0)
    def _():
        for u in range(nsub):
            wb_ref[u, :, :SUB_COLS] = wa_ref[:, subs[u]].astype(BF16)
            wb_ref[u, :, SUB_COLS:] = wg_ref[:, subs[u]].astype(BF16)
        if cast_down:
            wdb_ref[...] = wd_ref[...].astype(BF16)

    @pl.when(lax.rem(i, tiles_per_seq) == 0)
    def _():
        carry_ref[0] = sa_ref[...]
        carry_ref[1] = sg_ref[...]

    cws = [jnp.concatenate([cwa_ref[:, subs[u]], cwg_ref[:, subs[u]]], axis=1) for u in range(nsub)]
    piece = min(EPI_ROWS, rows_per_group)
    prev = [None] * nsub
    for c in range(bm // rc):
        xc = xn_ref[c * rc:(c + 1) * rc, :]
        for u in range(nsub):
            h = jnp.dot(xc, wb_ref[u], preferred_element_type=F32)
            for q in range(rc // piece):
                row = c * rc + q * piece
                grp = row // rows_per_group
                hcur = h[q * piece:(q + 1) * piece]
                if row % rows_per_group == 0:
                    prev8 = jnp.concatenate([carry_ref[0, grp, :, subs[u]], carry_ref[1, grp, :, subs[u]]], axis=1)
                else:
                    prev8 = prev[u]
                hext = jnp.concatenate([prev8, hcur], axis=0)
                s1 = pltpu.roll(hext, 1, 0)[SUBLANES:]
                s2 = pltpu.roll(hext, 2, 0)[SUBLANES:]
                cw = cws[u]
                conv = cw[3:4] + s2 * cw[0:1] + s1 * cw[1:2] + hcur * cw[2:3]
                act_ref[row:row + piece, subs[u]] = (
                    jax.nn.gelu(conv[:, SUB_COLS:], approximate=True) * conv[:, :SUB_COLS]).astype(BF16)
                prev[u] = hcur[piece - SUBLANES:]
                if (row + piece) % rows_per_group == 0:
                    la_ref[grp, :, subs[u]] = prev[u][:, :SUB_COLS]
                    lg_ref[grp, :, subs[u]] = prev[u][:, SUB_COLS:]
                    carry_ref[0, grp, :, subs[u]] = prev[u][:, :SUB_COLS]
                    carry_ref[1, grp, :, subs[u]] = prev[u][:, SUB_COLS:]


def _ffn_up(xn, w_up, layer, conv_w, conv_b, state, w_down, *, groups, rows_per_group, bn, name):
    m, k = xn.shape
    nseq = state.shape[0]
    bm = groups * rows_per_group
    ni = m // bm
    tiles_per_seq = ni * groups // nseq
    nj = D_FF // bn
    rc = min(bm, 256)
    cw = jnp.concatenate([conv_w, conv_b[None], jnp.zeros((SUBLANES - CONV_W - 1, 2 * D_FF), F32)], axis=0)
    st = jnp.concatenate([jnp.zeros((nseq, SUBLANES - (CONV_W - 1), 2 * D_FF), F32), state], axis=1)
    cast_down = w_down is not None
    d_out = w_down.shape[2] if cast_down else 0
    out_shape = [jax.ShapeDtypeStruct((m, D_FF), BF16),
                 jax.ShapeDtypeStruct((ni, groups, SUBLANES, D_FF), F32),
                 jax.ShapeDtypeStruct((ni, groups, SUBLANES, D_FF), F32)]
    in_specs = [
        pl.BlockSpec((bm, k), lambda j, i: (i, 0)),
        pl.BlockSpec((None, k, bn), lambda j, i: (layer, 0, j)),
        pl.BlockSpec((None, k, bn), lambda j, i: (layer, 0, nj + j)),
        pl.BlockSpec((SUBLANES, bn), lambda j, i: (0, j)),
        pl.BlockSpec((SUBLANES, bn), lambda j, i: (0, nj + j)),
        pl.BlockSpec((groups, SUBLANES, bn), lambda j, i: (i // tiles_per_seq, 0, j)),
        pl.BlockSpec((groups, SUBLANES, bn), lambda j, i: (i // tiles_per_seq, 0, nj + j)),
    ]
    out_specs = [
        pl.BlockSpec((bm, bn), lambda j, i: (i, j)),
        pl.BlockSpec((None, groups, SUBLANES, bn), lambda j, i: (i, 0, 0, j)),
        pl.BlockSpec((None, groups, SUBLANES, bn), lambda j, i: (i, 0, 0, j)),
    ]
    operands = [xn, w_up, w_up, cw, cw, st, st]
    if cast_down:
        in_specs.append(pl.BlockSpec((None, bn, d_out), lambda j, i: (layer, j, 0)))
        out_specs.append(pl.BlockSpec((None, bn, d_out), lambda j, i: (0, j, 0)))
        out_shape.append(jax.ShapeDtypeStruct((1, D_FF, d_out), BF16))
        operands.append(w_down)
    outs = pl.pallas_call(
        functools.partial(_ffn_up_kernel, groups=groups, rows_per_group=rows_per_group, rc=rc,
                          tiles_per_seq=tiles_per_seq, cast_down=cast_down),
        out_shape=tuple(out_shape),
        grid=(nj, ni),
        in_specs=in_specs,
        out_specs=tuple(out_specs),
        scratch_shapes=[pltpu.VMEM((bn // SUB_COLS, k, 2 * SUB_COLS), BF16),
                        pltpu.VMEM((2, groups, SUBLANES, bn), F32)],
        compiler_params=_params("arbitrary", "arbitrary"),
        name=name,
    )(*operands)
    act, la, lg = outs[:3]
    keep = SUBLANES - (CONV_W - 1)
    ends = slice(tiles_per_seq - 1, ni, tiles_per_seq)
    new_state = jnp.concatenate([la[ends, :, keep:], lg[ends, :, keep:]], axis=-1)
    return act, new_state.reshape(nseq, CONV_W - 1, 2 * D_FF), (outs[3] if cast_down else None)


def _attn_head(q, k, v, sink_ref, h, bias):
    nq = q.shape[0]
    kh = k[:, h * HEAD_DIM:(h + 1) * HEAD_DIM].astype(BF16)
    vh = v[:, h * HEAD_DIM:(h + 1) * HEAD_DIM].astype(BF16)
    qs, sk = [], []
    for gq in range(Q_PER_KV):
        c0 = (h * Q_PER_KV + gq) * HEAD_DIM
        qs.append(q[:, c0:c0 + HEAD_DIM])
        sk.append(jnp.full((nq, 1), sink_ref[h * Q_PER_KV + gq], F32))
    qh = (jnp.concatenate(qs, axis=0) * (HEAD_DIM ** -0.5)).astype(BF16)
    sk = jnp.concatenate(sk, axis=0)
    s = lax.dot_general(qh, kh, (((1,), (1,)), ((), ())), preferred_element_type=F32)
    if bias is not None:
        s = s + bias
    mx = jnp.maximum(jnp.max(s, axis=-1, keepdims=True), sk)
    p = jnp.exp(s - mx)
    den = jnp.sum(p, axis=-1, keepdims=True) + jnp.exp(sk - mx)
    o = jnp.dot((p / den).astype(BF16), vh, preferred_element_type=F32)
    return jnp.concatenate([o[gq * nq:(gq + 1) * nq] for gq in range(Q_PER_KV)], axis=1)


ATTN_UNIT = 2 * CHUNK


def _attn_prompt_kernel(sink_ref, q_ref, kv_ref, kvp_ref, o_ref, bias_ref, *, tq):
    i = pl.program_id(1)
    nk = ATTN_UNIT + WINDOW
    cols = Q_PER_KV * ATTN_UNIT
    kvw = N_KV_HEADS * HEAD_DIM
    r_k = lax.broadcasted_iota(jnp.int32, (nk, cols), 0)
    r_q = lax.broadcasted_iota(jnp.int32, (nk, cols), 1)
    lo = (r_q & (ATTN_UNIT - 1)) & ~(CHUNK - 1)
    band = (r_k >= lo) & (r_k < lo + WINDOW + CHUNK)
    bias_ref[1] = jnp.where(band, 0.0, -jnp.inf)
    first_lo = jnp.where(i == 0, WINDOW, 0)
    bias_ref[0] = jnp.where(band & (r_k >= first_lo), 0.0, -jnp.inf)

    kv_all = jnp.concatenate([kvp_ref[0], kv_ref[0]], axis=0)
    k_all = kv_all[:, :kvw].astype(BF16)
    vt_all = jnp.transpose(kv_all[:, kvw:]).astype(BF16)
    for u in range(tq // ATTN_UNIT):
        rs = slice(u * ATTN_UNIT, (u + 1) * ATTN_UNIT)
        keys = slice(u * ATTN_UNIT, u * ATTN_UNIT + nk)
        k_u = k_all[keys]
        qt = jnp.transpose(q_ref[0, rs, :] * (HEAD_DIM ** -0.5)).astype(BF16)
        bias = bias_ref[min(u, 1)]
        pieces = []
        for h in range(N_KV_HEADS):
            heads = [h * Q_PER_KV + gq for gq in range(Q_PER_KV)]
            qt_h = jnp.concatenate([qt[n * HEAD_DIM:(n + 1) * HEAD_DIM] for n in heads], axis=1)
            parts = [qt_h]
            if h > 0:
                parts.insert(0, jnp.zeros((h * HEAD_DIM, cols), BF16))
            if h < N_KV_HEADS - 1:
                parts.append(jnp.zeros(((N_KV_HEADS - 1 - h) * HEAD_DIM, cols), BF16))
            st = jnp.dot(k_u, jnp.concatenate(parts, axis=0), preferred_element_type=F32) + bias
            sk = jnp.concatenate([jnp.full((1, ATTN_UNIT), sink_ref[n], F32) for n in heads], axis=1)
            mx = jnp.maximum(jnp.max(st, axis=0, keepdims=True), sk)
            p = jnp.exp(st - mx)
            den = jnp.sum(p, axis=0, keepdims=True) + jnp.exp(sk - mx)
            ot = jnp.dot(vt_all[h * HEAD_DIM:(h + 1) * HEAD_DIM, keys], (p / den).astype(BF16),
                         preferred_element_type=F32)
            pieces += [ot[:, gq * ATTN_UNIT:(gq + 1) * ATTN_UNIT] for gq in range(Q_PER_KV)]
        o_ref[0, rs, :] = jnp.transpose(jnp.concatenate(pieces, axis=0)).astype(BF16)


def _attn_prompt(z, sinks, *, tq):
    b, t, _ = z.shape
    per = tq // WINDOW
    return pl.pallas_call(
        functools.partial(_attn_prompt_kernel, tq=tq),
        out_shape=jax.ShapeDtypeStruct((b, t, N_Q_HEADS * HEAD_DIM), BF16),
        grid=(b, t // tq),
        in_specs=[
            pl.BlockSpec(memory_space=pltpu.SMEM),
            pl.BlockSpec((1, tq, 1024), lambda bb, i: (bb, i, 0)),
            pl.BlockSpec((1, tq, COL_BLK), lambda bb, i: (bb, i, KV_BLK)),
            pl.BlockSpec((1, WINDOW, COL_BLK), lambda bb, i: (bb, jnp.maximum(i * per - 1, 0), KV_BLK)),
        ],
        out_specs=pl.BlockSpec((1, tq, 1024), lambda bb, i: (bb, i, 0)),
        scratch_shapes=[pltpu.VMEM((2, ATTN_UNIT + WINDOW, Q_PER_KV * ATTN_UNIT), F32)],
        compiler_params=_params("arbitrary", "arbitrary"),
        name="attn_prompt",
    )(sinks, z, z, z)


def _attn_sample_kernel(sink_ref, q_ref, kv_ref, ck_ref, cv_ref, o_ref):
    kv = kv_ref[0]
    k = jnp.concatenate([ck_ref[0], kv[:, :256]], axis=0)
    v = jnp.concatenate([cv_ref[0], kv[:, 256:]], axis=0)
    q = q_ref[0]
    for h in range(N_KV_HEADS):
        o_ref[0, :, h * 256:(h + 1) * 256] = _attn_head(q, k, v, sink_ref, h, None).astype(BF16)


def _attn_sample(z, cache_k, cache_v, sinks):
    b, t, _ = z.shape
    nc = cache_k.shape[1]
    return pl.pallas_call(
        _attn_sample_kernel,
        out_shape=jax.ShapeDtypeStruct((b, t, N_Q_HEADS * HEAD_DIM), BF16),
        grid=(b,),
        in_specs=[
            pl.BlockSpec(memory_space=pltpu.SMEM),
            pl.BlockSpec((1, t, 1024), lambda bb: (bb, 0, 0)),
            pl.BlockSpec((1, t, COL_BLK), lambda bb: (bb, 0, KV_BLK)),
            pl.BlockSpec((1, nc, 256), lambda bb: (bb, 0, 0)),
            pl.BlockSpec((1, nc, 256), lambda bb: (bb, 0, 0)),
        ],
        out_specs=pl.BlockSpec((1, t, 1024), lambda bb: (bb, 0, 0)),
        compiler_params=_params("arbitrary"),
        name="attn_sample",
    )(sinks, z, z, cache_k.reshape(b, nc, 256), cache_v.reshape(b, nc, 256))


def _retention_kernel(rq_ref, rk_ref, rv_ref, rg_ref, cs_ref, sn_ref, intra_ref, qd_ref, kd_ref, cd_ref,
                      ng_ref, s0_ref, o_ref, st_ref):
    c = pl.program_id(2)

    @pl.when(c == 0)
    def _():
        st_ref[...] = s0_ref[...]

    cs = cs_ref[...]
    sn = sn_ref[...]
    for hh in range(RET_HEADS_PER_BLK):
        lanes = slice(hh * RET_DK, (hh + 1) * RET_DK)
        q = rq_ref[0, :, lanes]
        k = rk_ref[0, :, lanes]
        qr = q * cs + pltpu.roll(q, RET_DK // 2, 1) * sn
        kr = (k * cs + pltpu.roll(k, RET_DK // 2, 1) * sn) * (RET_DK ** -0.5)
        qb = qr.astype(BF16)
        kb = kr.astype(BF16)
        vb = rv_ref[0, :, lanes].astype(BF16)
        sc = lax.dot_general(qb, kb, (((1,), (1,)), ((), ())), preferred_element_type=F32) * intra_ref[hh]
        inner = jnp.dot(sc.astype(BF16), vb, preferred_element_type=F32)
        state = st_ref[0, hh]
        cross = jnp.dot(qb, state.astype(BF16), preferred_element_type=F32) * qd_ref[hh]
        kdt = jnp.transpose(kr * kd_ref[hh]).astype(BF16)
        st_ref[0, hh] = cd_ref[hh, 0:1, :] * state + jnp.dot(kdt, vb, preferred_element_type=F32)
        r = inner + cross
        mu = jnp.mean(r, axis=-1, keepdims=True)
        yc = r - mu
        yn = yc * lax.rsqrt(jnp.mean(yc * yc, axis=-1, keepdims=True) + EPS)
        o_ref[0, :, lanes] = ((yn * ng_ref[:, lanes]) * jax.nn.silu(rg_ref[0, :, lanes])).astype(BF16)


def _ret_log_gamma():
    return jnp.log1p(-jnp.exp2(-5.0 - jnp.arange(N_RET_HEADS, dtype=F32)))


def _retention(z, pos, state0, norm_g, *, chunk):
    b, t, _ = z.shape
    log_g = _ret_log_gamma()
    idx = jnp.arange(chunk, dtype=F32)
    diff = idx[:, None] - idx[None, :]
    intra = jnp.where(diff[None] >= 0.0,
                      jnp.exp(log_g[:, None, None] * jnp.maximum(diff, 0.0)[None]), 0.0)
    ones = jnp.ones((1, 1, RET_DV), F32)
    q_decay = jnp.exp(log_g[:, None] * (idx[None, :] + 1.0))[:, :, None] * ones
    k_decay = jnp.exp(log_g[:, None] * (chunk - 1.0 - idx)[None, :])[:, :, None] * ones
    c_decay = jnp.exp(log_g * chunk)[:, None, None] * jnp.ones((1, SUBLANES, RET_DV), F32)
    half = RET_DK // 2
    freq = 1.0 / (ROPE_BASE ** (jnp.arange(half, dtype=F32) / half))
    ang = pos.astype(F32)[:, None] * freq[None, :]
    cos, sin = jnp.cos(ang), jnp.sin(ang)
    cs = jnp.concatenate([cos, cos], axis=-1)
    sn = jnp.concatenate([-sin, sin], axis=-1)
    nhb = N_RET_HEADS // RET_HEADS_PER_BLK
    hb = RET_HEADS_PER_BLK

    def zspec(blk):
        return pl.BlockSpec((1, chunk, COL_BLK), lambda bb, g, c: (bb, c, blk + g))

    def tab(rows):
        return pl.BlockSpec((hb, rows, RET_DV), lambda bb, g, c: (g, 0, 0))

    out, st = pl.pallas_call(
        _retention_kernel,
        out_shape=(jax.ShapeDtypeStruct((b, t, N_RET_HEADS * RET_DV), BF16),
                   jax.ShapeDtypeStruct((b, N_RET_HEADS, RET_DK, RET_DV), F32)),
        grid=(b, nhb, t // chunk),
        in_specs=[
            zspec(RQ_BLK), zspec(RK_BLK), zspec(RV_BLK), zspec(RG_BLK),
            pl.BlockSpec((chunk, RET_DK), lambda bb, g, c: (c, 0)),
            pl.BlockSpec((chunk, RET_DK), lambda bb, g, c: (c, 0)),
            pl.BlockSpec((hb, chunk, chunk), lambda bb, g, c: (g, 0, 0)),
            tab(chunk), tab(chunk), tab(SUBLANES),
            pl.BlockSpec((1, COL_BLK), lambda bb, g, c: (0, g)),
            pl.BlockSpec((1, hb, RET_DK, RET_DV), lambda bb, g, c: (bb, g, 0, 0)),
        ],
        out_specs=(
            pl.BlockSpec((1, chunk, COL_BLK), lambda bb, g, c: (bb, c, g)),
            pl.BlockSpec((1, hb, RET_DK, RET_DV), lambda bb, g, c: (bb, g, 0, 0)),
        ),
        compiler_params=_params("arbitrary", "arbitrary", "arbitrary"),
        name="retention",
    )(z, z, z, z, cs, sn, intra, q_decay, k_decay, c_decay, norm_g.reshape(1, -1), state0)
    return out, st


def _gate_kernel(u_ref, v_ref, lg_ref, lb_ref, ws_ref, bs_ref, y_ref, vn_ref, *, rows, span):
    ri = lax.broadcasted_iota(jnp.int32, (span, span), 0)
    ci = lax.broadcasted_iota(jnp.int32, (span, span), 1)
    wt = [jnp.where(ri >= ci, ws_ref[g], 0.0).astype(BF16) for g in range(GM_GROUPS)]
    bs = bs_ref[...]
    lg = lg_ref[...]
    lb = lb_ref[...]
    for c in range(rows // span):
        rs = slice(c * span, (c + 1) * span)
        v = v_ref[0, rs, :]
        mu = jnp.mean(v, axis=-1, keepdims=True)
        xc = v - mu
        vn = (xc * lax.rsqrt(jnp.mean(xc * xc, axis=-1, keepdims=True) + EPS)) * lg + lb
        vn_ref[0, rs, :] = vn
        vb = vn.astype(BF16)
        for g in range(GM_GROUPS):
            cols = slice(g * GM_GROUP_DIM, (g + 1) * GM_GROUP_DIM)
            mixed = jnp.dot(wt[g], vb[:, cols], preferred_element_type=F32) + bs[:, g:g + 1]
            y_ref[0, rs, cols] = (u_ref[0, rs, cols] * mixed).astype(BF16)


def _spatial_gate(uv, ln_g, ln_b, ws, bs, *, rows):
    b, t, _ = uv.shape
    span = min(t, GM_CHUNK)
    ws_l = ws[:, :span, :span]
    bs_t = jnp.transpose(bs[:, :span])
    return pl.pallas_call(
        functools.partial(_gate_kernel, rows=rows, span=span),
        out_shape=(jax.ShapeDtypeStruct((b, t, D_MODEL), BF16),
                   jax.ShapeDtypeStruct((b, t, D_MODEL), F32)),
        grid=(b, t // rows),
        in_specs=[
            pl.BlockSpec((1, rows, D_MODEL), lambda bb, i: (bb, i, 0)),
            pl.BlockSpec((1, rows, D_MODEL), lambda bb, i: (bb, i, 1)),
            pl.BlockSpec((1, D_MODEL), lambda bb, i: (0, 0)),
            pl.BlockSpec((1, D_MODEL), lambda bb, i: (0, 0)),
            pl.BlockSpec((GM_GROUPS, span, span), lambda bb, i: (0, 0, 0)),
            pl.BlockSpec((span, GM_GROUPS), lambda bb, i: (0, 0)),
        ],
        out_specs=(
            pl.BlockSpec((1, rows, D_MODEL), lambda bb, i: (bb, i, 0)),
            pl.BlockSpec((1, rows, D_MODEL), lambda bb, i: (bb, i, 0)),
        ),
        compiler_params=_params("arbitrary", "arbitrary"),
        name="spatial_gate",
    )(uv, uv, ln_g.reshape(1, -1), ln_b.reshape(1, -1), ws_l, bs_t)


ROW_TILE = 1024
FFN_ROW_TILE = 2048
K_BLK = 512
OUT_ROW_TILE = 512
OUT_K_BLK = 1024
IN_EVEN_BLK = 1408
IN_ODD_BLK = 1024
FFN_BLK = 512


def _conv_ffn(hp, xp, hs, xs, g3, g_next, w_up, layer, conv_w, conv_b, w_down, state_p, state_s):
    nseq_p, nseq_s = state_p.shape[0], state_s.shape[0]
    act_p, c_p, wdb = _ffn_up(xp, w_up, layer, conv_w, conv_b, state_p, w_down, groups=1,
                              rows_per_group=FFN_ROW_TILE, bn=FFN_BLK, name="ffn_up_p")
    act_s, c_s, _ = _ffn_up(xs, w_up, layer, conv_w, conv_b, state_s, None, groups=nseq_s,
                            rows_per_group=xs.shape[0] // nseq_s, bn=FFN_BLK, name="ffn_up_s")
    hp, xp = _matmul_norm_res([act_p], wdb, 0, g3, hp, g_next, bm=ROW_TILE, bk=K_BLK, resident=False,
                              name="ffn_down_p")
    hs, xs = _matmul_norm_res([act_s], wdb, 0, g3, hs, g_next, bm=hs.shape[0], bk=K_BLK, resident=False,
                              name="ffn_down_s")
    return hp, xp, c_p, hs, xs, c_s


def kernel(x_prompt, x_sample, cache_swa_k, cache_swa_v, state_ret, state_ffn_conv, norm_g, w_in_even,
           w_out_even, attn_sinks, ret_norm_g, w_in_odd, w_out_odd, gm_ln_g, gm_ln_b, gm_ws, gm_bs,
           ffn_w_up, ffn_conv_w, ffn_conv_b, ffn_w_down):
    nb, seq, d = x_prompt.shape
    db, dseq, _ = x_sample.shape
    depth = norm_g.shape[0]
    hp = x_prompt.reshape(nb * seq, d)
    hs = x_sample.reshape(db * dseq, d)
    bm_p, bm_s = ROW_TILE, db * dseq
    pos_p = jnp.arange(seq)
    pos_s = PAST_LEN + jnp.arange(dseq)
    kp_l, vp_l, rp_l, cp_l = [], [], [], []
    ks_l, vs_l, rs_l, cs_l, gv_l = [], [], [], [], []
    xp = _rmsnorm(hp, norm_g[0, 0], bm=512, name="norm_in_p")
    xs = _rmsnorm(hs, norm_g[0, 0], bm=bm_s, name="norm_in_s")
    for layer in range(depth):
        g = norm_g[layer]
        g_next = norm_g[layer + 1, 0] if layer + 1 < depth else None
        if layer % 2 == 0:
            e = layer // 2
            zp = _xw(xp, w_in_even, e, bm=bm_p, bn=IN_EVEN_BLK, gelu=False, name="in_even_p")
            zs = _xw(xs, w_in_even, e, bm=bm_s, bn=IN_EVEN_BLK, gelu=False, name="in_even_s")
            zp = zp.reshape(nb, seq, EVEN_IN)
            zs = zs.reshape(db, dseq, EVEN_IN)
            attn_p = _attn_prompt(zp, attn_sinks[e], tq=256)
            attn_s = _attn_sample(zs, cache_swa_k[e], cache_swa_v[e], attn_sinks[e])
            ret_p, r_p = _retention(zp, pos_p, jnp.zeros((nb, N_RET_HEADS, RET_DK, RET_DV), F32),
                                    ret_norm_g[e], chunk=RET_CHUNK)
            ret_s, r_s = _retention(zs, pos_s, state_ret[e].astype(F32), ret_norm_g[e], chunk=dseq)
            mixed_p = [attn_p.reshape(nb * seq, -1), ret_p.reshape(nb * seq, -1)]
            mixed_s = [attn_s.reshape(db * dseq, -1), ret_s.reshape(db * dseq, -1)]
            hp, xp = _matmul_norm_res(mixed_p, w_out_even, e, g[1], hp, g[2], bm=OUT_ROW_TILE, bk=OUT_K_BLK,
                                      resident=True, name="out_even_p")
            hs, xs = _matmul_norm_res(mixed_s, w_out_even, e, g[1], hs, g[2], bm=bm_s, bk=OUT_K_BLK,
                                      resident=False, name="out_even_s")
            k_new = zs[:, :, 1024:1280].reshape(db, dseq, N_KV_HEADS, HEAD_DIM)
            v_new = zs[:, :, 1280:1536].reshape(db, dseq, N_KV_HEADS, HEAD_DIM)
            n_keep = cache_swa_k.shape[2]
            kp_l.append(zp[:, seq - WINDOW:, 1024:1280].reshape(nb, WINDOW, N_KV_HEADS, HEAD_DIM))
            vp_l.append(zp[:, seq - WINDOW:, 1280:1536].reshape(nb, WINDOW, N_KV_HEADS, HEAD_DIM))
            ks_l.append(jnp.concatenate([cache_swa_k[e], k_new], axis=1)[:, -n_keep:])
            vs_l.append(jnp.concatenate([cache_swa_v[e], v_new], axis=1)[:, -n_keep:])
            rp_l.append(r_p)
            rs_l.append(r_s.astype(state_ret.dtype))
        else:
            o = layer // 2
            uvp = _xw(xp, w_in_odd, o, bm=bm_p, bn=IN_ODD_BLK, gelu=True, name="in_odd_p")
            uvs = _xw(xs, w_in_odd, o, bm=bm_s, bn=IN_ODD_BLK, gelu=True, name="in_odd_s")
            yp, _ = _spatial_gate(uvp.reshape(nb, seq, -1), gm_ln_g[o], gm_ln_b[o], gm_ws[o], gm_bs[o], rows=512)
            ys, gv = _spatial_gate(uvs.reshape(db, dseq, -1), gm_ln_g[o], gm_ln_b[o], gm_ws[o], gm_bs[o],
                                   rows=dseq)
            hp, xp = _matmul_norm_res([yp.reshape(nb * seq, -1)], w_out_odd, o, g[1], hp, g[2], bm=OUT_ROW_TILE,
                                      bk=OUT_K_BLK, resident=True, name="out_odd_p")
            hs, xs = _matmul_norm_res([ys.reshape(db * dseq, -1)], w_out_odd, o, g[1], hs, g[2], bm=bm_s,
                                      bk=OUT_K_BLK, resident=False, name="out_odd_s")
            gv_l.append(gv)
        zero_state = jnp.zeros((nb, CONV_W - 1, 2 * D_FF), F32)
        hp, xp, c_p, hs, xs, c_s = _conv_ffn(hp, xp, hs, xs, g[3], g_next, ffn_w_up, layer, ffn_conv_w[layer],
                                             ffn_conv_b[layer], ffn_w_down, zero_state, state_ffn_conv[layer])
        cp_l.append(c_p)
        cs_l.append(c_s)
    return (hp.reshape(nb, seq, d), hs.reshape(db, dseq, d),
            jnp.stack(kp_l), jnp.stack(vp_l), jnp.stack(rp_l), jnp.stack(cp_l),
            jnp.stack(ks_l), jnp.stack(vs_l), jnp.stack(rs_l), jnp.stack(cs_l), jnp.stack(gv_l))
```

```python
import functools

import jax
import jax.numpy as jnp
from jax import lax
from jax.experimental import pallas as pl
from jax.experimental.pallas import tpu as pltpu

F32 = jnp.float32
BF16 = jnp.bfloat16

D_MODEL = 2048
CHUNK = 64
HEAD_DIM = 64
N_Q_HEADS = 16
N_KV_HEADS = 4
Q_PER_KV = N_Q_HEADS // N_KV_HEADS
WINDOW = 128
N_RET_HEADS = 8
RET_DK = 128
RET_DV = 128
ROPE_BASE = 10000.0
GM_CHUNK = 128
GM_GROUPS = 8
GM_GROUP_DIM = D_MODEL // GM_GROUPS
D_FF = 5632
CONV_W = 3
EPS = 1e-6
PAST_LEN = 1024
EVEN_IN = 5632

SUBLANES = 8
VMEM_LIMIT_BYTES = 56 * 1024 * 1024

COL_BLK = 512
KV_BLK = 2
RQ_BLK, RK_BLK, RV_BLK, RG_BLK = 3, 5, 7, 9
RET_HEADS_PER_BLK = COL_BLK // RET_DK
RET_CHUNK = 256
SUB_COLS = 256
EPI_ROWS = 64
NORM_ROWS = 64


def _params(*sem):
    return pltpu.CompilerParams(dimension_semantics=sem, vmem_limit_bytes=VMEM_LIMIT_BYTES)


def _rmsnorm_rows(x_ref, g_ref, out_ref, rows, chunk):
    g = g_ref[...]

    def body(c, carry):
        r0 = pl.multiple_of(c * chunk, chunk)
        x = x_ref[pl.ds(r0, chunk), :]
        ms = jnp.mean(x * x, axis=-1, keepdims=True)
        out_ref[pl.ds(r0, chunk), :] = ((x * lax.rsqrt(ms + EPS)) * g).astype(out_ref.dtype)
        return carry

    lax.fori_loop(0, rows // chunk, body, 0)


def _rmsnorm_kernel(x_ref, g_ref, o_ref, *, bm):
    _rmsnorm_rows(x_ref, g_ref, o_ref, bm, min(bm, 128))


def _rmsnorm(x, g, *, bm, name):
    m, k = x.shape
    return pl.pallas_call(
        functools.partial(_rmsnorm_kernel, bm=bm),
        out_shape=jax.ShapeDtypeStruct((m, k), BF16),
        grid=(m // bm,),
        in_specs=[pl.BlockSpec((bm, k), lambda i: (i, 0)), pl.BlockSpec((1, k), lambda i: (0, 0))],
        out_specs=pl.BlockSpec((bm, k), lambda i: (i, 0)),
        compiler_params=_params("arbitrary"),
        name=name,
    )(x, g.reshape(1, k))


def _xw_kernel(xn_ref, w_ref, o_ref, wb_ref, *, bm, rc, gelu):
    @pl.when(pl.program_id(1) == 0)
    def _():
        wb_ref[...] = w_ref[...].astype(BF16)

    for c in range(bm // rc):
        y = jnp.dot(xn_ref[c * rc:(c + 1) * rc, :], wb_ref[...], preferred_element_type=F32)
        if gelu:
            y = jax.nn.gelu(y, approximate=True)
        o_ref[c * rc:(c + 1) * rc, :] = y.astype(o_ref.dtype)


def _xw(xn, w, layer, *, bm, bn, gelu, name):
    m, k = xn.shape
    n = w.shape[2]
    rc = min(bm, 256)
    return pl.pallas_call(
        functools.partial(_xw_kernel, bm=bm, rc=rc, gelu=gelu),
        out_shape=jax.ShapeDtypeStruct((m, n), F32),
        grid=(n // bn, m // bm),
        in_specs=[
            pl.BlockSpec((bm, k), lambda j, i: (i, 0)),
            pl.BlockSpec((None, k, bn), lambda j, i: (layer, 0, j)),
        ],
        out_specs=pl.BlockSpec((bm, bn), lambda j, i: (i, j)),
        scratch_shapes=[pltpu.VMEM((k, bn), BF16)],
        compiler_params=_params("arbitrary", "arbitrary"),
        name=name,
    )(xn, w)


def _matmul_norm_res_kernel(*refs, splits, nk, bm, rc, emit_next, weights):
    na = len(splits)
    a_refs = refs[:na]
    rest = list(refs[na:])
    wb_ref = rest.pop() if weights != "bf16" else None
    if emit_next:
        w_ref, g_ref, r_ref, gn_ref, o_ref, xn_ref = rest
    else:
        w_ref, g_ref, r_ref, o_ref = rest
    i = pl.program_id(0)
    k = pl.program_id(1)
    if weights == "stream":
        wb_ref[...] = w_ref[...].astype(BF16)
        w_blk = wb_ref
    elif weights == "resident":
        @pl.when(i == 0)
        def _():
            wb_ref[k] = w_ref[...].astype(BF16)

        w_blk = wb_ref.at[k]
    else:
        w_blk = w_ref

    def finish(r0, y):
        rows = slice(r0, r0 + NORM_ROWS)
        ms = jnp.mean(y * y, axis=-1, keepdims=True)
        h = r_ref[rows, :] + (y * lax.rsqrt(ms + EPS)) * g_ref[...]
        o_ref[rows, :] = h
        if emit_next:
            ms2 = jnp.mean(h * h, axis=-1, keepdims=True)
            xn_ref[rows, :] = ((h * lax.rsqrt(ms2 + EPS)) * gn_ref[...]).astype(BF16)

    def step(a_ref, first, last):
        for c in range(bm // rc):
            rows = slice(c * rc, (c + 1) * rc)
            y = jnp.dot(a_ref[rows, :], w_blk[...], preferred_element_type=F32)
            if not first:
                y = o_ref[rows, :] + y
            if last:
                for r in range(0, rc, NORM_ROWS):
                    finish(c * rc + r, y[r:r + NORM_ROWS])
            else:
                o_ref[rows, :] = y

    for a_ref, (k0, k1) in zip(a_refs, splits):
        for first, last in sorted({(kk == 0, kk == nk - 1) for kk in range(k0, k1)}):
            ks = [kk for kk in range(k0, k1) if (kk == 0, kk == nk - 1) == (first, last)]
            pl.when((k >= ks[0]) & (k <= ks[-1]))(functools.partial(step, a_ref, first, last))


def _matmul_norm_res(a_list, w, layer, g, resid, g_next, *, bm, bk, resident, name):
    m = a_list[0].shape[0]
    n = w.shape[2]
    splits, k0 = [], 0
    for a in a_list:
        splits.append((k0, k0 + a.shape[1] // bk))
        k0 = splits[-1][1]
    nk = k0
    rc = min(bm, 256)
    emit_next = g_next is not None
    weights = "bf16" if w.dtype == BF16 else ("resident" if resident else "stream")

    def a_spec(k0, k1):
        return pl.BlockSpec((bm, bk), lambda i, k: (i, jnp.clip(k - k0, 0, k1 - k0 - 1)))

    if weights == "resident":
        w_spec = pl.BlockSpec((None, bk, n), lambda i, k: (layer, jnp.where(i == 0, k, nk - 1), 0))
        scratch = [pltpu.VMEM((nk, bk, n), BF16)]
    else:
        w_spec = pl.BlockSpec((None, bk, n), lambda i, k: (layer, k, 0))
        scratch = [pltpu.VMEM((bk, n), BF16)] if weights == "stream" else []
    row_vec = pl.BlockSpec((1, n), lambda i, k: (0, 0))
    tile = pl.BlockSpec((bm, n), lambda i, k: (i, 0))
    out = pl.pallas_call(
        functools.partial(_matmul_norm_res_kernel, splits=tuple(splits), nk=nk, bm=bm, rc=rc,
                          emit_next=emit_next, weights=weights),
        out_shape=((jax.ShapeDtypeStruct((m, n), F32), jax.ShapeDtypeStruct((m, n), BF16)) if emit_next
                   else jax.ShapeDtypeStruct((m, n), F32)),
        grid=(m // bm, nk),
        in_specs=[a_spec(*sp) for sp in splits] + [w_spec, row_vec, tile] + ([row_vec] if emit_next else []),
        out_specs=(tile, tile) if emit_next else tile,
        scratch_shapes=scratch,
        compiler_params=_params("arbitrary", "arbitrary"),
        name=name,
    )(*a_list, w, g.reshape(1, n), resid, *([g_next.reshape(1, n)] if emit_next else []))
    return out if emit_next else (out, None)


def _ffn_up_kernel(*refs, groups, rows_per_group, rc, tiles_per_seq, cast_down):
    if cast_down:
        (xn_ref, wa_ref, wg_ref, cwa_ref, cwg_ref, sa_ref, sg_ref, wd_ref,
         act_ref, la_ref, lg_ref, wdb_ref, wb_ref, carry_ref) = refs
    else:
        (xn_ref, wa_ref, wg_ref, cwa_ref, cwg_ref, sa_ref, sg_ref,
         act_ref, la_ref, lg_ref, wb_ref, carry_ref) = refs
    i = pl.program_id(1)
    bm = groups * rows_per_group
    bn = act_ref.shape[1]
    nsub = bn // SUB_COLS
    subs = [slice(u * SUB_COLS, (u + 1) * SUB_COLS) for u in range(nsub)]

    @pl.when(i == 0)
    def _():
        for u in range(nsub):
            wb_ref[u, :, :SUB_COLS] = wa_ref[:, subs[u]].astype(BF16)
            wb_ref[u, :, SUB_COLS:] = wg_ref[:, subs[u]].astype(BF16)
        if cast_down:
            wdb_ref[...] = wd_ref[...].astype(BF16)

    @pl.when(lax.rem(i, tiles_per_seq) == 0)
    def _():
        carry_ref[0] = sa_ref[...]
        carry_ref[1] = sg_ref[...]

    cws = [jnp.concatenate([cwa_ref[:, subs[u]], cwg_ref[:, subs[u]]], axis=1) for u in range(nsub)]
    piece = min(EPI_ROWS, rows_per_group)
    prev = [None] * nsub
    for c in range(bm // rc):
        xc = xn_ref[c * rc:(c + 1) * rc, :]
        for u in range(nsub):
            h = jnp.dot(xc, wb_ref[u], preferred_element_type=F32)
            for q in range(rc // piece):
                row = c * rc + q * piece
                grp = row // rows_per_group
                hcur = h[q * piece:(q + 1) * piece]
                if row % rows_per_group == 0:
                    prev8 = jnp.concatenate([carry_ref[0, grp, :, subs[u]], carry_ref[1, grp, :, subs[u]]], axis=1)
                else:
                    prev8 = prev[u]
                hext = jnp.concatenate([prev8, hcur], axis=0)
                s1 = pltpu.roll(hext, 1, 0)[SUBLANES:]
                s2 = pltpu.roll(hext, 2, 0)[SUBLANES:]
                cw = cws[u]
                conv = cw[3:4] + s2 * cw[0:1] + s1 * cw[1:2] + hcur * cw[2:3]
                act_ref[row:row + piece, subs[u]] = (
                    jax.nn.gelu(conv[:, SUB_COLS:], approximate=True) * conv[:, :SUB_COLS]).astype(BF16)
                prev[u] = hcur[piece - SUBLANES:]
                if (row + piece) % rows_per_group == 0:
                    la_ref[grp, :, subs[u]] = prev[u][:, :SUB_COLS]
                    lg_ref[grp, :, subs[u]] = prev[u][:, SUB_COLS:]
                    carry_ref[0, grp, :, subs[u]] = prev[u][:, :SUB_COLS]
                    carry_ref[1, grp, :, subs[u]] = prev[u][:, SUB_COLS:]


def _ffn_up(xn, w_up, layer, conv_w, conv_b, state, w_down, *, groups, rows_per_group, bn, name):
    m, k = xn.shape
    nseq = state.shape[0]
    bm = groups * rows_per_group
    ni = m // bm
    tiles_per_seq = ni * groups // nseq
    nj = D_FF // bn
    rc = min(bm, 256)
    cw = jnp.concatenate([conv_w, conv_b[None], jnp.zeros((SUBLANES - CONV_W - 1, 2 * D_FF), F32)], axis=0)
    st = jnp.concatenate([jnp.zeros((nseq, SUBLANES - (CONV_W - 1), 2 * D_FF), F32), state], axis=1)
    cast_down = w_down is not None
    d_out = w_down.shape[2] if cast_down else 0
    out_shape = [jax.ShapeDtypeStruct((m, D_FF), BF16),
                 jax.ShapeDtypeStruct((ni, groups, SUBLANES, D_FF), F32),
                 jax.ShapeDtypeStruct((ni, groups, SUBLANES, D_FF), F32)]
    in_specs = [
        pl.BlockSpec((bm, k), lambda j, i: (i, 0)),
        pl.BlockSpec((None, k, bn), lambda j, i: (layer, 0, j)),
        pl.BlockSpec((None, k, bn), lambda j, i: (layer, 0, nj + j)),
        pl.BlockSpec((SUBLANES, bn), lambda j, i: (0, j)),
        pl.BlockSpec((SUBLANES, bn), lambda j, i: (0, nj + j)),
        pl.BlockSpec((groups, SUBLANES, bn), lambda j, i: (i // tiles_per_seq, 0, j)),
        pl.BlockSpec((groups, SUBLANES, bn), lambda j, i: (i // tiles_per_seq, 0, nj + j)),
    ]
    out_specs = [
        pl.BlockSpec((bm, bn), lambda j, i: (i, j)),
        pl.BlockSpec((None, groups, SUBLANES, bn), lambda j, i: (i, 0, 0, j)),
        pl.BlockSpec((None, groups, SUBLANES, bn), lambda j, i: (i, 0, 0, j)),
    ]
    operands = [xn, w_up, w_up, cw, cw, st, st]
    if cast_down:
        in_specs.append(pl.BlockSpec((None, bn, d_out), lambda j, i: (layer, j, 0)))
        out_specs.append(pl.BlockSpec((None, bn, d_out), lambda j, i: (0, j, 0)))
        out_shape.append(jax.ShapeDtypeStruct((1, D_FF, d_out), BF16))
        operands.append(w_down)
    outs = pl.pallas_call(
        functools.partial(_ffn_up_kernel, groups=groups, rows_per_group=rows_per_group, rc=rc,
                          tiles_per_seq=tiles_per_seq, cast_down=cast_down),
        out_shape=tuple(out_shape),
        grid=(nj, ni),
        in_specs=in_specs,
        out_specs=tuple(out_specs),
        scratch_shapes=[pltpu.VMEM((bn // SUB_COLS, k, 2 * SUB_COLS), BF16),
                        pltpu.VMEM((2, groups, SUBLANES, bn), F32)],
        compiler_params=_params("arbitrary", "arbitrary"),
        name=name,
    )(*operands)
    act, la, lg = outs[:3]
    keep = SUBLANES - (CONV_W - 1)
    ends = slice(tiles_per_seq - 1, ni, tiles_per_seq)
    new_state = jnp.concatenate([la[ends, :, keep:], lg[ends, :, keep:]], axis=-1)
    return act, new_state.reshape(nseq, CONV_W - 1, 2 * D_FF), (outs[3] if cast_down else None)


def _attn_head(q, k, v, sink_ref, h, bias):
    nq = q.shape[0]
    kh = k[:, h * HEAD_DIM:(h + 1) * HEAD_DIM].astype(BF16)
    vh = v[:, h * HEAD_DIM:(h + 1) * HEAD_DIM].astype(BF16)
    qs, sk = [], []
    for gq in range(Q_PER_KV):
        c0 = (h * Q_PER_KV + gq) * HEAD_DIM
        qs.append(q[:, c0:c0 + HEAD_DIM])
        sk.append(jnp.full((nq, 1), sink_ref[h * Q_PER_KV + gq], F32))
    qh = (jnp.concatenate(qs, axis=0) * (HEAD_DIM ** -0.5)).astype(BF16)
    sk = jnp.concatenate(sk, axis=0)
    s = lax.dot_general(qh, kh, (((1,), (1,)), ((), ())), preferred_element_type=F32)
    if bias is not None:
        s = s + bias
    mx = jnp.maximum(jnp.max(s, axis=-1, keepdims=True), sk)
    p = jnp.exp(s - mx)
    den = jnp.sum(p, axis=-1, keepdims=True) + jnp.exp(sk - mx)
    o = jnp.dot((p / den).astype(BF16), vh, preferred_element_type=F32)
    return jnp.concatenate([o[gq * nq:(gq + 1) * nq] for gq in range(Q_PER_KV)], axis=1)


ATTN_UNIT = 2 * CHUNK


def _attn_prompt_kernel(sink_ref, q_ref, kv_ref, kvp_ref, o_ref, bias_ref, *, tq):
    i = pl.program_id(1)
    nk = ATTN_UNIT + WINDOW
    cols = Q_PER_KV * ATTN_UNIT
    kvw = N_KV_HEADS * HEAD_DIM
    r_k = lax.broadcasted_iota(jnp.int32, (nk, cols), 0)
    r_q = lax.broadcasted_iota(jnp.int32, (nk, cols), 1)
    lo = (r_q & (ATTN_UNIT - 1)) & ~(CHUNK - 1)
    band = (r_k >= lo) & (r_k < lo + WINDOW + CHUNK)
    bias_ref[1] = jnp.where(band, 0.0, -jnp.inf)
    first_lo = jnp.where(i == 0, WINDOW, 0)
    bias_ref[0] = jnp.where(band & (r_k >= first_lo), 0.0, -jnp.inf)

    kv_all = jnp.concatenate([kvp_ref[0], kv_ref[0]], axis=0)
    k_all = kv_all[:, :kvw].astype(BF16)
    vt_all = jnp.transpose(kv_all[:, kvw:]).astype(BF16)
    for u in range(tq // ATTN_UNIT):
        rs = slice(u * ATTN_UNIT, (u + 1) * ATTN_UNIT)
        keys = slice(u * ATTN_UNIT, u * ATTN_UNIT + nk)
        k_u = k_all[keys]
        qt = jnp.transpose(q_ref[0, rs, :] * (HEAD_DIM ** -0.5)).astype(BF16)
        bias = bias_ref[min(u, 1)]
        pieces = []
        for h in range(N_KV_HEADS):
            heads = [h * Q_PER_KV + gq for gq in range(Q_PER_KV)]
            qt_h = jnp.concatenate([qt[n * HEAD_DIM:(n + 1) * HEAD_DIM] for n in heads], axis=1)
            parts = [qt_h]
            if h > 0:
                parts.insert(0, jnp.zeros((h * HEAD_DIM, cols), BF16))
            if h < N_KV_HEADS - 1:
                parts.append(jnp.zeros(((N_KV_HEADS - 1 - h) * HEAD_DIM, cols), BF16))
            st = jnp.dot(k_u, jnp.concatenate(parts, axis=0), preferred_element_type=F32) + bias
            sk = jnp.concatenate([jnp.full((1, ATTN_UNIT), sink_ref[n], F32) for n in heads], axis=1)
            mx = jnp.maximum(jnp.max(st, axis=0, keepdims=True), sk)
            p = jnp.exp(st - mx)
            den = jnp.sum(p, axis=0, keepdims=True) + jnp.exp(sk - mx)
            ot = jnp.dot(vt_all[h * HEAD_DIM:(h + 1) * HEAD_DIM, keys], (p / den).astype(BF16),
                         preferred_element_type=F32)
            pieces += [ot[:, gq * ATTN_UNIT:(gq + 1) * ATTN_UNIT] for gq in range(Q_PER_KV)]
        o_ref[0, rs, :] = jnp.transpose(jnp.concatenate(pieces, axis=0)).astype(BF16)


def _attn_prompt(z, sinks, *, tq):
    b, t, _ = z.shape
    per = tq // WINDOW
    return pl.pallas_call(
        functools.partial(_attn_prompt_kernel, tq=tq),
        out_shape=jax.ShapeDtypeStruct((b, t, N_Q_HEADS * HEAD_DIM), BF16),
        grid=(b, t // tq),
        in_specs=[
            pl.BlockSpec(memory_space=pltpu.SMEM),
            pl.BlockSpec((1, tq, 1024), lambda bb, i: (bb, i, 0)),
            pl.BlockSpec((1, tq, COL_BLK), lambda bb, i: (bb, i, KV_BLK)),
            pl.BlockSpec((1, WINDOW, COL_BLK), lambda bb, i: (bb, jnp.maximum(i * per - 1, 0), KV_BLK)),
        ],
        out_specs=pl.BlockSpec((1, tq, 1024), lambda bb, i: (bb, i, 0)),
        scratch_shapes=[pltpu.VMEM((2, ATTN_UNIT + WINDOW, Q_PER_KV * ATTN_UNIT), F32)],
        compiler_params=_params("arbitrary", "arbitrary"),
        name="attn_prompt",
    )(sinks, z, z, z)


def _attn_sample_kernel(sink_ref, q_ref, kv_ref, ck_ref, cv_ref, o_ref):
    kv = kv_ref[0]
    k = jnp.concatenate([ck_ref[0], kv[:, :256]], axis=0)
    v = jnp.concatenate([cv_ref[0], kv[:, 256:]], axis=0)
    q = q_ref[0]
    for h in range(N_KV_HEADS):
        o_ref[0, :, h * 256:(h + 1) * 256] = _attn_head(q, k, v, sink_ref, h, None).astype(BF16)


def _attn_sample(z, cache_k, cache_v, sinks):
    b, t, _ = z.shape
    nc = cache_k.shape[1]
    return pl.pallas_call(
        _attn_sample_kernel,
        out_shape=jax.ShapeDtypeStruct((b, t, N_Q_HEADS * HEAD_DIM), BF16),
        grid=(b,),
        in_specs=[
            pl.BlockSpec(memory_space=pltpu.SMEM),
            pl.BlockSpec((1, t, 1024), lambda bb: (bb, 0, 0)),
            pl.BlockSpec((1, t, COL_BLK), lambda bb: (bb, 0, KV_BLK)),
            pl.BlockSpec((1, nc, 256), lambda bb: (bb, 0, 0)),
            pl.BlockSpec((1, nc, 256), lambda bb: (bb, 0, 0)),
        ],
        out_specs=pl.BlockSpec((1, t, 1024), lambda bb: (bb, 0, 0)),
        compiler_params=_params("arbitrary"),
        name="attn_sample",
    )(sinks, z, z, cache_k.reshape(b, nc, 256), cache_v.reshape(b, nc, 256))


def _retention_kernel(rq_ref, rk_ref, rv_ref, rg_ref, cs_ref, sn_ref, intra_ref, qd_ref, kd_ref, cd_ref,
                      ng_ref, s0_ref, o_ref, st_ref):
    c = pl.program_id(2)

    @pl.when(c == 0)
    def _():
        st_ref[...] = s0_ref[...]

    cs = cs_ref[...]
    sn = sn_ref[...]
    for hh in range(RET_HEADS_PER_BLK):
        lanes = slice(hh * RET_DK, (hh + 1) * RET_DK)
        q = rq_ref[0, :, lanes]
        k = rk_ref[0, :, lanes]
        qr = q * cs + pltpu.roll(q, RET_DK // 2, 1) * sn
        kr = (k * cs + pltpu.roll(k, RET_DK // 2, 1) * sn) * (RET_DK ** -0.5)
        qb = qr.astype(BF16)
        kb = kr.astype(BF16)
        vb = rv_ref[0, :, lanes].astype(BF16)
        sc = lax.dot_general(qb, kb, (((1,), (1,)), ((), ())), preferred_element_type=F32) * intra_ref[hh]
        inner = jnp.dot(sc.astype(BF16), vb, preferred_element_type=F32)
        state = st_ref[0, hh]
        cross = jnp.dot(qb, state.astype(BF16), preferred_element_type=F32) * qd_ref[hh]
        kdt = jnp.transpose(kr * kd_ref[hh]).astype(BF16)
        st_ref[0, hh] = cd_ref[hh, 0:1, :] * state + jnp.dot(kdt, vb, preferred_element_type=F32)
        r = inner + cross
        mu = jnp.mean(r, axis=-1, keepdims=True)
        yc = r - mu
        yn = yc * lax.rsqrt(jnp.mean(yc * yc, axis=-1, keepdims=True) + EPS)
        o_ref[0, :, lanes] = ((yn * ng_ref[:, lanes]) * jax.nn.silu(rg_ref[0, :, lanes])).astype(BF16)


def _ret_log_gamma():
    return jnp.log1p(-jnp.exp2(-5.0 - jnp.arange(N_RET_HEADS, dtype=F32)))


def _retention(z, pos, state0, norm_g, *, chunk):
    b, t, _ = z.shape
    log_g = _ret_log_gamma()
    idx = jnp.arange(chunk, dtype=F32)
    diff = idx[:, None] - idx[None, :]
    intra = jnp.where(diff[None] >= 0.0,
                      jnp.exp(log_g[:, None, None] * jnp.maximum(diff, 0.0)[None]), 0.0)
    ones = jnp.ones((1, 1, RET_DV), F32)
    q_decay = jnp.exp(log_g[:, None] * (idx[None, :] + 1.0))[:, :, None] * ones
    k_decay = jnp.exp(log_g[:, None] * (chunk - 1.0 - idx)[None, :])[:, :, None] * ones
    c_decay = jnp.exp(log_g * chunk)[:, None, None] * jnp.ones((1, SUBLANES, RET_DV), F32)
    half = RET_DK // 2
    freq = 1.0 / (ROPE_BASE ** (jnp.arange(half, dtype=F32) / half))
    ang = pos.astype(F32)[:, None] * freq[None, :]
    cos, sin = jnp.cos(ang), jnp.sin(ang)
    cs = jnp.concatenate([cos, cos], axis=-1)
    sn = jnp.concatenate([-sin, sin], axis=-1)
    nhb = N_RET_HEADS // RET_HEADS_PER_BLK
    hb = RET_HEADS_PER_BLK

    def zspec(blk):
        return pl.BlockSpec((1, chunk, COL_BLK), lambda bb, g, c: (bb, c, blk + g))

    def tab(rows):
        return pl.BlockSpec((hb, rows, RET_DV), lambda bb, g, c: (g, 0, 0))

    out, st = pl.pallas_call(
        _retention_kernel,
        out_shape=(jax.ShapeDtypeStruct((b, t, N_RET_HEADS * RET_DV), BF16),
                   jax.ShapeDtypeStruct((b, N_RET_HEADS, RET_DK, RET_DV), F32)),
        grid=(b, nhb, t // chunk),
        in_specs=[
            zspec(RQ_BLK), zspec(RK_BLK), zspec(RV_BLK), zspec(RG_BLK),
            pl.BlockSpec((chunk, RET_DK), lambda bb, g, c: (c, 0)),
            pl.BlockSpec((chunk, RET_DK), lambda bb, g, c: (c, 0)),
            pl.BlockSpec((hb, chunk, chunk), lambda bb, g, c: (g, 0, 0)),
            tab(chunk), tab(chunk), tab(SUBLANES),
            pl.BlockSpec((1, COL_BLK), lambda bb, g, c: (0, g)),
            pl.BlockSpec((1, hb, RET_DK, RET_DV), lambda bb, g, c: (bb, g, 0, 0)),
        ],
        out_specs=(
            pl.BlockSpec((1, chunk, COL_BLK), lambda bb, g, c: (bb, c, g)),
            pl.BlockSpec((1, hb, RET_DK, RET_DV), lambda bb, g, c: (bb, g, 0, 0)),
        ),
        compiler_params=_params("arbitrary", "arbitrary", "arbitrary"),
        name="retention",
    )(z, z, z, z, cs, sn, intra, q_decay, k_decay, c_decay, norm_g.reshape(1, -1), state0)
    return out, st


def _gate_kernel(u_ref, v_ref, lg_ref, lb_ref, ws_ref, bs_ref, y_ref, *maybe_vn_ref, rows, span):
    ri = lax.broadcasted_iota(jnp.int32, (span, span), 0)
    ci = lax.broadcasted_iota(jnp.int32, (span, span), 1)
    wt = [jnp.where(ri >= ci, ws_ref[g], 0.0).astype(BF16) for g in range(GM_GROUPS)]
    bs = bs_ref[...]
    lg = lg_ref[...]
    lb = lb_ref[...]
    for c in range(rows // span):
        rs = slice(c * span, (c + 1) * span)
        v = v_ref[0, rs, :]
        mu = jnp.mean(v, axis=-1, keepdims=True)
        xc = v - mu
        vn = (xc * lax.rsqrt(jnp.mean(xc * xc, axis=-1, keepdims=True) + EPS)) * lg + lb
        for vn_ref in maybe_vn_ref:
            vn_ref[0, rs, :] = vn
        vb = vn.astype(BF16)
        for g in range(GM_GROUPS):
            cols = slice(g * GM_GROUP_DIM, (g + 1) * GM_GROUP_DIM)
            mixed = jnp.dot(wt[g], vb[:, cols], preferred_element_type=F32) + bs[:, g:g + 1]
            y_ref[0, rs, cols] = (u_ref[0, rs, cols] * mixed).astype(BF16)


def _spatial_gate(uv, ln_g, ln_b, ws, bs, *, rows, emit_vn):
    b, t, _ = uv.shape
    span = min(t, GM_CHUNK)
    ws_l = ws[:, :span, :span]
    bs_t = jnp.transpose(bs[:, :span])
    tile = pl.BlockSpec((1, rows, D_MODEL), lambda bb, i: (bb, i, 0))
    outs = pl.pallas_call(
        functools.partial(_gate_kernel, rows=rows, span=span),
        out_shape=(jax.ShapeDtypeStruct((b, t, D_MODEL), BF16),)
        + ((jax.ShapeDtypeStruct((b, t, D_MODEL), F32),) if emit_vn else ()),
        grid=(b, t // rows),
        in_specs=[
            pl.BlockSpec((1, rows, D_MODEL), lambda bb, i: (bb, i, 0)),
            pl.BlockSpec((1, rows, D_MODEL), lambda bb, i: (bb, i, 1)),
            pl.BlockSpec((1, D_MODEL), lambda bb, i: (0, 0)),
            pl.BlockSpec((1, D_MODEL), lambda bb, i: (0, 0)),
            pl.BlockSpec((GM_GROUPS, span, span), lambda bb, i: (0, 0, 0)),
            pl.BlockSpec((span, GM_GROUPS), lambda bb, i: (0, 0)),
        ],
        out_specs=(tile, tile) if emit_vn else (tile,),
        compiler_params=_params("arbitrary", "arbitrary"),
        name="spatial_gate",
    )(uv, uv, ln_g.reshape(1, -1), ln_b.reshape(1, -1), ws_l, bs_t)
    return outs if emit_vn else (outs[0], None)


ROW_TILE = 1024
FFN_ROW_TILE = 2048
K_BLK = 512
OUT_ROW_TILE = 512
OUT_K_BLK = 1024
IN_EVEN_BLK = 1408
IN_ODD_BLK = 1024
FFN_BLK = 512


def _conv_ffn(hp, xp, hs, xs, g3, g_next, w_up, layer, conv_w, conv_b, w_down, state_p, state_s):
    nseq_p, nseq_s = state_p.shape[0], state_s.shape[0]
    act_p, c_p, wdb = _ffn_up(xp, w_up, layer, conv_w, conv_b, state_p, w_down, groups=1,
                              rows_per_group=FFN_ROW_TILE, bn=FFN_BLK, name="ffn_up_p")
    act_s, c_s, _ = _ffn_up(xs, w_up, layer, conv_w, conv_b, state_s, None, groups=nseq_s,
                            rows_per_group=xs.shape[0] // nseq_s, bn=FFN_BLK, name="ffn_up_s")
    hp, xp = _matmul_norm_res([act_p], wdb, 0, g3, hp, g_next, bm=ROW_TILE, bk=K_BLK, resident=False,
                              name="ffn_down_p")
    hs, xs = _matmul_norm_res([act_s], wdb, 0, g3, hs, g_next, bm=hs.shape[0], bk=K_BLK, resident=False,
                              name="ffn_down_s")
    return hp, xp, c_p, hs, xs, c_s


def kernel(x_prompt, x_sample, cache_swa_k, cache_swa_v, state_ret, state_ffn_conv, norm_g, w_in_even,
           w_out_even, attn_sinks, ret_norm_g, w_in_odd, w_out_odd, gm_ln_g, gm_ln_b, gm_ws, gm_bs,
           ffn_w_up, ffn_conv_w, ffn_conv_b, ffn_w_down):
    nb, seq, d = x_prompt.shape
    db, dseq, _ = x_sample.shape
    depth = norm_g.shape[0]
    hp = x_prompt.reshape(nb * seq, d)
    hs = x_sample.reshape(db * dseq, d)
    bm_p, bm_s = ROW_TILE, db * dseq
    pos_p = jnp.arange(seq)
    pos_s = PAST_LEN + jnp.arange(dseq)
    kp_l, vp_l, rp_l, cp_l = [], [], [], []
    ks_l, vs_l, rs_l, cs_l, gv_l = [], [], [], [], []
    xp = _rmsnorm(hp, norm_g[0, 0], bm=512, name="norm_in_p")
    xs = _rmsnorm(hs, norm_g[0, 0], bm=bm_s, name="norm_in_s")
    for layer in range(depth):
        g = norm_g[layer]
        g_next = norm_g[layer + 1, 0] if layer + 1 < depth else None
        if layer % 2 == 0:
            e = layer // 2
            zp = _xw(xp, w_in_even, e, bm=bm_p, bn=IN_EVEN_BLK, gelu=False, name="in_even_p")
            zs = _xw(xs, w_in_even, e, bm=bm_s, bn=IN_EVEN_BLK, gelu=False, name="in_even_s")
            zp = zp.reshape(nb, seq, EVEN_IN)
            zs = zs.reshape(db, dseq, EVEN_IN)
            attn_p = _attn_prompt(zp, attn_sinks[e], tq=256)
            attn_s = _attn_sample(zs, cache_swa_k[e], cache_swa_v[e], attn_sinks[e])
            ret_p, r_p = _retention(zp, pos_p, jnp.zeros((nb, N_RET_HEADS, RET_DK, RET_DV), F32),
                                    ret_norm_g[e], chunk=RET_CHUNK)
            ret_s, r_s = _retention(zs, pos_s, state_ret[e].astype(F32), ret_norm_g[e], chunk=dseq)
            mixed_p = [attn_p.reshape(nb * seq, -1), ret_p.reshape(nb * seq, -1)]
            mixed_s = [attn_s.reshape(db * dseq, -1), ret_s.reshape(db * dseq, -1)]
            hp, xp = _matmul_norm_res(mixed_p, w_out_even, e, g[1], hp, g[2], bm=OUT_ROW_TILE, bk=OUT_K_BLK,
                                      resident=True, name="out_even_p")
            hs, xs = _matmul_norm_res(mixed_s, w_out_even, e, g[1], hs, g[2], bm=bm_s, bk=OUT_K_BLK,
                                      resident=False, name="out_even_s")
            k_new = zs[:, :, 1024:1280].reshape(db, dseq, N_KV_HEADS, HEAD_DIM)
            v_new = zs[:, :, 1280:1536].reshape(db, dseq, N_KV_HEADS, HEAD_DIM)
            n_keep = cache_swa_k.shape[2]
            kp_l.append(zp[:, seq - WINDOW:, 1024:1280].reshape(nb, WINDOW, N_KV_HEADS, HEAD_DIM))
            vp_l.append(zp[:, seq - WINDOW:, 1280:1536].reshape(nb, WINDOW, N_KV_HEADS, HEAD_DIM))
            ks_l.append(jnp.concatenate([cache_swa_k[e], k_new], axis=1)[:, -n_keep:])
            vs_l.append(jnp.concatenate([cache_swa_v[e], v_new], axis=1)[:, -n_keep:])
            rp_l.append(r_p)
            rs_l.append(r_s.astype(state_ret.dtype))
        else:
            o = layer // 2
            uvp = _xw(xp, w_in_odd, o, bm=bm_p, bn=IN_ODD_BLK, gelu=True, name="in_odd_p")
            uvs = _xw(xs, w_in_odd, o, bm=bm_s, bn=IN_ODD_BLK, gelu=True, name="in_odd_s")
            yp, _ = _spatial_gate(uvp.reshape(nb, seq, -1), gm_ln_g[o], gm_ln_b[o], gm_ws[o], gm_bs[o], rows=512,
                                  emit_vn=False)
            ys, gv = _spatial_gate(uvs.reshape(db, dseq, -1), gm_ln_g[o], gm_ln_b[o], gm_ws[o], gm_bs[o],
                                   rows=dseq, emit_vn=True)
            hp, xp = _matmul_norm_res([yp.reshape(nb * seq, -1)], w_out_odd, o, g[1], hp, g[2], bm=OUT_ROW_TILE,
                                      bk=OUT_K_BLK, resident=True, name="out_odd_p")
            hs, xs = _matmul_norm_res([ys.reshape(db * dseq, -1)], w_out_odd, o, g[1], hs, g[2], bm=bm_s,
                                      bk=OUT_K_BLK, resident=False, name="out_odd_s")
            gv_l.append(gv)
        zero_state = jnp.zeros((nb, CONV_W - 1, 2 * D_FF), F32)
        hp, xp, c_p, hs, xs, c_s = _conv_ffn(hp, xp, hs, xs, g[3], g_next, ffn_w_up, layer, ffn_conv_w[layer],
                                             ffn_conv_b[layer], ffn_w_down, zero_state, state_ffn_conv[layer])
        cp_l.append(c_p)
        cs_l.append(c_s)
    return (hp.reshape(nb, seq, d), hs.reshape(db, dseq, d),
            jnp.stack(kp_l), jnp.stack(vp_l), jnp.stack(rp_l), jnp.stack(cp_l),
            jnp.stack(ks_l), jnp.stack(vs_l), jnp.stack(rs_l), jnp.stack(cs_l), jnp.stack(gv_l))
```

```python
import functools

import jax
import jax.numpy as jnp
from jax import lax
from jax.experimental import pallas as pl
from jax.experimental.pallas import tpu as pltpu

F32 = jnp.float32
BF16 = jnp.bfloat16

D_MODEL = 2048
CHUNK = 64
HEAD_DIM = 64
N_Q_HEADS = 16
N_KV_HEADS = 4
Q_PER_KV = N_Q_HEADS // N_KV_HEADS
WINDOW = 128
N_RET_HEADS = 8
RET_DK = 128
RET_DV = 128
ROPE_BASE = 10000.0
GM_CHUNK = 128
GM_GROUPS = 8
GM_GROUP_DIM = D_MODEL // GM_GROUPS
D_FF = 5632
CONV_W = 3
EPS = 1e-6
PAST_LEN = 1024
EVEN_IN = 5632

SUBLANES = 8
VMEM_LIMIT_BYTES = 56 * 1024 * 1024

COL_BLK = 512
KV_BLK = 2
RQ_BLK, RK_BLK, RV_BLK, RG_BLK = 3, 5, 7, 9
RET_HEADS_PER_BLK = COL_BLK // RET_DK
RET_CHUNK = 256
SUB_COLS = 256
EPI_ROWS = 16
NORM_ROWS = 64


def _params(*sem):
    return pltpu.CompilerParams(dimension_semantics=sem, vmem_limit_bytes=VMEM_LIMIT_BYTES)


def _rmsnorm_rows(x_ref, g_ref, out_ref, rows, chunk):
    g = g_ref[...]

    def body(c, carry):
        r0 = pl.multiple_of(c * chunk, chunk)
        x = x_ref[pl.ds(r0, chunk), :]
        ms = jnp.mean(x * x, axis=-1, keepdims=True)
        out_ref[pl.ds(r0, chunk), :] = ((x * lax.rsqrt(ms + EPS)) * g).astype(out_ref.dtype)
        return carry

    lax.fori_loop(0, rows // chunk, body, 0)


def _zero_after(x):
    z = pltpu.bitcast(x, jnp.uint32)
    z = lax.shift_right_logical(lax.shift_right_logical(z, jnp.uint32(16)), jnp.uint32(16))
    return pltpu.bitcast(z, F32)


def _rmsnorm_kernel(x_ref, g_ref, o_ref, *, bm):
    _rmsnorm_rows(x_ref, g_ref, o_ref, bm, min(bm, 128))


def _rmsnorm(x, g, *, bm, name):
    m, k = x.shape
    return pl.pallas_call(
        functools.partial(_rmsnorm_kernel, bm=bm),
        out_shape=jax.ShapeDtypeStruct((m, k), BF16),
        grid=(m // bm,),
        in_specs=[pl.BlockSpec((bm, k), lambda i: (i, 0)), pl.BlockSpec((1, k), lambda i: (0, 0))],
        out_specs=pl.BlockSpec((bm, k), lambda i: (i, 0)),
        compiler_params=_params("arbitrary"),
        name=name,
    )(x, g.reshape(1, k))


def _xw_kernel(xn_ref, w_ref, o_ref, wb_ref, *, bm, rc, gelu):
    @pl.when(pl.program_id(1) == 0)
    def _():
        wb_ref[...] = w_ref[...].astype(BF16)

    for c in range(bm // rc):
        y = jnp.dot(xn_ref[c * rc:(c + 1) * rc, :], wb_ref[...], preferred_element_type=F32)
        if gelu:
            y = jax.nn.gelu(y, approximate=True)
        o_ref[c * rc:(c + 1) * rc, :] = y.astype(o_ref.dtype)


def _xw(xn, w, layer, *, bm, bn, gelu, name):
    m, k = xn.shape
    n = w.shape[2]
    rc = min(bm, 256)
    return pl.pallas_call(
        functools.partial(_xw_kernel, bm=bm, rc=rc, gelu=gelu),
        out_shape=jax.ShapeDtypeStruct((m, n), F32),
        grid=(n // bn, m // bm),
        in_specs=[
            pl.BlockSpec((bm, k), lambda j, i: (i, 0)),
            pl.BlockSpec((None, k, bn), lambda j, i: (layer, 0, j)),
        ],
        out_specs=pl.BlockSpec((bm, bn), lambda j, i: (i, j)),
        scratch_shapes=[pltpu.VMEM((k, bn), BF16)],
        compiler_params=_params("arbitrary", "arbitrary"),
        name=name,
    )(xn, w)


def _matmul_norm_res_kernel(*refs, splits, nk, bm, rc, emit_next, weights):
    na = len(splits)
    a_refs = refs[:na]
    rest = list(refs[na:])
    wb_ref = rest.pop() if weights != "bf16" else None
    if emit_next:
        w_ref, g_ref, r_ref, gn_ref, o_ref, xn_ref = rest
    else:
        w_ref, g_ref, r_ref, o_ref = rest
    i = pl.program_id(0)
    k = pl.program_id(1)
    if weights == "stream":
        wb_ref[...] = w_ref[...].astype(BF16)
        w_blk = wb_ref
    elif weights == "resident":
        @pl.when(i == 0)
        def _():
            wb_ref[k] = w_ref[...].astype(BF16)

        w_blk = wb_ref.at[k]
    else:
        w_blk = w_ref

    def finish(r0, y):
        rows = slice(r0, r0 + NORM_ROWS)
        ms = jnp.mean(y * y, axis=-1, keepdims=True)
        h = r_ref[rows, :] + (y * lax.rsqrt(ms + EPS)) * g_ref[...]
        o_ref[rows, :] = h
        if emit_next:
            ms2 = jnp.mean(h * h, axis=-1, keepdims=True)
            xn_ref[rows, :] = ((h * lax.rsqrt(ms2 + EPS)) * gn_ref[...]).astype(BF16)

    def step(a_ref, first, last):
        for c in range(bm // rc):
            rows = slice(c * rc, (c + 1) * rc)
            y = jnp.dot(a_ref[rows, :], w_blk[...], preferred_element_type=F32)
            if not first:
                y = o_ref[rows, :] + y
            if last:
                for r in range(0, rc, NORM_ROWS):
                    finish(c * rc + r, y[r:r + NORM_ROWS])
            else:
                o_ref[rows, :] = y

    for a_ref, (k0, k1) in zip(a_refs, splits):
        for first, last in sorted({(kk == 0, kk == nk - 1) for kk in range(k0, k1)}):
            ks = [kk for kk in range(k0, k1) if (kk == 0, kk == nk - 1) == (first, last)]
            pl.when((k >= ks[0]) & (k <= ks[-1]))(functools.partial(step, a_ref, first, last))


def _matmul_norm_res(a_list, w, layer, g, resid, g_next, *, bm, bk, resident, name):
    m = a_list[0].shape[0]
    n = w.shape[2]
    splits, k0 = [], 0
    for a in a_list:
        splits.append((k0, k0 + a.shape[1] // bk))
        k0 = splits[-1][1]
    nk = k0
    rc = min(bm, 256)
    emit_next = g_next is not None
    weights = "bf16" if w.dtype == BF16 else ("resident" if resident else "stream")

    def a_spec(k0, k1):
        return pl.BlockSpec((bm, bk), lambda i, k: (i, jnp.clip(k - k0, 0, k1 - k0 - 1)))

    if weights == "resident":
        w_spec = pl.BlockSpec((None, bk, n), lambda i, k: (layer, jnp.where(i == 0, k, nk - 1), 0))
        scratch = [pltpu.VMEM((nk, bk, n), BF16)]
    else:
        w_spec = pl.BlockSpec((None, bk, n), lambda i, k: (layer, k, 0))
        scratch = [pltpu.VMEM((bk, n), BF16)] if weights == "stream" else []
    row_vec = pl.BlockSpec((1, n), lambda i, k: (0, 0))
    tile = pl.BlockSpec((bm, n), lambda i, k: (i, 0))
    out = pl.pallas_call(
        functools.partial(_matmul_norm_res_kernel, splits=tuple(splits), nk=nk, bm=bm, rc=rc,
                          emit_next=emit_next, weights=weights),
        out_shape=((jax.ShapeDtypeStruct((m, n), F32), jax.ShapeDtypeStruct((m, n), BF16)) if emit_next
                   else jax.ShapeDtypeStruct((m, n), F32)),
        grid=(m // bm, nk),
        in_specs=[a_spec(*sp) for sp in splits] + [w_spec, row_vec, tile] + ([row_vec] if emit_next else []),
        out_specs=(tile, tile) if emit_next else tile,
        scratch_shapes=scratch,
        compiler_params=_params("arbitrary", "arbitrary"),
        name=name,
    )(*a_list, w, g.reshape(1, n), resid, *([g_next.reshape(1, n)] if emit_next else []))
    return out if emit_next else (out, None)


def _ffn_up_kernel(*refs, groups, rows_per_group, rc, tiles_per_seq, cast_down):
    if cast_down:
        (xn_ref, wa_ref, wg_ref, cwa_ref, cwg_ref, sa_ref, sg_ref, wd_ref,
         act_ref, la_ref, lg_ref, wdb_ref, wb_ref, carry_ref) = refs
    else:
        (xn_ref, wa_ref, wg_ref, cwa_ref, cwg_ref, sa_ref, sg_ref,
         act_ref, la_ref, lg_ref, wb_ref, carry_ref) = refs
    i = pl.program_id(1)
    bm = groups * rows_per_group
    bn = act_ref.shape[1]
    nsub = bn // SUB_COLS
    subs = [slice(u * SUB_COLS, (u + 1) * SUB_COLS) for u in range(nsub)]

    @pl.when(i == 0)
    def _():
        for u in range(nsub):
            wb_ref[u, :, :SUB_COLS] = wa_ref[:, subs[u]].astype(BF16)
            wb_ref[u, :, SUB_COLS:] = wg_ref[:, subs[u]].astype(BF16)
        if cast_down:
            wdb_ref[...] = wd_ref[...].astype(BF16)

    @pl.when(lax.rem(i, tiles_per_seq) == 0)
    def _():
        carry_ref[0] = sa_ref[...]
        carry_ref[1] = sg_ref[...]

    cws = [jnp.concatenate([cwa_ref[:, subs[u]], cwg_ref[:, subs[u]]], axis=1) for u in range(nsub)]
    piece = min(EPI_ROWS, rows_per_group)
    prev = [None] * nsub
    after = [None] * nsub
    for c in range(bm // rc):
        xc = xn_ref[c * rc:(c + 1) * rc, :]
        for u in range(nsub):
            h = jnp.dot(xc, wb_ref[u], preferred_element_type=F32)
            for q in range(rc // piece):
                row = c * rc + q * piece
                grp = row // rows_per_group
                hcur = h[q * piece:(q + 1) * piece]
                if row % rows_per_group == 0:
                    prev8 = jnp.concatenate([carry_ref[0, grp, :, subs[u]], carry_ref[1, grp, :, subs[u]]], axis=1)
                else:
                    prev8 = prev[u]
                hext = jnp.concatenate([prev8, hcur], axis=0)
                s1 = pltpu.roll(hext, 1, 0)[SUBLANES:]
                s2 = pltpu.roll(hext, 2, 0)[SUBLANES:]
                cw = cws[u] if after[u] is None else cws[u] + jnp.concatenate([after[u]] * 2, axis=1)
                conv = cw[3:4] + s2 * cw[0:1] + s1 * cw[1:2] + hcur * cw[2:3]
                act = jax.nn.gelu(conv[:, SUB_COLS:], approximate=True) * conv[:, :SUB_COLS]
                act_ref[row:row + piece, subs[u]] = act.astype(BF16)
                after[u] = _zero_after(act[piece - SUBLANES:])
                prev[u] = hcur[piece - SUBLANES:]
                if (row + piece) % rows_per_group == 0:
                    la_ref[grp, :, subs[u]] = prev[u][:, :SUB_COLS]
                    lg_ref[grp, :, subs[u]] = prev[u][:, SUB_COLS:]
                    carry_ref[0, grp, :, subs[u]] = prev[u][:, :SUB_COLS]
                    carry_ref[1, grp, :, subs[u]] = prev[u][:, SUB_COLS:]


def _ffn_up(xn, w_up, layer, conv_w, conv_b, state, w_down, *, groups, rows_per_group, bn, name):
    m, k = xn.shape
    nseq = state.shape[0]
    bm = groups * rows_per_group
    ni = m // bm
    tiles_per_seq = ni * groups // nseq
    nj = D_FF // bn
    rc = min(bm, 256)
    cw = jnp.concatenate([conv_w, conv_b[None], jnp.zeros((SUBLANES - CONV_W - 1, 2 * D_FF), F32)], axis=0)
    st = jnp.concatenate([jnp.zeros((nseq, SUBLANES - (CONV_W - 1), 2 * D_FF), F32), state], axis=1)
    cast_down = w_down is not None
    d_out = w_down.shape[2] if cast_down else 0
    out_shape = [jax.ShapeDtypeStruct((m, D_FF), BF16),
                 jax.ShapeDtypeStruct((ni, groups, SUBLANES, D_FF), F32),
                 jax.ShapeDtypeStruct((ni, groups, SUBLANES, D_FF), F32)]
    in_specs = [
        pl.BlockSpec((bm, k), lambda j, i: (i, 0)),
        pl.BlockSpec((None, k, bn), lambda j, i: (layer, 0, j)),
        pl.BlockSpec((None, k, bn), lambda j, i: (layer, 0, nj + j)),
        pl.BlockSpec((SUBLANES, bn), lambda j, i: (0, j)),
        pl.BlockSpec((SUBLANES, bn), lambda j, i: (0, nj + j)),
        pl.BlockSpec((groups, SUBLANES, bn), lambda j, i: (i // tiles_per_seq, 0, j)),
        pl.BlockSpec((groups, SUBLANES, bn), lambda j, i: (i // tiles_per_seq, 0, nj + j)),
    ]
    out_specs = [
        pl.BlockSpec((bm, bn), lambda j, i: (i, j)),
        pl.BlockSpec((None, groups, SUBLANES, bn), lambda j, i: (i, 0, 0, j)),
        pl.BlockSpec((None, groups, SUBLANES, bn), lambda j, i: (i, 0, 0, j)),
    ]
    operands = [xn, w_up, w_up, cw, cw, st, st]
    if cast_down:
        in_specs.append(pl.BlockSpec((None, bn, d_out), lambda j, i: (layer, j, 0)))
        out_specs.append(pl.BlockSpec((None, bn, d_out), lambda j, i: (0, j, 0)))
        out_shape.append(jax.ShapeDtypeStruct((1, D_FF, d_out), BF16))
        operands.append(w_down)
    outs = pl.pallas_call(
        functools.partial(_ffn_up_kernel, groups=groups, rows_per_group=rows_per_group, rc=rc,
                          tiles_per_seq=tiles_per_seq, cast_down=cast_down),
        out_shape=tuple(out_shape),
        grid=(nj, ni),
        in_specs=in_specs,
        out_specs=tuple(out_specs),
        scratch_shapes=[pltpu.VMEM((bn // SUB_COLS, k, 2 * SUB_COLS), BF16),
                        pltpu.VMEM((2, groups, SUBLANES, bn), F32)],
        compiler_params=_params("arbitrary", "arbitrary"),
        name=name,
    )(*operands)
    act, la, lg = outs[:3]
    keep = SUBLANES - (CONV_W - 1)
    ends = slice(tiles_per_seq - 1, ni, tiles_per_seq)
    new_state = jnp.concatenate([la[ends, :, keep:], lg[ends, :, keep:]], axis=-1)
    return act, new_state.reshape(nseq, CONV_W - 1, 2 * D_FF), (outs[3] if cast_down else None)


def _attn_head(q, k, v, sink_ref, h, bias):
    nq = q.shape[0]
    kh = k[:, h * HEAD_DIM:(h + 1) * HEAD_DIM].astype(BF16)
    vh = v[:, h * HEAD_DIM:(h + 1) * HEAD_DIM].astype(BF16)
    qs, sk = [], []
    for gq in range(Q_PER_KV):
        c0 = (h * Q_PER_KV + gq) * HEAD_DIM
        qs.append(q[:, c0:c0 + HEAD_DIM])
        sk.append(jnp.full((nq, 1), sink_ref[h * Q_PER_KV + gq], F32))
    qh = (jnp.concatenate(qs, axis=0) * (HEAD_DIM ** -0.5)).astype(BF16)
    sk = jnp.concatenate(sk, axis=0)
    s = lax.dot_general(qh, kh, (((1,), (1,)), ((), ())), preferred_element_type=F32)
    if bias is not None:
        s = s + bias
    mx = jnp.maximum(jnp.max(s, axis=-1, keepdims=True), sk)
    p = jnp.exp(s - mx)
    den = jnp.sum(p, axis=-1, keepdims=True) + jnp.exp(sk - mx)
    o = jnp.dot((p / den).astype(BF16), vh, preferred_element_type=F32)
    return jnp.concatenate([o[gq * nq:(gq + 1) * nq] for gq in range(Q_PER_KV)], axis=1)


ATTN_UNIT = 2 * CHUNK


def _attn_prompt_kernel(sink_ref, q_ref, kv_ref, kvp_ref, o_ref, bias_ref, *, tq):
    i = pl.program_id(1)
    nk = ATTN_UNIT + WINDOW
    cols = Q_PER_KV * ATTN_UNIT
    kvw = N_KV_HEADS * HEAD_DIM
    r_k = lax.broadcasted_iota(jnp.int32, (nk, cols), 0)
    r_q = lax.broadcasted_iota(jnp.int32, (nk, cols), 1)
    lo = (r_q & (ATTN_UNIT - 1)) & ~(CHUNK - 1)
    band = (r_k >= lo) & (r_k < lo + WINDOW + CHUNK)
    bias_ref[1] = jnp.where(band, 0.0, -jnp.inf)
    first_lo = jnp.where(i == 0, WINDOW, 0)
    bias_ref[0] = jnp.where(band & (r_k >= first_lo), 0.0, -jnp.inf)

    kv_all = jnp.concatenate([kvp_ref[0], kv_ref[0]], axis=0)
    k_all = kv_all[:, :kvw].astype(BF16)
    vt_all = jnp.transpose(kv_all[:, kvw:]).astype(BF16)
    for u in range(tq // ATTN_UNIT):
        rs = slice(u * ATTN_UNIT, (u + 1) * ATTN_UNIT)
        keys = slice(u * ATTN_UNIT, u * ATTN_UNIT + nk)
        k_u = k_all[keys]
        qt = jnp.transpose(q_ref[0, rs, :] * (HEAD_DIM ** -0.5)).astype(BF16)
        bias = bias_ref[min(u, 1)]
        pieces = []
        for h in range(N_KV_HEADS):
            heads = [h * Q_PER_KV + gq for gq in range(Q_PER_KV)]
            qt_h = jnp.concatenate([qt[n * HEAD_DIM:(n + 1) * HEAD_DIM] for n in heads], axis=1)
            parts = [qt_h]
            if h > 0:
                parts.insert(0, jnp.zeros((h * HEAD_DIM, cols), BF16))
            if h < N_KV_HEADS - 1:
                parts.append(jnp.zeros(((N_KV_HEADS - 1 - h) * HEAD_DIM, cols), BF16))
            st = jnp.dot(k_u, jnp.concatenate(parts, axis=0), preferred_element_type=F32) + bias
            sk = jnp.concatenate([jnp.full((1, ATTN_UNIT), sink_ref[n], F32) for n in heads], axis=1)
            mx = jnp.maximum(jnp.max(st, axis=0, keepdims=True), sk)
            p = jnp.exp(st - mx)
            den = jnp.sum(p, axis=0, keepdims=True) + jnp.exp(sk - mx)
            ot = jnp.dot(vt_all[h * HEAD_DIM:(h + 1) * HEAD_DIM, keys], (p / den).astype(BF16),
                         preferred_element_type=F32)
            pieces += [ot[:, gq * ATTN_UNIT:(gq + 1) * ATTN_UNIT] for gq in range(Q_PER_KV)]
        o_ref[0, rs, :] = jnp.transpose(jnp.concatenate(pieces, axis=0)).astype(BF16)


def _attn_prompt(z, sinks, *, tq):
    b, t, _ = z.shape
    per = tq // WINDOW
    return pl.pallas_call(
        functools.partial(_attn_prompt_kernel, tq=tq),
        out_shape=jax.ShapeDtypeStruct((b, t, N_Q_HEADS * HEAD_DIM), BF16),
        grid=(b, t // tq),
        in_specs=[
            pl.BlockSpec(memory_space=pltpu.SMEM),
            pl.BlockSpec((1, tq, 1024), lambda bb, i: (bb, i, 0)),
            pl.BlockSpec((1, tq, COL_BLK), lambda bb, i: (bb, i, KV_BLK)),
            pl.BlockSpec((1, WINDOW, COL_BLK), lambda bb, i: (bb, jnp.maximum(i * per - 1, 0), KV_BLK)),
        ],
        out_specs=pl.BlockSpec((1, tq, 1024), lambda bb, i: (bb, i, 0)),
        scratch_shapes=[pltpu.VMEM((2, ATTN_UNIT + WINDOW, Q_PER_KV * ATTN_UNIT), F32)],
        compiler_params=_params("arbitrary", "arbitrary"),
        name="attn_prompt",
    )(sinks, z, z, z)


def _attn_sample_kernel(sink_ref, q_ref, kv_ref, ck_ref, cv_ref, o_ref):
    kv = kv_ref[0]
    k = jnp.concatenate([ck_ref[0], kv[:, :256]], axis=0)
    v = jnp.concatenate([cv_ref[0], kv[:, 256:]], axis=0)
    q = q_ref[0]
    for h in range(N_KV_HEADS):
        o_ref[0, :, h * 256:(h + 1) * 256] = _attn_head(q, k, v, sink_ref, h, None).astype(BF16)


def _attn_sample(z, cache_k, cache_v, sinks):
    b, t, _ = z.shape
    nc = cache_k.shape[1]
    return pl.pallas_call(
        _attn_sample_kernel,
        out_shape=jax.ShapeDtypeStruct((b, t, N_Q_HEADS * HEAD_DIM), BF16),
        grid=(b,),
        in_specs=[
            pl.BlockSpec(memory_space=pltpu.SMEM),
            pl.BlockSpec((1, t, 1024), lambda bb: (bb, 0, 0)),
            pl.BlockSpec((1, t, COL_BLK), lambda bb: (bb, 0, KV_BLK)),
            pl.BlockSpec((1, nc, 256), lambda bb: (bb, 0, 0)),
            pl.BlockSpec((1, nc, 256), lambda bb: (bb, 0, 0)),
        ],
        out_specs=pl.BlockSpec((1, t, 1024), lambda bb: (bb, 0, 0)),
        compiler_params=_params("arbitrary"),
        name="attn_sample",
    )(sinks, z, z, cache_k.reshape(b, nc, 256), cache_v.reshape(b, nc, 256))


def _retention_kernel(rq_ref, rk_ref, rv_ref, rg_ref, cs_ref, sn_ref, intra_ref, qd_ref, kd_ref, cd_ref,
                      ng_ref, s0_ref, o_ref, st_ref):
    c = pl.program_id(2)

    @pl.when(c == 0)
    def _():
        st_ref[...] = s0_ref[...]

    cs = cs_ref[...]
    sn = sn_ref[...]
    for hh in range(RET_HEADS_PER_BLK):
        lanes = slice(hh * RET_DK, (hh + 1) * RET_DK)
        q = rq_ref[0, :, lanes]
        k = rk_ref[0, :, lanes]
        qr = q * cs + pltpu.roll(q, RET_DK // 2, 1) * sn
        kr = (k * cs + pltpu.roll(k, RET_DK // 2, 1) * sn) * (RET_DK ** -0.5)
        qb = qr.astype(BF16)
        kb = kr.astype(BF16)
        vb = rv_ref[0, :, lanes].astype(BF16)
        sc = lax.dot_general(qb, kb, (((1,), (1,)), ((), ())), preferred_element_type=F32) * intra_ref[hh]
        inner = jnp.dot(sc.astype(BF16), vb, preferred_element_type=F32)
        state = st_ref[0, hh]
        cross = jnp.dot(qb, state.astype(BF16), preferred_element_type=F32) * qd_ref[hh]
        kdt = jnp.transpose(kr * kd_ref[hh]).astype(BF16)
        st_ref[0, hh] = cd_ref[hh, 0:1, :] * state + jnp.dot(kdt, vb, preferred_element_type=F32)
        r = inner + cross
        mu = jnp.mean(r, axis=-1, keepdims=True)
        yc = r - mu
        yn = yc * lax.rsqrt(jnp.mean(yc * yc, axis=-1, keepdims=True) + EPS)
        o_ref[0, :, lanes] = ((yn * ng_ref[:, lanes]) * jax.nn.silu(rg_ref[0, :, lanes])).astype(BF16)


def _ret_log_gamma():
    return jnp.log1p(-jnp.exp2(-5.0 - jnp.arange(N_RET_HEADS, dtype=F32)))


def _retention(z, pos, state0, norm_g, *, chunk):
    b, t, _ = z.shape
    log_g = _ret_log_gamma()
    idx = jnp.arange(chunk, dtype=F32)
    diff = idx[:, None] - idx[None, :]
    intra = jnp.where(diff[None] >= 0.0,
                      jnp.exp(log_g[:, None, None] * jnp.maximum(diff, 0.0)[None]), 0.0)
    ones = jnp.ones((1, 1, RET_DV), F32)
    q_decay = jnp.exp(log_g[:, None] * (idx[None, :] + 1.0))[:, :, None] * ones
    k_decay = jnp.exp(log_g[:, None] * (chunk - 1.0 - idx)[None, :])[:, :, None] * ones
    c_decay = jnp.exp(log_g * chunk)[:, None, None] * jnp.ones((1, SUBLANES, RET_DV), F32)
    half = RET_DK // 2
    freq = 1.0 / (ROPE_BASE ** (jnp.arange(half, dtype=F32) / half))
    ang = pos.astype(F32)[:, None] * freq[None, :]
    cos, sin = jnp.cos(ang), jnp.sin(ang)
    cs = jnp.concatenate([cos, cos], axis=-1)
    sn = jnp.concatenate([-sin, sin], axis=-1)
    nhb = N_RET_HEADS // RET_HEADS_PER_BLK
    hb = RET_HEADS_PER_BLK

    def zspec(blk):
        return pl.BlockSpec((1, chunk, COL_BLK), lambda bb, g, c: (bb, c, blk + g))

    def tab(rows):
        return pl.BlockSpec((hb, rows, RET_DV), lambda bb, g, c: (g, 0, 0))

    out, st = pl.pallas_call(
        _retention_kernel,
        out_shape=(jax.ShapeDtypeStruct((b, t, N_RET_HEADS * RET_DV), BF16),
                   jax.ShapeDtypeStruct((b, N_RET_HEADS, RET_DK, RET_DV), F32)),
        grid=(b, nhb, t // chunk),
        in_specs=[
            zspec(RQ_BLK), zspec(RK_BLK), zspec(RV_BLK), zspec(RG_BLK),
            pl.BlockSpec((chunk, RET_DK), lambda bb, g, c: (c, 0)),
            pl.BlockSpec((chunk, RET_DK), lambda bb, g, c: (c, 0)),
            pl.BlockSpec((hb, chunk, chunk), lambda bb, g, c: (g, 0, 0)),
            tab(chunk), tab(chunk), tab(SUBLANES),
            pl.BlockSpec((1, COL_BLK), lambda bb, g, c: (0, g)),
            pl.BlockSpec((1, hb, RET_DK, RET_DV), lambda bb, g, c: (bb, g, 0, 0)),
        ],
        out_specs=(
            pl.BlockSpec((1, chunk, COL_BLK), lambda bb, g, c: (bb, c, g)),
            pl.BlockSpec((1, hb, RET_DK, RET_DV), lambda bb, g, c: (bb, g, 0, 0)),
        ),
        compiler_params=_params("arbitrary", "arbitrary", "arbitrary"),
        name="retention",
    )(z, z, z, z, cs, sn, intra, q_decay, k_decay, c_decay, norm_g.reshape(1, -1), state0)
    return out, st


def _gate_kernel(u_ref, v_ref, lg_ref, lb_ref, ws_ref, bs_ref, y_ref, *maybe_vn_ref, rows, span):
    ri = lax.broadcasted_iota(jnp.int32, (span, span), 0)
    ci = lax.broadcasted_iota(jnp.int32, (span, span), 1)
    wt = [jnp.where(ri >= ci, ws_ref[g], 0.0).astype(BF16) for g in range(GM_GROUPS)]
    bs = bs_ref[...]
    lg = lg_ref[...]
    lb = lb_ref[...]
    for c in range(rows // span):
        rs = slice(c * span, (c + 1) * span)
        v = v_ref[0, rs, :]
        mu = jnp.mean(v, axis=-1, keepdims=True)
        xc = v - mu
        vn = (xc * lax.rsqrt(jnp.mean(xc * xc, axis=-1, keepdims=True) + EPS)) * lg + lb
        for vn_ref in maybe_vn_ref:
            vn_ref[0, rs, :] = vn
        vb = vn.astype(BF16)
        for g in range(GM_GROUPS):
            cols = slice(g * GM_GROUP_DIM, (g + 1) * GM_GROUP_DIM)
            mixed = jnp.dot(wt[g], vb[:, cols], preferred_element_type=F32) + bs[:, g:g + 1]
            y_ref[0, rs, cols] = (u_ref[0, rs, cols] * mixed).astype(BF16)


def _spatial_gate(uv, ln_g, ln_b, ws, bs, *, rows, emit_vn):
    b, t, _ = uv.shape
    span = min(t, GM_CHUNK)
    ws_l = ws[:, :span, :span]
    bs_t = jnp.transpose(bs[:, :span])
    tile = pl.BlockSpec((1, rows, D_MODEL), lambda bb, i: (bb, i, 0))
    outs = pl.pallas_call(
        functools.partial(_gate_kernel, rows=rows, span=span),
        out_shape=(jax.ShapeDtypeStruct((b, t, D_MODEL), BF16),)
        + ((jax.ShapeDtypeStruct((b, t, D_MODEL), F32),) if emit_vn else ()),
        grid=(b, t // rows),
        in_specs=[
            pl.BlockSpec((1, rows, D_MODEL), lambda bb, i: (bb, i, 0)),
            pl.BlockSpec((1, rows, D_MODEL), lambda bb, i: (bb, i, 1)),
            pl.BlockSpec((1, D_MODEL), lambda bb, i: (0, 0)),
            pl.BlockSpec((1, D_MODEL), lambda bb, i: (0, 0)),
            pl.BlockSpec((GM_GROUPS, span, span), lambda bb, i: (0, 0, 0)),
            pl.BlockSpec((span, GM_GROUPS), lambda bb, i: (0, 0)),
        ],
        out_specs=(tile, tile) if emit_vn else (tile,),
        compiler_params=_params("arbitrary", "arbitrary"),
        name="spatial_gate",
    )(uv, uv, ln_g.reshape(1, -1), ln_b.reshape(1, -1), ws_l, bs_t)
    return outs if emit_vn else (outs[0], None)


ROW_TILE = 1024
FFN_ROW_TILE = 2048
K_BLK = 512
OUT_ROW_TILE = 512
OUT_K_BLK = 1024
IN_EVEN_BLK = 1408
IN_ODD_BLK = 1024
FFN_BLK = 512
ATTN_ROW_TILE = 512


def _conv_ffn(hp, xp, hs, xs, g3, g_next, w_up, layer, conv_w, conv_b, w_down, state_p, state_s):
    nseq_p, nseq_s = state_p.shape[0], state_s.shape[0]
    act_p, c_p, wdb = _ffn_up(xp, w_up, layer, conv_w, conv_b, state_p, w_down, groups=1,
                              rows_per_group=FFN_ROW_TILE, bn=FFN_BLK, name="ffn_up_p")
    act_s, c_s, _ = _ffn_up(xs, w_up, layer, conv_w, conv_b, state_s, None, groups=nseq_s,
                            rows_per_group=xs.shape[0] // nseq_s, bn=FFN_BLK, name="ffn_up_s")
    hp, xp = _matmul_norm_res([act_p], wdb, 0, g3, hp, g_next, bm=ROW_TILE, bk=K_BLK, resident=False,
                              name="ffn_down_p")
    hs, xs = _matmul_norm_res([act_s], wdb, 0, g3, hs, g_next, bm=hs.shape[0], bk=K_BLK, resident=False,
                              name="ffn_down_s")
    return hp, xp, c_p, hs, xs, c_s


def kernel(x_prompt, x_sample, cache_swa_k, cache_swa_v, state_ret, state_ffn_conv, norm_g, w_in_even,
           w_out_even, attn_sinks, ret_norm_g, w_in_odd, w_out_odd, gm_ln_g, gm_ln_b, gm_ws, gm_bs,
           ffn_w_up, ffn_conv_w, ffn_conv_b, ffn_w_down):
    nb, seq, d = x_prompt.shape
    db, dseq, _ = x_sample.shape
    depth = norm_g.shape[0]
    hp = x_prompt.reshape(nb * seq, d)
    hs = x_sample.reshape(db * dseq, d)
    bm_p, bm_s = ROW_TILE, db * dseq
    pos_p = jnp.arange(seq)
    pos_s = PAST_LEN + jnp.arange(dseq)
    kp_l, vp_l, rp_l, cp_l = [], [], [], []
    ks_l, vs_l, rs_l, cs_l, gv_l = [], [], [], [], []
    xp = _rmsnorm(hp, norm_g[0, 0], bm=ROW_TILE, name="norm_in_p")
    xs = _rmsnorm(hs, norm_g[0, 0], bm=bm_s, name="norm_in_s")
    for layer in range(depth):
        g = norm_g[layer]
        g_next = norm_g[layer + 1, 0] if layer + 1 < depth else None
        if layer % 2 == 0:
            e = layer // 2
            zp = _xw(xp, w_in_even, e, bm=bm_p, bn=IN_EVEN_BLK, gelu=False, name="in_even_p")
            zs = _xw(xs, w_in_even, e, bm=bm_s, bn=IN_EVEN_BLK, gelu=False, name="in_even_s")
            zp = zp.reshape(nb, seq, EVEN_IN)
            zs = zs.reshape(db, dseq, EVEN_IN)
            attn_p = _attn_prompt(zp, attn_sinks[e], tq=ATTN_ROW_TILE)
            attn_s = _attn_sample(zs, cache_swa_k[e], cache_swa_v[e], attn_sinks[e])
            ret_p, r_p = _retention(zp, pos_p, jnp.zeros((nb, N_RET_HEADS, RET_DK, RET_DV), F32),
                                    ret_norm_g[e], chunk=RET_CHUNK)
            ret_s, r_s = _retention(zs, pos_s, state_ret[e].astype(F32), ret_norm_g[e], chunk=dseq)
            mixed_p = [attn_p.reshape(nb * seq, -1), ret_p.reshape(nb * seq, -1)]
            mixed_s = [attn_s.reshape(db * dseq, -1), ret_s.reshape(db * dseq, -1)]
            hp, xp = _matmul_norm_res(mixed_p, w_out_even, e, g[1], hp, g[2], bm=OUT_ROW_TILE, bk=OUT_K_BLK,
                                      resident=True, name="out_even_p")
            hs, xs = _matmul_norm_res(mixed_s, w_out_even, e, g[1], hs, g[2], bm=bm_s, bk=OUT_K_BLK,
                                      resident=False, name="out_even_s")
            k_new = zs[:, :, 1024:1280].reshape(db, dseq, N_KV_HEADS, HEAD_DIM)
            v_new = zs[:, :, 1280:1536].reshape(db, dseq, N_KV_HEADS, HEAD_DIM)
            n_keep = cache_swa_k.shape[2]
            kp_l.append(zp[:, seq - WINDOW:, 1024:1280].reshape(nb, WINDOW, N_KV_HEADS, HEAD_DIM))
            vp_l.append(zp[:, seq - WINDOW:, 1280:1536].reshape(nb, WINDOW, N_KV_HEADS, HEAD_DIM))
            ks_l.append(jnp.concatenate([cache_swa_k[e], k_new], axis=1)[:, -n_keep:])
            vs_l.append(jnp.concatenate([cache_swa_v[e], v_new], axis=1)[:, -n_keep:])
            rp_l.append(r_p)
            rs_l.append(r_s.astype(state_ret.dtype))
        else:
            o = layer // 2
            uvp = _xw(xp, w_in_odd, o, bm=bm_p, bn=IN_ODD_BLK, gelu=True, name="in_odd_p")
            uvs = _xw(xs, w_in_odd, o, bm=bm_s, bn=IN_ODD_BLK, gelu=True, name="in_odd_s")
            yp, _ = _spatial_gate(uvp.reshape(nb, seq, -1), gm_ln_g[o], gm_ln_b[o], gm_ws[o], gm_bs[o], rows=512,
                                  emit_vn=False)
            ys, gv = _spatial_gate(uvs.reshape(db, dseq, -1), gm_ln_g[o], gm_ln_b[o], gm_ws[o], gm_bs[o],
                                   rows=dseq, emit_vn=True)
            hp, xp = _matmul_norm_res([yp.reshape(nb * seq, -1)], w_out_odd, o, g[1], hp, g[2], bm=OUT_ROW_TILE,
                                      bk=OUT_K_BLK, resident=True, name="out_odd_p")
            hs, xs = _matmul_norm_res([ys.reshape(db * dseq, -1)], w_out_odd, o, g[1], hs, g[2], bm=bm_s,
                                      bk=OUT_K_BLK, resident=False, name="out_odd_s")
            gv_l.append(gv)
        zero_state = jnp.zeros((nb, CONV_W - 1, 2 * D_FF), F32)
        hp, xp, c_p, hs, xs, c_s = _conv_ffn(hp, xp, hs, xs, g[3], g_next, ffn_w_up, layer, ffn_conv_w[layer],
                                             ffn_conv_b[layer], ffn_w_down, zero_state, state_ffn_conv[layer])
        cp_l.append(c_p)
        cs_l.append(c_s)
    return (hp.reshape(nb, seq, d), hs.reshape(db, dseq, d),
            jnp.stack(kp_l), jnp.stack(vp_l), jnp.stack(rp_l), jnp.stack(cp_l),
            jnp.stack(ks_l), jnp.stack(vs_l), jnp.stack(rs_l), jnp.stack(cs_l), jnp.stack(gv_l))
```

```python
import functools

import jax
import jax.numpy as jnp
from jax import lax
from jax.experimental import pallas as pl
from jax.experimental.pallas import tpu as pltpu

F32 = jnp.float32
BF16 = jnp.bfloat16

D_MODEL = 2048
CHUNK = 64
HEAD_DIM = 64
N_Q_HEADS = 16
N_KV_HEADS = 4
Q_PER_KV = N_Q_HEADS // N_KV_HEADS
WINDOW = 128
N_RET_HEADS = 8
RET_DK = 128
RET_DV = 128
ROPE_BASE = 10000.0
GM_CHUNK = 128
GM_GROUPS = 8
GM_GROUP_DIM = D_MODEL // GM_GROUPS
D_FF = 5632
CONV_W = 3
EPS = 1e-6
PAST_LEN = 1024
EVEN_IN = 5632

SUBLANES = 8
VMEM_LIMIT_BYTES = 56 * 1024 * 1024

COL_BLK = 512
KV_BLK = 2
RQ_BLK, RK_BLK, RV_BLK, RG_BLK = 3, 5, 7, 9
RET_HEADS_PER_BLK = COL_BLK // RET_DK
RET_CHUNK = 256
SUB_COLS = 256
EPI_ROWS = 16
NORM_ROWS = 64


def _params(*sem):
    return pltpu.CompilerParams(dimension_semantics=sem, vmem_limit_bytes=VMEM_LIMIT_BYTES)


def _rmsnorm_rows(x_ref, g_ref, out_ref, rows, chunk):
    g = g_ref[...]

    def body(c, carry):
        r0 = pl.multiple_of(c * chunk, chunk)
        x = x_ref[pl.ds(r0, chunk), :]
        ms = jnp.mean(x * x, axis=-1, keepdims=True)
        out_ref[pl.ds(r0, chunk), :] = ((x * lax.rsqrt(ms + EPS)) * g).astype(out_ref.dtype)
        return carry

    lax.fori_loop(0, rows // chunk, body, 0)


def _zero_after(x):
    z = pltpu.bitcast(x, jnp.uint32)
    z = lax.shift_right_logical(lax.shift_right_logical(z, jnp.uint32(16)), jnp.uint32(16))
    return pltpu.bitcast(z, F32)


def _rmsnorm_kernel(x_ref, g_ref, o_ref, *, bm):
    _rmsnorm_rows(x_ref, g_ref, o_ref, bm, min(bm, 128))


def _rmsnorm(x, g, *, bm, name):
    m, k = x.shape
    return pl.pallas_call(
        functools.partial(_rmsnorm_kernel, bm=bm),
        out_shape=jax.ShapeDtypeStruct((m, k), BF16),
        grid=(m // bm,),
        in_specs=[pl.BlockSpec((bm, k), lambda i: (i, 0)), pl.BlockSpec((1, k), lambda i: (0, 0))],
        out_specs=pl.BlockSpec((bm, k), lambda i: (i, 0)),
        compiler_params=_params("arbitrary"),
        name=name,
    )(x, g.reshape(1, k))


def _xw_kernel(xn_ref, w_ref, o_ref, *wb_refs, bm, rc, gelu):
    if wb_refs:
        wb_ref = wb_refs[0]

        @pl.when(pl.program_id(1) == 0)
        def _():
            wb_ref[...] = w_ref[...].astype(BF16)
    else:
        wb_ref = w_ref

    for c in range(bm // rc):
        y = jnp.dot(xn_ref[c * rc:(c + 1) * rc, :], wb_ref[...], preferred_element_type=F32)
        if gelu:
            y = jax.nn.gelu(y, approximate=True)
        o_ref[c * rc:(c + 1) * rc, :] = y.astype(o_ref.dtype)


def _xw(xn, w, layer, *, bm, bn, gelu, emit_w, name):
    m, k = xn.shape
    tile = pl.BlockSpec((bm, k), lambda j, i: (i, 0))
    blk = pl.BlockSpec((None, k, bn), lambda j, i: (j, 0, 0))
    if w.dtype == BF16:
        nj = w.shape[0]
        w_spec, scratch, extra_shape, extra_spec = blk, [], (), ()
    else:
        nj = w.shape[2] // bn
        w_spec = pl.BlockSpec((None, k, bn), lambda j, i: (layer, 0, j))
        scratch = [] if emit_w else [pltpu.VMEM((k, bn), BF16)]
        extra_shape = (jax.ShapeDtypeStruct((nj, k, bn), BF16),) if emit_w else ()
        extra_spec = (blk,) if emit_w else ()
    outs = pl.pallas_call(
        functools.partial(_xw_kernel, bm=bm, rc=min(bm, 256), gelu=gelu),
        out_shape=(jax.ShapeDtypeStruct((m, nj * bn), F32),) + extra_shape,
        grid=(nj, m // bm),
        in_specs=[tile, w_spec],
        out_specs=(pl.BlockSpec((bm, bn), lambda j, i: (i, j)),) + extra_spec,
        scratch_shapes=scratch,
        compiler_params=_params("arbitrary", "arbitrary"),
        name=name,
    )(xn, w)
    return outs if emit_w and w.dtype != BF16 else (outs[0], None)


def _matmul_norm_res_kernel(*refs, splits, nk, bm, rc, emit_next, weights):
    na = len(splits)
    a_refs = refs[:na]
    rest = list(refs[na:])
    wb_ref = rest.pop() if weights != "bf16" else None
    if emit_next:
        w_ref, g_ref, r_ref, gn_ref, o_ref, xn_ref = rest
    else:
        w_ref, g_ref, r_ref, o_ref = rest
    i = pl.program_id(0)
    k = pl.program_id(1)
    if weights == "stream":
        wb_ref[...] = w_ref[...].astype(BF16)
        w_blk = wb_ref
    elif weights == "resident":
        @pl.when(i == 0)
        def _():
            wb_ref[k] = w_ref[...].astype(BF16)

        w_blk = wb_ref.at[k]
    else:
        w_blk = w_ref

    def finish(r0, y):
        rows = slice(r0, r0 + NORM_ROWS)
        ms = jnp.mean(y * y, axis=-1, keepdims=True)
        h = r_ref[rows, :] + (y * lax.rsqrt(ms + EPS)) * g_ref[...]
        o_ref[rows, :] = h
        if emit_next:
            ms2 = jnp.mean(h * h, axis=-1, keepdims=True)
            xn_ref[rows, :] = ((h * lax.rsqrt(ms2 + EPS)) * gn_ref[...]).astype(BF16)

    def step(a_ref, first, last):
        for c in range(bm // rc):
            rows = slice(c * rc, (c + 1) * rc)
            y = jnp.dot(a_ref[rows, :], w_blk[...], preferred_element_type=F32)
            if not first:
                y = o_ref[rows, :] + y
            if last:
                for r in range(0, rc, NORM_ROWS):
                    finish(c * rc + r, y[r:r + NORM_ROWS])
            else:
                o_ref[rows, :] = y

    for a_ref, (k0, k1) in zip(a_refs, splits):
        for first, last in sorted({(kk == 0, kk == nk - 1) for kk in range(k0, k1)}):
            ks = [kk for kk in range(k0, k1) if (kk == 0, kk == nk - 1) == (first, last)]
            pl.when((k >= ks[0]) & (k <= ks[-1]))(functools.partial(step, a_ref, first, last))


def _matmul_norm_res(a_list, w, layer, g, resid, g_next, *, bm, bk, resident, name):
    m = a_list[0].shape[0]
    n = w.shape[2]
    splits, k0 = [], 0
    for a in a_list:
        splits.append((k0, k0 + a.shape[1] // bk))
        k0 = splits[-1][1]
    nk = k0
    rc = min(bm, 256)
    emit_next = g_next is not None
    weights = "bf16" if w.dtype == BF16 else ("resident" if resident else "stream")

    def a_spec(k0, k1):
        return pl.BlockSpec((bm, bk), lambda i, k: (i, jnp.clip(k - k0, 0, k1 - k0 - 1)))

    if weights == "resident":
        w_spec = pl.BlockSpec((None, bk, n), lambda i, k: (layer, jnp.where(i == 0, k, nk - 1), 0))
        scratch = [pltpu.VMEM((nk, bk, n), BF16)]
    else:
        w_spec = pl.BlockSpec((None, bk, n), lambda i, k: (layer, k, 0))
        scratch = [pltpu.VMEM((bk, n), BF16)] if weights == "stream" else []
    row_vec = pl.BlockSpec((1, n), lambda i, k: (0, 0))
    tile = pl.BlockSpec((bm, n), lambda i, k: (i, 0))
    out = pl.pallas_call(
        functools.partial(_matmul_norm_res_kernel, splits=tuple(splits), nk=nk, bm=bm, rc=rc,
                          emit_next=emit_next, weights=weights),
        out_shape=((jax.ShapeDtypeStruct((m, n), F32), jax.ShapeDtypeStruct((m, n), BF16)) if emit_next
                   else jax.ShapeDtypeStruct((m, n), F32)),
        grid=(m // bm, nk),
        in_specs=[a_spec(*sp) for sp in splits] + [w_spec, row_vec, tile] + ([row_vec] if emit_next else []),
        out_specs=(tile, tile) if emit_next else tile,
        scratch_shapes=scratch,
        compiler_params=_params("arbitrary", "arbitrary"),
        name=name,
    )(*a_list, w, g.reshape(1, n), resid, *([g_next.reshape(1, n)] if emit_next else []))
    return out if emit_next else (out, None)


def _ffn_up_kernel(*refs, groups, rows_per_group, rc, tiles_per_seq, cast_down):
    if cast_down:
        (xn_ref, wa_ref, wg_ref, cwa_ref, cwg_ref, sa_ref, sg_ref, wd_ref,
         act_ref, la_ref, lg_ref, wdb_ref, wb_ref, carry_ref) = refs
    else:
        xn_ref, wb_ref, cwa_ref, cwg_ref, sa_ref, sg_ref, act_ref, la_ref, lg_ref, carry_ref = refs
    i = pl.program_id(1)
    bm = groups * rows_per_group
    bn = act_ref.shape[1]
    nsub = bn // SUB_COLS
    subs = [slice(u * SUB_COLS, (u + 1) * SUB_COLS) for u in range(nsub)]

    if cast_down:
        @pl.when(i == 0)
        def _():
            for u in range(nsub):
                wb_ref[u, :, :SUB_COLS] = wa_ref[:, subs[u]].astype(BF16)
                wb_ref[u, :, SUB_COLS:] = wg_ref[:, subs[u]].astype(BF16)
            wdb_ref[...] = wd_ref[...].astype(BF16)

    @pl.when(lax.rem(i, tiles_per_seq) == 0)
    def _():
        carry_ref[0] = sa_ref[...]
        carry_ref[1] = sg_ref[...]

    cws = [jnp.concatenate([cwa_ref[:, subs[u]], cwg_ref[:, subs[u]]], axis=1) for u in range(nsub)]
    piece = min(EPI_ROWS, rows_per_group)
    prev = [None] * nsub
    after = [None] * nsub
    for c in range(bm // rc):
        xc = xn_ref[c * rc:(c + 1) * rc, :]
        for u in range(nsub):
            h = jnp.dot(xc, wb_ref[u], preferred_element_type=F32)
            for q in range(rc // piece):
                row = c * rc + q * piece
                grp = row // rows_per_group
                hcur = h[q * piece:(q + 1) * piece]
                if row % rows_per_group == 0:
                    prev8 = jnp.concatenate([carry_ref[0, grp, :, subs[u]], carry_ref[1, grp, :, subs[u]]], axis=1)
                else:
                    prev8 = prev[u]
                hext = jnp.concatenate([prev8, hcur], axis=0)
                s1 = pltpu.roll(hext, 1, 0)[SUBLANES:]
                s2 = pltpu.roll(hext, 2, 0)[SUBLANES:]
                cw = cws[u] if after[u] is None else cws[u] + jnp.concatenate([after[u]] * 2, axis=1)
                conv = cw[3:4] + s2 * cw[0:1] + s1 * cw[1:2] + hcur * cw[2:3]
                act = jax.nn.gelu(conv[:, SUB_COLS:], approximate=True) * conv[:, :SUB_COLS]
                act_ref[row:row + piece, subs[u]] = act.astype(BF16)
                after[u] = _zero_after(act[piece - SUBLANES:])
                prev[u] = hcur[piece - SUBLANES:]
                if (row + piece) % rows_per_group == 0:
                    la_ref[grp, :, subs[u]] = prev[u][:, :SUB_COLS]
                    lg_ref[grp, :, subs[u]] = prev[u][:, SUB_COLS:]
                    carry_ref[0, grp, :, subs[u]] = prev[u][:, :SUB_COLS]
                    carry_ref[1, grp, :, subs[u]] = prev[u][:, SUB_COLS:]


def _ffn_up(xn, w_up, layer, conv_w, conv_b, state, w_down, *, groups, rows_per_group, bn, name):
    m, k = xn.shape
    nseq = state.shape[0]
    bm = groups * rows_per_group
    ni = m // bm
    tiles_per_seq = ni * groups // nseq
    nj = D_FF // bn
    nsub = bn // SUB_COLS
    rc = min(bm, 256)
    cw = jnp.concatenate([conv_w, conv_b[None], jnp.zeros((SUBLANES - CONV_W - 1, 2 * D_FF), F32)], axis=0)
    st = jnp.concatenate([jnp.zeros((nseq, SUBLANES - (CONV_W - 1), 2 * D_FF), F32), state], axis=1)
    cast_down = w_down is not None
    wub_spec = pl.BlockSpec((None, nsub, k, 2 * SUB_COLS), lambda j, i: (j, 0, 0, 0))
    out_shape = [jax.ShapeDtypeStruct((m, D_FF), BF16),
                 jax.ShapeDtypeStruct((ni, groups, SUBLANES, D_FF), F32),
                 jax.ShapeDtypeStruct((ni, groups, SUBLANES, D_FF), F32)]
    out_specs = [
        pl.BlockSpec((bm, bn), lambda j, i: (i, j)),
        pl.BlockSpec((None, groups, SUBLANES, bn), lambda j, i: (i, 0, 0, j)),
        pl.BlockSpec((None, groups, SUBLANES, bn), lambda j, i: (i, 0, 0, j)),
    ]
    tail_specs = [
        pl.BlockSpec((SUBLANES, bn), lambda j, i: (0, j)),
        pl.BlockSpec((SUBLANES, bn), lambda j, i: (0, nj + j)),
        pl.BlockSpec((groups, SUBLANES, bn), lambda j, i: (i // tiles_per_seq, 0, j)),
        pl.BlockSpec((groups, SUBLANES, bn), lambda j, i: (i // tiles_per_seq, 0, nj + j)),
    ]
    if cast_down:
        d_out = w_down.shape[2]
        in_specs = [pl.BlockSpec((bm, k), lambda j, i: (i, 0)),
                    pl.BlockSpec((None, k, bn), lambda j, i: (layer, 0, j)),
                    pl.BlockSpec((None, k, bn), lambda j, i: (layer, 0, nj + j))] + tail_specs + [
                        pl.BlockSpec((None, bn, d_out), lambda j, i: (layer, j, 0))]
        operands = [xn, w_up, w_up, cw, cw, st, st, w_down]
        out_specs += [pl.BlockSpec((None, bn, d_out), lambda j, i: (0, j, 0)), wub_spec]
        out_shape += [jax.ShapeDtypeStruct((1, D_FF, d_out), BF16),
                      jax.ShapeDtypeStruct((nj, nsub, k, 2 * SUB_COLS), BF16)]
    else:
        in_specs = [pl.BlockSpec((bm, k), lambda j, i: (i, 0)), wub_spec] + tail_specs
        operands = [xn, w_up, cw, cw, st, st]
    outs = pl.pallas_call(
        functools.partial(_ffn_up_kernel, groups=groups, rows_per_group=rows_per_group, rc=rc,
                          tiles_per_seq=tiles_per_seq, cast_down=cast_down),
        out_shape=tuple(out_shape),
        grid=(nj, ni),
        in_specs=in_specs,
        out_specs=tuple(out_specs),
        scratch_shapes=[pltpu.VMEM((2, groups, SUBLANES, bn), F32)],
        compiler_params=_params("arbitrary", "arbitrary"),
        name=name,
    )(*operands)
    act, la, lg = outs[:3]
    keep = SUBLANES - (CONV_W - 1)
    ends = slice(tiles_per_seq - 1, ni, tiles_per_seq)
    new_state = jnp.concatenate([la[ends, :, keep:], lg[ends, :, keep:]], axis=-1)
    new_state = new_state.reshape(nseq, CONV_W - 1, 2 * D_FF)
    return (act, new_state, outs[3], outs[4]) if cast_down else (act, new_state, None, None)


def _attn_head(q, k, v, sink_ref, h, bias):
    nq = q.shape[0]
    kh = k[:, h * HEAD_DIM:(h + 1) * HEAD_DIM].astype(BF16)
    vh = v[:, h * HEAD_DIM:(h + 1) * HEAD_DIM].astype(BF16)
    qs, sk = [], []
    for gq in range(Q_PER_KV):
        c0 = (h * Q_PER_KV + gq) * HEAD_DIM
        qs.append(q[:, c0:c0 + HEAD_DIM])
        sk.append(jnp.full((nq, 1), sink_ref[h * Q_PER_KV + gq], F32))
    qh = (jnp.concatenate(qs, axis=0) * (HEAD_DIM ** -0.5)).astype(BF16)
    sk = jnp.concatenate(sk, axis=0)
    s = lax.dot_general(qh, kh, (((1,), (1,)), ((), ())), preferred_element_type=F32)
    if bias is not None:
        s = s + bias
    mx = jnp.maximum(jnp.max(s, axis=-1, keepdims=True), sk)
    p = jnp.exp(s - mx)
    den = jnp.sum(p, axis=-1, keepdims=True) + jnp.exp(sk - mx)
    o = jnp.dot((p / den).astype(BF16), vh, preferred_element_type=F32)
    return jnp.concatenate([o[gq * nq:(gq + 1) * nq] for gq in range(Q_PER_KV)], axis=1)


ATTN_UNIT = 2 * CHUNK


def _attn_prompt_kernel(sink_ref, q_ref, kv_ref, kvp_ref, o_ref, bias_ref, *, tq):
    i = pl.program_id(1)
    nk = ATTN_UNIT + WINDOW
    cols = Q_PER_KV * ATTN_UNIT
    kvw = N_KV_HEADS * HEAD_DIM
    r_k = lax.broadcasted_iota(jnp.int32, (nk, cols), 0)
    r_q = lax.broadcasted_iota(jnp.int32, (nk, cols), 1)
    lo = (r_q & (ATTN_UNIT - 1)) & ~(CHUNK - 1)
    band = (r_k >= lo) & (r_k < lo + WINDOW + CHUNK)
    bias_ref[1] = jnp.where(band, 0.0, -jnp.inf)
    first_lo = jnp.where(i == 0, WINDOW, 0)
    bias_ref[0] = jnp.where(band & (r_k >= first_lo), 0.0, -jnp.inf)

    kv_all = jnp.concatenate([kvp_ref[0], kv_ref[0]], axis=0)
    k_all = kv_all[:, :kvw].astype(BF16)
    vt_all = jnp.transpose(kv_all[:, kvw:]).astype(BF16)
    for u in range(tq // ATTN_UNIT):
        rs = slice(u * ATTN_UNIT, (u + 1) * ATTN_UNIT)
        keys = slice(u * ATTN_UNIT, u * ATTN_UNIT + nk)
        k_u = k_all[keys]
        qt = jnp.transpose(q_ref[0, rs, :] * (HEAD_DIM ** -0.5)).astype(BF16)
        bias = bias_ref[min(u, 1)]
        pieces = []
        for h in range(N_KV_HEADS):
            heads = [h * Q_PER_KV + gq for gq in range(Q_PER_KV)]
            qt_h = jnp.concatenate([qt[n * HEAD_DIM:(n + 1) * HEAD_DIM] for n in heads], axis=1)
            parts = [qt_h]
            if h > 0:
                parts.insert(0, jnp.zeros((h * HEAD_DIM, cols), BF16))
            if h < N_KV_HEADS - 1:
                parts.append(jnp.zeros(((N_KV_HEADS - 1 - h) * HEAD_DIM, cols), BF16))
            st = jnp.dot(k_u, jnp.concatenate(parts, axis=0), preferred_element_type=F32) + bias
            sk = jnp.concatenate([jnp.full((1, ATTN_UNIT), sink_ref[n], F32) for n in heads], axis=1)
            mx = jnp.maximum(jnp.max(st, axis=0, keepdims=True), sk)
            p = jnp.exp(st - mx)
            den = jnp.sum(p, axis=0, keepdims=True) + jnp.exp(sk - mx)
            ot = jnp.dot(vt_all[h * HEAD_DIM:(h + 1) * HEAD_DIM, keys], (p / den).astype(BF16),
                         preferred_element_type=F32)
            pieces += [ot[:, gq * ATTN_UNIT:(gq + 1) * ATTN_UNIT] for gq in range(Q_PER_KV)]
        o_ref[0, rs, :] = jnp.transpose(jnp.concatenate(pieces, axis=0)).astype(BF16)


def _attn_prompt(z, sinks, *, tq):
    b, t, _ = z.shape
    per = tq // WINDOW
    return pl.pallas_call(
        functools.partial(_attn_prompt_kernel, tq=tq),
        out_shape=jax.ShapeDtypeStruct((b, t, N_Q_HEADS * HEAD_DIM), BF16),
        grid=(b, t // tq),
        in_specs=[
            pl.BlockSpec(memory_space=pltpu.SMEM),
            pl.BlockSpec((1, tq, 1024), lambda bb, i: (bb, i, 0)),
            pl.BlockSpec((1, tq, COL_BLK), lambda bb, i: (bb, i, KV_BLK)),
            pl.BlockSpec((1, WINDOW, COL_BLK), lambda bb, i: (bb, jnp.maximum(i * per - 1, 0), KV_BLK)),
        ],
        out_specs=pl.BlockSpec((1, tq, 1024), lambda bb, i: (bb, i, 0)),
        scratch_shapes=[pltpu.VMEM((2, ATTN_UNIT + WINDOW, Q_PER_KV * ATTN_UNIT), F32)],
        compiler_params=_params("arbitrary", "arbitrary"),
        name="attn_prompt",
    )(sinks, z, z, z)


def _attn_sample_kernel(sink_ref, q_ref, kv_ref, ck_ref, cv_ref, o_ref):
    kv = kv_ref[0]
    k = jnp.concatenate([ck_ref[0], kv[:, :256]], axis=0)
    v = jnp.concatenate([cv_ref[0], kv[:, 256:]], axis=0)
    q = q_ref[0]
    for h in range(N_KV_HEADS):
        o_ref[0, :, h * 256:(h + 1) * 256] = _attn_head(q, k, v, sink_ref, h, None).astype(BF16)


def _attn_sample(z, cache_k, cache_v, sinks):
    b, t, _ = z.shape
    nc = cache_k.shape[1]
    return pl.pallas_call(
        _attn_sample_kernel,
        out_shape=jax.ShapeDtypeStruct((b, t, N_Q_HEADS * HEAD_DIM), BF16),
        grid=(b,),
        in_specs=[
            pl.BlockSpec(memory_space=pltpu.SMEM),
            pl.BlockSpec((1, t, 1024), lambda bb: (bb, 0, 0)),
            pl.BlockSpec((1, t, COL_BLK), lambda bb: (bb, 0, KV_BLK)),
            pl.BlockSpec((1, nc, 256), lambda bb: (bb, 0, 0)),
            pl.BlockSpec((1, nc, 256), lambda bb: (bb, 0, 0)),
        ],
        out_specs=pl.BlockSpec((1, t, 1024), lambda bb: (bb, 0, 0)),
        compiler_params=_params("arbitrary"),
        name="attn_sample",
    )(sinks, z, z, cache_k.reshape(b, nc, 256), cache_v.reshape(b, nc, 256))


def _retention_kernel(rq_ref, rk_ref, rv_ref, rg_ref, cs_ref, sn_ref, intra_ref, qd_ref, kd_ref, cd_ref,
                      ng_ref, s0_ref, o_ref, st_ref):
    c = pl.program_id(2)

    @pl.when(c == 0)
    def _():
        st_ref[...] = s0_ref[...]

    cs = cs_ref[...]
    sn = sn_ref[...]
    for hh in range(RET_HEADS_PER_BLK):
        lanes = slice(hh * RET_DK, (hh + 1) * RET_DK)
        q = rq_ref[0, :, lanes]
        k = rk_ref[0, :, lanes]
        qr = q * cs + pltpu.roll(q, RET_DK // 2, 1) * sn
        kr = (k * cs + pltpu.roll(k, RET_DK // 2, 1) * sn) * (RET_DK ** -0.5)
        qb = qr.astype(BF16)
        kb = kr.astype(BF16)
        vb = rv_ref[0, :, lanes].astype(BF16)
        sc = lax.dot_general(qb, kb, (((1,), (1,)), ((), ())), preferred_element_type=F32) * intra_ref[hh]
        inner = jnp.dot(sc.astype(BF16), vb, preferred_element_type=F32)
        state = st_ref[0, hh]
        cross = jnp.dot(qb, state.astype(BF16), preferred_element_type=F32) * qd_ref[hh]
        kdt = jnp.transpose(kr * kd_ref[hh]).astype(BF16)
        st_ref[0, hh] = cd_ref[hh, 0:1, :] * state + jnp.dot(kdt, vb, preferred_element_type=F32)
        r = inner + cross
        mu = jnp.mean(r, axis=-1, keepdims=True)
        yc = r - mu
        yn = yc * lax.rsqrt(jnp.mean(yc * yc, axis=-1, keepdims=True) + EPS)
        o_ref[0, :, lanes] = ((yn * ng_ref[:, lanes]) * jax.nn.silu(rg_ref[0, :, lanes])).astype(BF16)


def _ret_log_gamma():
    return jnp.log1p(-jnp.exp2(-5.0 - jnp.arange(N_RET_HEADS, dtype=F32)))


def _retention(z, pos, state0, norm_g, *, chunk):
    b, t, _ = z.shape
    log_g = _ret_log_gamma()
    idx = jnp.arange(chunk, dtype=F32)
    diff = idx[:, None] - idx[None, :]
    intra = jnp.where(diff[None] >= 0.0,
                      jnp.exp(log_g[:, None, None] * jnp.maximum(diff, 0.0)[None]), 0.0)
    ones = jnp.ones((1, 1, RET_DV), F32)
    q_decay = jnp.exp(log_g[:, None] * (idx[None, :] + 1.0))[:, :, None] * ones
    k_decay = jnp.exp(log_g[:, None] * (chunk - 1.0 - idx)[None, :])[:, :, None] * ones
    c_decay = jnp.exp(log_g * chunk)[:, None, None] * jnp.ones((1, SUBLANES, RET_DV), F32)
    half = RET_DK // 2
    freq = 1.0 / (ROPE_BASE ** (jnp.arange(half, dtype=F32) / half))
    ang = pos.astype(F32)[:, None] * freq[None, :]
    cos, sin = jnp.cos(ang), jnp.sin(ang)
    cs = jnp.concatenate([cos, cos], axis=-1)
    sn = jnp.concatenate([-sin, sin], axis=-1)
    nhb = N_RET_HEADS // RET_HEADS_PER_BLK
    hb = RET_HEADS_PER_BLK

    def zspec(blk):
        return pl.BlockSpec((1, chunk, COL_BLK), lambda bb, g, c: (bb, c, blk + g))

    def tab(rows):
        return pl.BlockSpec((hb, rows, RET_DV), lambda bb, g, c: (g, 0, 0))

    out, st = pl.pallas_call(
        _retention_kernel,
        out_shape=(jax.ShapeDtypeStruct((b, t, N_RET_HEADS * RET_DV), BF16),
                   jax.ShapeDtypeStruct((b, N_RET_HEADS, RET_DK, RET_DV), F32)),
        grid=(b, nhb, t // chunk),
        in_specs=[
            zspec(RQ_BLK), zspec(RK_BLK), zspec(RV_BLK), zspec(RG_BLK),
            pl.BlockSpec((chunk, RET_DK), lambda bb, g, c: (c, 0)),
            pl.BlockSpec((chunk, RET_DK), lambda bb, g, c: (c, 0)),
            pl.BlockSpec((hb, chunk, chunk), lambda bb, g, c: (g, 0, 0)),
            tab(chunk), tab(chunk), tab(SUBLANES),
            pl.BlockSpec((1, COL_BLK), lambda bb, g, c: (0, g)),
            pl.BlockSpec((1, hb, RET_DK, RET_DV), lambda bb, g, c: (bb, g, 0, 0)),
        ],
        out_specs=(
            pl.BlockSpec((1, chunk, COL_BLK), lambda bb, g, c: (bb, c, g)),
            pl.BlockSpec((1, hb, RET_DK, RET_DV), lambda bb, g, c: (bb, g, 0, 0)),
        ),
        compiler_params=_params("arbitrary", "arbitrary", "arbitrary"),
        name="retention",
    )(z, z, z, z, cs, sn, intra, q_decay, k_decay, c_decay, norm_g.reshape(1, -1), state0)
    return out, st


def _gate_kernel(u_ref, v_ref, lg_ref, lb_ref, ws_ref, bs_ref, y_ref, *maybe_vn_ref, rows, span):
    ri = lax.broadcasted_iota(jnp.int32, (span, span), 0)
    ci = lax.broadcasted_iota(jnp.int32, (span, span), 1)
    wt = [jnp.where(ri >= ci, ws_ref[g], 0.0).astype(BF16) for g in range(GM_GROUPS)]
    bs = bs_ref[...]
    lg = lg_ref[...]
    lb = lb_ref[...]
    for c in range(rows // span):
        rs = slice(c * span, (c + 1) * span)
        v = v_ref[0, rs, :]
        mu = jnp.mean(v, axis=-1, keepdims=True)
        xc = v - mu
        vn = (xc * lax.rsqrt(jnp.mean(xc * xc, axis=-1, keepdims=True) + EPS)) * lg + lb
        for vn_ref in maybe_vn_ref:
            vn_ref[0, rs, :] = vn
        vb = vn.astype(BF16)
        for g in range(GM_GROUPS):
            cols = slice(g * GM_GROUP_DIM, (g + 1) * GM_GROUP_DIM)
            mixed = jnp.dot(wt[g], vb[:, cols], preferred_element_type=F32) + bs[:, g:g + 1]
            y_ref[0, rs, cols] = (u_ref[0, rs, cols] * mixed).astype(BF16)


def _spatial_gate(uv, ln_g, ln_b, ws, bs, *, rows, emit_vn):
    b, t, _ = uv.shape
    span = min(t, GM_CHUNK)
    ws_l = ws[:, :span, :span]
    bs_t = jnp.transpose(bs[:, :span])
    tile = pl.BlockSpec((1, rows, D_MODEL), lambda bb, i: (bb, i, 0))
    outs = pl.pallas_call(
        functools.partial(_gate_kernel, rows=rows, span=span),
        out_shape=(jax.ShapeDtypeStruct((b, t, D_MODEL), BF16),)
        + ((jax.ShapeDtypeStruct((b, t, D_MODEL), F32),) if emit_vn else ()),
        grid=(b, t // rows),
        in_specs=[
            pl.BlockSpec((1, rows, D_MODEL), lambda bb, i: (bb, i, 0)),
            pl.BlockSpec((1, rows, D_MODEL), lambda bb, i: (bb, i, 1)),
            pl.BlockSpec((1, D_MODEL), lambda bb, i: (0, 0)),
            pl.BlockSpec((1, D_MODEL), lambda bb, i: (0, 0)),
            pl.BlockSpec((GM_GROUPS, span, span), lambda bb, i: (0, 0, 0)),
            pl.BlockSpec((span, GM_GROUPS), lambda bb, i: (0, 0)),
        ],
        out_specs=(tile, tile) if emit_vn else (tile,),
        compiler_params=_params("arbitrary", "arbitrary"),
        name="spatial_gate",
    )(uv, uv, ln_g.reshape(1, -1), ln_b.reshape(1, -1), ws_l, bs_t)
    return outs if emit_vn else (outs[0], None)


ROW_TILE = 1024
FFN_ROW_TILE = 1024
K_BLK = 512
OUT_ROW_TILE = 512
OUT_K_BLK = 1024
IN_EVEN_BLK = 1408
IN_ODD_BLK = 1024
FFN_BLK = 512
ATTN_ROW_TILE = 512


def _conv_ffn(hp, xp, hs, xs, g3, g_next, w_up, layer, conv_w, conv_b, w_down, state_p, state_s):
    nseq_p, nseq_s = state_p.shape[0], state_s.shape[0]
    act_p, c_p, wdb, wub = _ffn_up(xp, w_up, layer, conv_w, conv_b, state_p, w_down, groups=1,
                                   rows_per_group=FFN_ROW_TILE, bn=FFN_BLK, name="ffn_up_p")
    act_s, c_s, _, _ = _ffn_up(xs, wub, layer, conv_w, conv_b, state_s, None, groups=nseq_s,
                               rows_per_group=xs.shape[0] // nseq_s, bn=FFN_BLK, name="ffn_up_s")
    hp, xp = _matmul_norm_res([act_p], wdb, 0, g3, hp, g_next, bm=ROW_TILE, bk=K_BLK, resident=False,
                              name="ffn_down_p")
    hs, xs = _matmul_norm_res([act_s], wdb, 0, g3, hs, g_next, bm=hs.shape[0], bk=K_BLK, resident=False,
                              name="ffn_down_s")
    return hp, xp, c_p, hs, xs, c_s


def kernel(x_prompt, x_sample, cache_swa_k, cache_swa_v, state_ret, state_ffn_conv, norm_g, w_in_even,
           w_out_even, attn_sinks, ret_norm_g, w_in_odd, w_out_odd, gm_ln_g, gm_ln_b, gm_ws, gm_bs,
           ffn_w_up, ffn_conv_w, ffn_conv_b, ffn_w_down):
    nb, seq, d = x_prompt.shape
    db, dseq, _ = x_sample.shape
    depth = norm_g.shape[0]
    hp = x_prompt.reshape(nb * seq, d)
    hs = x_sample.reshape(db * dseq, d)
    bm_p, bm_s = ROW_TILE, db * dseq
    pos_p = jnp.arange(seq)
    pos_s = PAST_LEN + jnp.arange(dseq)
    kp_l, vp_l, rp_l, cp_l = [], [], [], []
    ks_l, vs_l, rs_l, cs_l, gv_l = [], [], [], [], []
    xp = _rmsnorm(hp, norm_g[0, 0], bm=ROW_TILE, name="norm_in_p")
    xs = _rmsnorm(hs, norm_g[0, 0], bm=bm_s, name="norm_in_s")
    for layer in range(depth):
        g = norm_g[layer]
        g_next = norm_g[layer + 1, 0] if layer + 1 < depth else None
        if layer % 2 == 0:
            e = layer // 2
            zp, wib = _xw(xp, w_in_even, e, bm=bm_p, bn=IN_EVEN_BLK, gelu=False, emit_w=True, name="in_even_p")
            zs, _ = _xw(xs, wib, e, bm=bm_s, bn=IN_EVEN_BLK, gelu=False, emit_w=False, name="in_even_s")
            zp = zp.reshape(nb, seq, EVEN_IN)
            zs = zs.reshape(db, dseq, EVEN_IN)
            attn_p = _attn_prompt(zp, attn_sinks[e], tq=ATTN_ROW_TILE)
            attn_s = _attn_sample(zs, cache_swa_k[e], cache_swa_v[e], attn_sinks[e])
            ret_p, r_p = _retention(zp, pos_p, jnp.zeros((nb, N_RET_HEADS, RET_DK, RET_DV), F32),
                                    ret_norm_g[e], chunk=RET_CHUNK)
            ret_s, r_s = _retention(zs, pos_s, state_ret[e].astype(F32), ret_norm_g[e], chunk=dseq)
            mixed_p = [attn_p.reshape(nb * seq, -1), ret_p.reshape(nb * seq, -1)]
            mixed_s = [attn_s.reshape(db * dseq, -1), ret_s.reshape(db * dseq, -1)]
            hp, xp = _matmul_norm_res(mixed_p, w_out_even, e, g[1], hp, g[2], bm=OUT_ROW_TILE, bk=OUT_K_BLK,
                                      resident=True, name="out_even_p")
            hs, xs = _matmul_norm_res(mixed_s, w_out_even, e, g[1], hs, g[2], bm=bm_s, bk=OUT_K_BLK,
                                      resident=False, name="out_even_s")
            k_new = zs[:, :, 1024:1280].reshape(db, dseq, N_KV_HEADS, HEAD_DIM)
            v_new = zs[:, :, 1280:1536].reshape(db, dseq, N_KV_HEADS, HEAD_DIM)
            n_keep = cache_swa_k.shape[2]
            kp_l.append(zp[:, seq - WINDOW:, 1024:1280].reshape(nb, WINDOW, N_KV_HEADS, HEAD_DIM))
            vp_l.append(zp[:, seq - WINDOW:, 1280:1536].reshape(nb, WINDOW, N_KV_HEADS, HEAD_DIM))
            ks_l.append(jnp.concatenate([cache_swa_k[e], k_new], axis=1)[:, -n_keep:])
            vs_l.append(jnp.concatenate([cache_swa_v[e], v_new], axis=1)[:, -n_keep:])
            rp_l.append(r_p)
            rs_l.append(r_s.astype(state_ret.dtype))
        else:
            o = layer // 2
            uvp, wib = _xw(xp, w_in_odd, o, bm=bm_p, bn=IN_ODD_BLK, gelu=True, emit_w=True, name="in_odd_p")
            uvs, _ = _xw(xs, wib, o, bm=bm_s, bn=IN_ODD_BLK, gelu=True, emit_w=False, name="in_odd_s")
            yp, _ = _spatial_gate(uvp.reshape(nb, seq, -1), gm_ln_g[o], gm_ln_b[o], gm_ws[o], gm_bs[o], rows=512,
                                  emit_vn=False)
            ys, gv = _spatial_gate(uvs.reshape(db, dseq, -1), gm_ln_g[o], gm_ln_b[o], gm_ws[o], gm_bs[o],
                                   rows=dseq, emit_vn=True)
            hp, xp = _matmul_norm_res([yp.reshape(nb * seq, -1)], w_out_odd, o, g[1], hp, g[2], bm=OUT_ROW_TILE,
                                      bk=OUT_K_BLK, resident=True, name="out_odd_p")
            hs, xs = _matmul_norm_res([ys.reshape(db * dseq, -1)], w_out_odd, o, g[1], hs, g[2], bm=bm_s,
                                      bk=OUT_K_BLK, resident=False, name="out_odd_s")
            gv_l.append(gv)
        zero_state = jnp.zeros((nb, CONV_W - 1, 2 * D_FF), F32)
        hp, xp, c_p, hs, xs, c_s = _conv_ffn(hp, xp, hs, xs, g[3], g_next, ffn_w_up, layer, ffn_conv_w[layer],
                                             ffn_conv_b[layer], ffn_w_down, zero_state, state_ffn_conv[layer])
        cp_l.append(c_p)
        cs_l.append(c_s)
    return (hp.reshape(nb, seq, d), hs.reshape(db, dseq, d),
            jnp.stack(kp_l), jnp.stack(vp_l), jnp.stack(rp_l), jnp.stack(cp_l),
            jnp.stack(ks_l), jnp.stack(vs_l), jnp.stack(rs_l), jnp.stack(cs_l), jnp.stack(gv_l))
```

```python
import functools

import jax
import jax.numpy as jnp
from jax import lax
from jax.experimental import pallas as pl
from jax.experimental.pallas import tpu as pltpu

F32 = jnp.float32
BF16 = jnp.bfloat16

D_MODEL = 2048
CHUNK = 64
HEAD_DIM = 64
N_Q_HEADS = 16
N_KV_HEADS = 4
Q_PER_KV = N_Q_HEADS // N_KV_HEADS
WINDOW = 128
N_RET_HEADS = 8
RET_DK = 128
RET_DV = 128
ROPE_BASE = 10000.0
GM_CHUNK = 128
GM_GROUPS = 8
GM_GROUP_DIM = D_MODEL // GM_GROUPS
D_FF = 5632
CONV_W = 3
EPS = 1e-6
PAST_LEN = 1024
EVEN_IN = 5632

SUBLANES = 8
VMEM_LIMIT_BYTES = 56 * 1024 * 1024

COL_BLK = 512
KV_BLK = 2
RQ_BLK, RK_BLK, RV_BLK, RG_BLK = 3, 5, 7, 9
RET_HEADS_PER_BLK = COL_BLK // RET_DK
RET_CHUNK = 256
SUB_COLS = 256
EPI_ROWS = 16
NORM_ROWS = 64


def _params(*sem):
    return pltpu.CompilerParams(dimension_semantics=sem, vmem_limit_bytes=VMEM_LIMIT_BYTES)


def _rmsnorm_rows(x_ref, g_ref, out_ref, rows, chunk):
    g = g_ref[...]

    def body(c, carry):
        r0 = pl.multiple_of(c * chunk, chunk)
        x = x_ref[pl.ds(r0, chunk), :]
        ms = jnp.mean(x * x, axis=-1, keepdims=True)
        out_ref[pl.ds(r0, chunk), :] = ((x * lax.rsqrt(ms + EPS)) * g).astype(out_ref.dtype)
        return carry

    lax.fori_loop(0, rows // chunk, body, 0)


def _zero_after(x):
    z = pltpu.bitcast(x, jnp.uint32)
    z = lax.shift_right_logical(lax.shift_right_logical(z, jnp.uint32(16)), jnp.uint32(16))
    return pltpu.bitcast(z, F32)


def _rmsnorm_kernel(x_ref, g_ref, o_ref, *, bm):
    _rmsnorm_rows(x_ref, g_ref, o_ref, bm, min(bm, 128))


def _rmsnorm(x, g, *, bm, name):
    m, k = x.shape
    return pl.pallas_call(
        functools.partial(_rmsnorm_kernel, bm=bm),
        out_shape=jax.ShapeDtypeStruct((m, k), BF16),
        grid=(m // bm,),
        in_specs=[pl.BlockSpec((bm, k), lambda i: (i, 0)), pl.BlockSpec((1, k), lambda i: (0, 0))],
        out_specs=pl.BlockSpec((bm, k), lambda i: (i, 0)),
        compiler_params=_params("arbitrary"),
        name=name,
    )(x, g.reshape(1, k))


def _xw_kernel(xn_ref, w_ref, o_ref, wb_ref, *, bm, rc, gelu):
    @pl.when(pl.program_id(1) == 0)
    def _():
        wb_ref[...] = w_ref[...].astype(BF16)

    for c in range(bm // rc):
        y = jnp.dot(xn_ref[c * rc:(c + 1) * rc, :], wb_ref[...], preferred_element_type=F32)
        if gelu:
            y = jax.nn.gelu(y, approximate=True)
        o_ref[c * rc:(c + 1) * rc, :] = y.astype(o_ref.dtype)


def _xw(xn, w, layer, *, bm, bn, gelu, name):
    m, k = xn.shape
    n = w.shape[2]
    rc = min(bm, 256)
    return pl.pallas_call(
        functools.partial(_xw_kernel, bm=bm, rc=rc, gelu=gelu),
        out_shape=jax.ShapeDtypeStruct((m, n), F32),
        grid=(n // bn, m // bm),
        in_specs=[
            pl.BlockSpec((bm, k), lambda j, i: (i, 0)),
            pl.BlockSpec((None, k, bn), lambda j, i: (layer, 0, j)),
        ],
        out_specs=pl.BlockSpec((bm, bn), lambda j, i: (i, j)),
        scratch_shapes=[pltpu.VMEM((k, bn), BF16)],
        compiler_params=_params("arbitrary", "arbitrary"),
        name=name,
    )(xn, w)


def _matmul_norm_res_kernel(*refs, splits, nk, bm, rc, emit_next, weights):
    na = len(splits)
    a_refs = refs[:na]
    rest = list(refs[na:])
    wb_ref = rest.pop() if weights != "bf16" else None
    if emit_next:
        w_ref, g_ref, r_ref, gn_ref, o_ref, xn_ref = rest
    else:
        w_ref, g_ref, r_ref, o_ref = rest
    i = pl.program_id(0)
    k = pl.program_id(1)
    if weights == "stream":
        wb_ref[...] = w_ref[...].astype(BF16)
        w_blk = wb_ref
    elif weights == "resident":
        @pl.when(i == 0)
        def _():
            wb_ref[k] = w_ref[...].astype(BF16)

        w_blk = wb_ref.at[k]
    else:
        w_blk = w_ref

    def finish(r0, y):
        rows = slice(r0, r0 + NORM_ROWS)
        ms = jnp.mean(y * y, axis=-1, keepdims=True)
        h = r_ref[rows, :] + (y * lax.rsqrt(ms + EPS)) * g_ref[...]
        o_ref[rows, :] = h
        if emit_next:
            ms2 = jnp.mean(h * h, axis=-1, keepdims=True)
            xn_ref[rows, :] = ((h * lax.rsqrt(ms2 + EPS)) * gn_ref[...]).astype(BF16)

    def step(a_ref, first, last):
        for c in range(bm // rc):
            rows = slice(c * rc, (c + 1) * rc)
            y = jnp.dot(a_ref[rows, :], w_blk[...], preferred_element_type=F32)
            if not first:
                y = o_ref[rows, :] + y
            if last:
                for r in range(0, rc, NORM_ROWS):
                    finish(c * rc + r, y[r:r + NORM_ROWS])
            else:
                o_ref[rows, :] = y

    for a_ref, (k0, k1) in zip(a_refs, splits):
        for first, last in sorted({(kk == 0, kk == nk - 1) for kk in range(k0, k1)}):
            ks = [kk for kk in range(k0, k1) if (kk == 0, kk == nk - 1) == (first, last)]
            pl.when((k >= ks[0]) & (k <= ks[-1]))(functools.partial(step, a_ref, first, last))


def _matmul_norm_res(a_list, w, layer, g, resid, g_next, *, bm, bk, resident, name):
    m = a_list[0].shape[0]
    n = w.shape[2]
    splits, k0 = [], 0
    for a in a_list:
        splits.append((k0, k0 + a.shape[1] // bk))
        k0 = splits[-1][1]
    nk = k0
    rc = min(bm, 256)
    emit_next = g_next is not None
    weights = "bf16" if w.dtype == BF16 else ("resident" if resident else "stream")

    def a_spec(k0, k1):
        return pl.BlockSpec((bm, bk), lambda i, k: (i, jnp.clip(k - k0, 0, k1 - k0 - 1)))

    if weights == "resident":
        w_spec = pl.BlockSpec((None, bk, n), lambda i, k: (layer, jnp.where(i == 0, k, nk - 1), 0))
        scratch = [pltpu.VMEM((nk, bk, n), BF16)]
    else:
        w_spec = pl.BlockSpec((None, bk, n), lambda i, k: (layer, k, 0))
        scratch = [pltpu.VMEM((bk, n), BF16)] if weights == "stream" else []
    row_vec = pl.BlockSpec((1, n), lambda i, k: (0, 0))
    tile = pl.BlockSpec((bm, n), lambda i, k: (i, 0))
    out = pl.pallas_call(
        functools.partial(_matmul_norm_res_kernel, splits=tuple(splits), nk=nk, bm=bm, rc=rc,
                          emit_next=emit_next, weights=weights),
        out_shape=((jax.ShapeDtypeStruct((m, n), F32), jax.ShapeDtypeStruct((m, n), BF16)) if emit_next
                   else jax.ShapeDtypeStruct((m, n), F32)),
        grid=(m // bm, nk),
        in_specs=[a_spec(*sp) for sp in splits] + [w_spec, row_vec, tile] + ([row_vec] if emit_next else []),
        out_specs=(tile, tile) if emit_next else tile,
        scratch_shapes=scratch,
        compiler_params=_params("arbitrary", "arbitrary"),
        name=name,
    )(*a_list, w, g.reshape(1, n), resid, *([g_next.reshape(1, n)] if emit_next else []))
    return out if emit_next else (out, None)


def _down_kernel(*refs, bm, emit_next):
    if emit_next:
        a_ref, w_hbm_ref, g_ref, r_ref, gn_ref, o_ref, xn_ref, w_ref, sem = refs
    else:
        a_ref, w_hbm_ref, g_ref, r_ref, o_ref, w_ref, sem = refs

    @pl.when(pl.program_id(0) == 0)
    def _():
        copy = pltpu.make_async_copy(w_hbm_ref, w_ref, sem)
        copy.start()
        copy.wait()

    y = jnp.dot(a_ref[...], w_ref[...], preferred_element_type=F32)
    for r in range(0, bm, NORM_ROWS):
        rows = slice(r, r + NORM_ROWS)
        yy = y[rows]
        ms = jnp.mean(yy * yy, axis=-1, keepdims=True)
        h = r_ref[rows, :] + (yy * lax.rsqrt(ms + EPS)) * g_ref[...]
        o_ref[rows, :] = h
        if emit_next:
            ms2 = jnp.mean(h * h, axis=-1, keepdims=True)
            xn_ref[rows, :] = ((h * lax.rsqrt(ms2 + EPS)) * gn_ref[...]).astype(BF16)


def _down_norm_res(a, w, g, resid, g_next, *, bm, name):
    m, kdim = a.shape
    n = w.shape[1]
    emit_next = g_next is not None
    row_vec = pl.BlockSpec((1, n), lambda i: (0, 0))
    tile = pl.BlockSpec((bm, n), lambda i: (i, 0))
    out = pl.pallas_call(
        functools.partial(_down_kernel, bm=bm, emit_next=emit_next),
        out_shape=((jax.ShapeDtypeStruct((m, n), F32), jax.ShapeDtypeStruct((m, n), BF16)) if emit_next
                   else jax.ShapeDtypeStruct((m, n), F32)),
        grid=(m // bm,),
        in_specs=[pl.BlockSpec((bm, kdim), lambda i: (i, 0)), pl.BlockSpec(memory_space=pl.ANY), row_vec, tile]
        + ([row_vec] if emit_next else []),
        out_specs=(tile, tile) if emit_next else tile,
        scratch_shapes=[pltpu.VMEM((kdim, n), BF16), pltpu.SemaphoreType.DMA(())],
        compiler_params=_params("arbitrary"),
        name=name,
    )(a, w, g.reshape(1, n), resid, *([g_next.reshape(1, n)] if emit_next else []))
    return out if emit_next else (out, None)


def _ffn_up_kernel(*refs, groups, rows_per_group, rc, tiles_per_seq, cast_down):
    if cast_down:
        (xn_ref, wa_ref, wg_ref, cwa_ref, cwg_ref, sa_ref, sg_ref, wd_ref,
         act_ref, la_ref, lg_ref, wdb_ref, wb_ref, carry_ref) = refs
    else:
        (xn_ref, wa_ref, wg_ref, cwa_ref, cwg_ref, sa_ref, sg_ref,
         act_ref, la_ref, lg_ref, wb_ref, carry_ref) = refs
    i = pl.program_id(1)
    bm = groups * rows_per_group
    bn = act_ref.shape[1]
    nsub = bn // SUB_COLS
    subs = [slice(u * SUB_COLS, (u + 1) * SUB_COLS) for u in range(nsub)]

    @pl.when(i == 0)
    def _():
        for u in range(nsub):
            wb_ref[u, :, :SUB_COLS] = wa_ref[:, subs[u]].astype(BF16)
            wb_ref[u, :, SUB_COLS:] = wg_ref[:, subs[u]].astype(BF16)
        if cast_down:
            wdb_ref[...] = wd_ref[...].astype(BF16)

    @pl.when(lax.rem(i, tiles_per_seq) == 0)
    def _():
        carry_ref[0] = sa_ref[...]
        carry_ref[1] = sg_ref[...]

    cws = [jnp.concatenate([cwa_ref[:, subs[u]], cwg_ref[:, subs[u]]], axis=1) for u in range(nsub)]
    piece = min(EPI_ROWS, rows_per_group)
    prev = [None] * nsub
    after = [None] * nsub
    for c in range(bm // rc):
        xc = xn_ref[c * rc:(c + 1) * rc, :]
        for u in range(nsub):
            h = jnp.dot(xc, wb_ref[u], preferred_element_type=F32)
            for q in range(rc // piece):
                row = c * rc + q * piece
                grp = row // rows_per_group
                hcur = h[q * piece:(q + 1) * piece]
                if row % rows_per_group == 0:
                    prev8 = jnp.concatenate([carry_ref[0, grp, :, subs[u]], carry_ref[1, grp, :, subs[u]]], axis=1)
                else:
                    prev8 = prev[u]
                hext = jnp.concatenate([prev8, hcur], axis=0)
                s1 = pltpu.roll(hext, 1, 0)[SUBLANES:]
                s2 = pltpu.roll(hext, 2, 0)[SUBLANES:]
                cw = cws[u] if after[u] is None else cws[u] + jnp.concatenate([after[u]] * 2, axis=1)
                conv = cw[3:4] + s2 * cw[0:1] + s1 * cw[1:2] + hcur * cw[2:3]
                act = jax.nn.gelu(conv[:, SUB_COLS:], approximate=True) * conv[:, :SUB_COLS]
                act_ref[row:row + piece, subs[u]] = act.astype(BF16)
                after[u] = _zero_after(act[piece - SUBLANES:])
                prev[u] = hcur[piece - SUBLANES:]
                if (row + piece) % rows_per_group == 0:
                    la_ref[grp, :, subs[u]] = prev[u][:, :SUB_COLS]
                    lg_ref[grp, :, subs[u]] = prev[u][:, SUB_COLS:]
                    carry_ref[0, grp, :, subs[u]] = prev[u][:, :SUB_COLS]
                    carry_ref[1, grp, :, subs[u]] = prev[u][:, SUB_COLS:]


def _ffn_up(xn, w_up, layer, conv_w, conv_b, state, w_down, *, groups, rows_per_group, bn, name):
    m, k = xn.shape
    nseq = state.shape[0]
    bm = groups * rows_per_group
    ni = m // bm
    tiles_per_seq = ni * groups // nseq
    nj = D_FF // bn
    rc = min(bm, 256)
    cw = jnp.concatenate([conv_w, conv_b[None], jnp.zeros((SUBLANES - CONV_W - 1, 2 * D_FF), F32)], axis=0)
    st = jnp.concatenate([jnp.zeros((nseq, SUBLANES - (CONV_W - 1), 2 * D_FF), F32), state], axis=1)
    cast_down = w_down is not None
    d_out = w_down.shape[2] if cast_down else 0
    out_shape = [jax.ShapeDtypeStruct((m, D_FF), BF16),
                 jax.ShapeDtypeStruct((ni, groups, SUBLANES, D_FF), F32),
                 jax.ShapeDtypeStruct((ni, groups, SUBLANES, D_FF), F32)]
    in_specs = [
        pl.BlockSpec((bm, k), lambda j, i: (i, 0)),
        pl.BlockSpec((None, k, bn), lambda j, i: (layer, 0, j)),
        pl.BlockSpec((None, k, bn), lambda j, i: (layer, 0, nj + j)),
        pl.BlockSpec((SUBLANES, bn), lambda j, i: (0, j)),
        pl.BlockSpec((SUBLANES, bn), lambda j, i: (0, nj + j)),
        pl.BlockSpec((groups, SUBLANES, bn), lambda j, i: (i // tiles_per_seq, 0, j)),
        pl.BlockSpec((groups, SUBLANES, bn), lambda j, i: (i // tiles_per_seq, 0, nj + j)),
    ]
    out_specs = [
        pl.BlockSpec((bm, bn), lambda j, i: (i, j)),
        pl.BlockSpec((None, groups, SUBLANES, bn), lambda j, i: (i, 0, 0, j)),
        pl.BlockSpec((None, groups, SUBLANES, bn), lambda j, i: (i, 0, 0, j)),
    ]
    operands = [xn, w_up, w_up, cw, cw, st, st]
    if cast_down:
        in_specs.append(pl.BlockSpec((None, bn, d_out), lambda j, i: (layer, j, 0)))
        out_specs.append(pl.BlockSpec((None, bn, d_out), lambda j, i: (0, j, 0)))
        out_shape.append(jax.ShapeDtypeStruct((1, D_FF, d_out), BF16))
        operands.append(w_down)
    outs = pl.pallas_call(
        functools.partial(_ffn_up_kernel, groups=groups, rows_per_group=rows_per_group, rc=rc,
                          tiles_per_seq=tiles_per_seq, cast_down=cast_down),
        out_shape=tuple(out_shape),
        grid=(nj, ni),
        in_specs=in_specs,
        out_specs=tuple(out_specs),
        scratch_shapes=[pltpu.VMEM((bn // SUB_COLS, k, 2 * SUB_COLS), BF16),
                        pltpu.VMEM((2, groups, SUBLANES, bn), F32)],
        compiler_params=_params("arbitrary", "arbitrary"),
        name=name,
    )(*operands)
    act, la, lg = outs[:3]
    keep = SUBLANES - (CONV_W - 1)
    ends = slice(tiles_per_seq - 1, ni, tiles_per_seq)
    new_state = jnp.concatenate([la[ends, :, keep:], lg[ends, :, keep:]], axis=-1)
    return act, new_state.reshape(nseq, CONV_W - 1, 2 * D_FF), (outs[3] if cast_down else None)


def _attn_head(q, k, v, sink_ref, h, bias):
    nq = q.shape[0]
    kh = k[:, h * HEAD_DIM:(h + 1) * HEAD_DIM].astype(BF16)
    vh = v[:, h * HEAD_DIM:(h + 1) * HEAD_DIM].astype(BF16)
    qs, sk = [], []
    for gq in range(Q_PER_KV):
        c0 = (h * Q_PER_KV + gq) * HEAD_DIM
        qs.append(q[:, c0:c0 + HEAD_DIM])
        sk.append(jnp.full((nq, 1), sink_ref[h * Q_PER_KV + gq], F32))
    qh = (jnp.concatenate(qs, axis=0) * (HEAD_DIM ** -0.5)).astype(BF16)
    sk = jnp.concatenate(sk, axis=0)
    s = lax.dot_general(qh, kh, (((1,), (1,)), ((), ())), preferred_element_type=F32)
    if bias is not None:
        s = s + bias
    mx = jnp.maximum(jnp.max(s, axis=-1, keepdims=True), sk)
    p = jnp.exp(s - mx)
    den = jnp.sum(p, axis=-1, keepdims=True) + jnp.exp(sk - mx)
    o = jnp.dot((p / den).astype(BF16), vh, preferred_element_type=F32)
    return jnp.concatenate([o[gq * nq:(gq + 1) * nq] for gq in range(Q_PER_KV)], axis=1)


ATTN_UNIT = 2 * CHUNK


def _attn_prompt_kernel(sink_ref, q_ref, kv_ref, kvp_ref, o_ref, bias_ref, *, tq):
    i = pl.program_id(1)
    nk = ATTN_UNIT + WINDOW
    cols = Q_PER_KV * ATTN_UNIT
    kvw = N_KV_HEADS * HEAD_DIM
    r_k = lax.broadcasted_iota(jnp.int32, (nk, cols), 0)
    r_q = lax.broadcasted_iota(jnp.int32, (nk, cols), 1)
    lo = (r_q & (ATTN_UNIT - 1)) & ~(CHUNK - 1)
    band = (r_k >= lo) & (r_k < lo + WINDOW + CHUNK)
    bias_ref[1] = jnp.where(band, 0.0, -jnp.inf)
    first_lo = jnp.where(i == 0, WINDOW, 0)
    bias_ref[0] = jnp.where(band & (r_k >= first_lo), 0.0, -jnp.inf)

    kv_all = jnp.concatenate([kvp_ref[0], kv_ref[0]], axis=0)
    k_all = kv_all[:, :kvw].astype(BF16)
    vt_all = jnp.transpose(kv_all[:, kvw:]).astype(BF16)
    for u in range(tq // ATTN_UNIT):
        rs = slice(u * ATTN_UNIT, (u + 1) * ATTN_UNIT)
        keys = slice(u * ATTN_UNIT, u * ATTN_UNIT + nk)
        k_u = k_all[keys]
        qt = jnp.transpose(q_ref[0, rs, :] * (HEAD_DIM ** -0.5)).astype(BF16)
        bias = bias_ref[min(u, 1)]
        pieces = []
        for h in range(N_KV_HEADS):
            heads = [h * Q_PER_KV + gq for gq in range(Q_PER_KV)]
            qt_h = jnp.concatenate([qt[n * HEAD_DIM:(n + 1) * HEAD_DIM] for n in heads], axis=1)
            parts = [qt_h]
            if h > 0:
                parts.insert(0, jnp.zeros((h * HEAD_DIM, cols), BF16))
            if h < N_KV_HEADS - 1:
                parts.append(jnp.zeros(((N_KV_HEADS - 1 - h) * HEAD_DIM, cols), BF16))
            st = jnp.dot(k_u, jnp.concatenate(parts, axis=0), preferred_element_type=F32) + bias
            sk = jnp.concatenate([jnp.full((1, ATTN_UNIT), sink_ref[n], F32) for n in heads], axis=1)
            mx = jnp.maximum(jnp.max(st, axis=0, keepdims=True), sk)
            p = jnp.exp(st - mx)
            den = jnp.sum(p, axis=0, keepdims=True) + jnp.exp(sk - mx)
            ot = jnp.dot(vt_all[h * HEAD_DIM:(h + 1) * HEAD_DIM, keys], (p / den).astype(BF16),
                         preferred_element_type=F32)
            pieces += [ot[:, gq * ATTN_UNIT:(gq + 1) * ATTN_UNIT] for gq in range(Q_PER_KV)]
        o_ref[0, rs, :] = jnp.transpose(jnp.concatenate(pieces, axis=0)).astype(BF16)


def _attn_prompt(z, sinks, *, tq):
    b, t, _ = z.shape
    per = tq // WINDOW
    return pl.pallas_call(
        functools.partial(_attn_prompt_kernel, tq=tq),
        out_shape=jax.ShapeDtypeStruct((b, t, N_Q_HEADS * HEAD_DIM), BF16),
        grid=(b, t // tq),
        in_specs=[
            pl.BlockSpec(memory_space=pltpu.SMEM),
            pl.BlockSpec((1, tq, 1024), lambda bb, i: (bb, i, 0)),
            pl.BlockSpec((1, tq, COL_BLK), lambda bb, i: (bb, i, KV_BLK)),
            pl.BlockSpec((1, WINDOW, COL_BLK), lambda bb, i: (bb, jnp.maximum(i * per - 1, 0), KV_BLK)),
        ],
        out_specs=pl.BlockSpec((1, tq, 1024), lambda bb, i: (bb, i, 0)),
        scratch_shapes=[pltpu.VMEM((2, ATTN_UNIT + WINDOW, Q_PER_KV * ATTN_UNIT), F32)],
        compiler_params=_params("arbitrary", "arbitrary"),
        name="attn_prompt",
    )(sinks, z, z, z)


def _attn_sample_kernel(sink_ref, q_ref, kv_ref, ck_ref, cv_ref, o_ref):
    kv = kv_ref[0]
    k = jnp.concatenate([ck_ref[0], kv[:, :256]], axis=0)
    v = jnp.concatenate([cv_ref[0], kv[:, 256:]], axis=0)
    q = q_ref[0]
    for h in range(N_KV_HEADS):
        o_ref[0, :, h * 256:(h + 1) * 256] = _attn_head(q, k, v, sink_ref, h, None).astype(BF16)


def _attn_sample(z, cache_k, cache_v, sinks):
    b, t, _ = z.shape
    nc = cache_k.shape[1]
    return pl.pallas_call(
        _attn_sample_kernel,
        out_shape=jax.ShapeDtypeStruct((b, t, N_Q_HEADS * HEAD_DIM), BF16),
        grid=(b,),
        in_specs=[
            pl.BlockSpec(memory_space=pltpu.SMEM),
            pl.BlockSpec((1, t, 1024), lambda bb: (bb, 0, 0)),
            pl.BlockSpec((1, t, COL_BLK), lambda bb: (bb, 0, KV_BLK)),
            pl.BlockSpec((1, nc, 256), lambda bb: (bb, 0, 0)),
            pl.BlockSpec((1, nc, 256), lambda bb: (bb, 0, 0)),
        ],
        out_specs=pl.BlockSpec((1, t, 1024), lambda bb: (bb, 0, 0)),
        compiler_params=_params("arbitrary"),
        name="attn_sample",
    )(sinks, z, z, cache_k.reshape(b, nc, 256), cache_v.reshape(b, nc, 256))


def _retention_kernel(rq_ref, rk_ref, rv_ref, rg_ref, cs_ref, sn_ref, intra_ref, qd_ref, kd_ref, cd_ref,
                      ng_ref, s0_ref, o_ref, st_ref):
    c = pl.program_id(2)

    @pl.when(c == 0)
    def _():
        st_ref[...] = s0_ref[...]

    cs = cs_ref[...]
    sn = sn_ref[...]
    for hh in range(RET_HEADS_PER_BLK):
        lanes = slice(hh * RET_DK, (hh + 1) * RET_DK)
        q = rq_ref[0, :, lanes]
        k = rk_ref[0, :, lanes]
        qr = q * cs + pltpu.roll(q, RET_DK // 2, 1) * sn
        kr = (k * cs + pltpu.roll(k, RET_DK // 2, 1) * sn) * (RET_DK ** -0.5)
        qb = qr.astype(BF16)
        kb = kr.astype(BF16)
        vb = rv_ref[0, :, lanes].astype(BF16)
        sc = lax.dot_general(qb, kb, (((1,), (1,)), ((), ())), preferred_element_type=F32) * intra_ref[hh]
        inner = jnp.dot(sc.astype(BF16), vb, preferred_element_type=F32)
        state = st_ref[0, hh]
        cross = jnp.dot(qb, state.astype(BF16), preferred_element_type=F32) * qd_ref[hh]
        kdt = jnp.transpose(kr * kd_ref[hh]).astype(BF16)
        st_ref[0, hh] = cd_ref[hh, 0:1, :] * state + jnp.dot(kdt, vb, preferred_element_type=F32)
        r = inner + cross
        mu = jnp.mean(r, axis=-1, keepdims=True)
        yc = r - mu
        yn = yc * lax.rsqrt(jnp.mean(yc * yc, axis=-1, keepdims=True) + EPS)
        o_ref[0, :, lanes] = ((yn * ng_ref[:, lanes]) * jax.nn.silu(rg_ref[0, :, lanes])).astype(BF16)


def _ret_log_gamma():
    return jnp.log1p(-jnp.exp2(-5.0 - jnp.arange(N_RET_HEADS, dtype=F32)))


def _retention(z, pos, state0, norm_g, *, chunk):
    b, t, _ = z.shape
    log_g = _ret_log_gamma()
    idx = jnp.arange(chunk, dtype=F32)
    diff = idx[:, None] - idx[None, :]
    intra = jnp.where(diff[None] >= 0.0,
                      jnp.exp(log_g[:, None, None] * jnp.maximum(diff, 0.0)[None]), 0.0)
    ones = jnp.ones((1, 1, RET_DV), F32)
    q_decay = jnp.exp(log_g[:, None] * (idx[None, :] + 1.0))[:, :, None] * ones
    k_decay = jnp.exp(log_g[:, None] * (chunk - 1.0 - idx)[None, :])[:, :, None] * ones
    c_decay = jnp.exp(log_g * chunk)[:, None, None] * jnp.ones((1, SUBLANES, RET_DV), F32)
    half = RET_DK // 2
    freq = 1.0 / (ROPE_BASE ** (jnp.arange(half, dtype=F32) / half))
    ang = pos.astype(F32)[:, None] * freq[None, :]
    cos, sin = jnp.cos(ang), jnp.sin(ang)
    cs = jnp.concatenate([cos, cos], axis=-1)
    sn = jnp.concatenate([-sin, sin], axis=-1)
    nhb = N_RET_HEADS // RET_HEADS_PER_BLK
    hb = RET_HEADS_PER_BLK

    def zspec(blk):
        return pl.BlockSpec((1, chunk, COL_BLK), lambda bb, g, c: (bb, c, blk + g))

    def tab(rows):
        return pl.BlockSpec((hb, rows, RET_DV), lambda bb, g, c: (g, 0, 0))

    out, st = pl.pallas_call(
        _retention_kernel,
        out_shape=(jax.ShapeDtypeStruct((b, t, N_RET_HEADS * RET_DV), BF16),
                   jax.ShapeDtypeStruct((b, N_RET_HEADS, RET_DK, RET_DV), F32)),
        grid=(b, nhb, t // chunk),
        in_specs=[
            zspec(RQ_BLK), zspec(RK_BLK), zspec(RV_BLK), zspec(RG_BLK),
            pl.BlockSpec((chunk, RET_DK), lambda bb, g, c: (c, 0)),
            pl.BlockSpec((chunk, RET_DK), lambda bb, g, c: (c, 0)),
            pl.BlockSpec((hb, chunk, chunk), lambda bb, g, c: (g, 0, 0)),
            tab(chunk), tab(chunk), tab(SUBLANES),
            pl.BlockSpec((1, COL_BLK), lambda bb, g, c: (0, g)),
            pl.BlockSpec((1, hb, RET_DK, RET_DV), lambda bb, g, c: (bb, g, 0, 0)),
        ],
        out_specs=(
            pl.BlockSpec((1, chunk, COL_BLK), lambda bb, g, c: (bb, c, g)),
            pl.BlockSpec((1, hb, RET_DK, RET_DV), lambda bb, g, c: (bb, g, 0, 0)),
        ),
        compiler_params=_params("arbitrary", "arbitrary", "arbitrary"),
        name="retention",
    )(z, z, z, z, cs, sn, intra, q_decay, k_decay, c_decay, norm_g.reshape(1, -1), state0)
    return out, st


def _gate_kernel(u_ref, v_ref, lg_ref, lb_ref, ws_ref, bs_ref, y_ref, *maybe_vn_ref, rows, span):
    ri = lax.broadcasted_iota(jnp.int32, (span, span), 0)
    ci = lax.broadcasted_iota(jnp.int32, (span, span), 1)
    wt = [jnp.where(ri >= ci, ws_ref[g], 0.0).astype(BF16) for g in range(GM_GROUPS)]
    bs = bs_ref[...]
    lg = lg_ref[...]
    lb = lb_ref[...]
    for c in range(rows // span):
        rs = slice(c * span, (c + 1) * span)
        v = v_ref[0, rs, :]
        mu = jnp.mean(v, axis=-1, keepdims=True)
        xc = v - mu
        vn = (xc * lax.rsqrt(jnp.mean(xc * xc, axis=-1, keepdims=True) + EPS)) * lg + lb
        for vn_ref in maybe_vn_ref:
            vn_ref[0, rs, :] = vn
        vb = vn.astype(BF16)
        for g in range(GM_GROUPS):
            cols = slice(g * GM_GROUP_DIM, (g + 1) * GM_GROUP_DIM)
            mixed = jnp.dot(wt[g], vb[:, cols], preferred_element_type=F32) + bs[:, g:g + 1]
            y_ref[0, rs, cols] = (u_ref[0, rs, cols] * mixed).astype(BF16)


def _spatial_gate(uv, ln_g, ln_b, ws, bs, *, rows, emit_vn):
    b, t, _ = uv.shape
    span = min(t, GM_CHUNK)
    ws_l = ws[:, :span, :span]
    bs_t = jnp.transpose(bs[:, :span])
    tile = pl.BlockSpec((1, rows, D_MODEL), lambda bb, i: (bb, i, 0))
    outs = pl.pallas_call(
        functools.partial(_gate_kernel, rows=rows, span=span),
        out_shape=(jax.ShapeDtypeStruct((b, t, D_MODEL), BF16),)
        + ((jax.ShapeDtypeStruct((b, t, D_MODEL), F32),) if emit_vn else ()),
        grid=(b, t // rows),
        in_specs=[
            pl.BlockSpec((1, rows, D_MODEL), lambda bb, i: (bb, i, 0)),
            pl.BlockSpec((1, rows, D_MODEL), lambda bb, i: (bb, i, 1)),
            pl.BlockSpec((1, D_MODEL), lambda bb, i: (0, 0)),
            pl.BlockSpec((1, D_MODEL), lambda bb, i: (0, 0)),
            pl.BlockSpec((GM_GROUPS, span, span), lambda bb, i: (0, 0, 0)),
            pl.BlockSpec((span, GM_GROUPS), lambda bb, i: (0, 0)),
        ],
        out_specs=(tile, tile) if emit_vn else (tile,),
        compiler_params=_params("arbitrary", "arbitrary"),
        name="spatial_gate",
    )(uv, uv, ln_g.reshape(1, -1), ln_b.reshape(1, -1), ws_l, bs_t)
    return outs if emit_vn else (outs[0], None)


ROW_TILE = 1024
FFN_ROW_TILE = 2048
DOWN_ROW_TILE = 256
OUT_ROW_TILE = 512
OUT_K_BLK = 1024
IN_EVEN_BLK = 1408
IN_ODD_BLK = 1024
FFN_BLK = 512
ATTN_ROW_TILE = 512


def _conv_ffn(hp, xp, hs, xs, g3, g_next, w_up, layer, conv_w, conv_b, w_down, state_p, state_s):
    nseq_p, nseq_s = state_p.shape[0], state_s.shape[0]
    act_p, c_p, wdb = _ffn_up(xp, w_up, layer, conv_w, conv_b, state_p, w_down, groups=1,
                              rows_per_group=FFN_ROW_TILE, bn=FFN_BLK, name="ffn_up_p")
    act_s, c_s, _ = _ffn_up(xs, w_up, layer, conv_w, conv_b, state_s, None, groups=nseq_s,
                            rows_per_group=xs.shape[0] // nseq_s, bn=FFN_BLK, name="ffn_up_s")
    wdb = wdb.reshape(wdb.shape[1:])
    hp, xp = _down_norm_res(act_p, wdb, g3, hp, g_next, bm=DOWN_ROW_TILE, name="ffn_down_p")
    hs, xs = _down_norm_res(act_s, wdb, g3, hs, g_next, bm=DOWN_ROW_TILE, name="ffn_down_s")
    return hp, xp, c_p, hs, xs, c_s


def kernel(x_prompt, x_sample, cache_swa_k, cache_swa_v, state_ret, state_ffn_conv, norm_g, w_in_even,
           w_out_even, attn_sinks, ret_norm_g, w_in_odd, w_out_odd, gm_ln_g, gm_ln_b, gm_ws, gm_bs,
           ffn_w_up, ffn_conv_w, ffn_conv_b, ffn_w_down):
    nb, seq, d = x_prompt.shape
    db, dseq, _ = x_sample.shape
    depth = norm_g.shape[0]
    hp = x_prompt.reshape(nb * seq, d)
    hs = x_sample.reshape(db * dseq, d)
    bm_p, bm_s = ROW_TILE, db * dseq
    pos_p = jnp.arange(seq)
    pos_s = PAST_LEN + jnp.arange(dseq)
    kp_l, vp_l, rp_l, cp_l = [], [], [], []
    ks_l, vs_l, rs_l, cs_l, gv_l = [], [], [], [], []
    xp = _rmsnorm(hp, norm_g[0, 0], bm=ROW_TILE, name="norm_in_p")
    xs = _rmsnorm(hs, norm_g[0, 0], bm=bm_s, name="norm_in_s")
    for layer in range(depth):
        g = norm_g[layer]
        g_next = norm_g[layer + 1, 0] if layer + 1 < depth else None
        if layer % 2 == 0:
            e = layer // 2
            zp = _xw(xp, w_in_even, e, bm=bm_p, bn=IN_EVEN_BLK, gelu=False, name="in_even_p")
            zs = _xw(xs, w_in_even, e, bm=bm_s, bn=IN_EVEN_BLK, gelu=False, name="in_even_s")
            zp = zp.reshape(nb, seq, EVEN_IN)
            zs = zs.reshape(db, dseq, EVEN_IN)
            attn_p = _attn_prompt(zp, attn_sinks[e], tq=ATTN_ROW_TILE)
            attn_s = _attn_sample(zs, cache_swa_k[e], cache_swa_v[e], attn_sinks[e])
            ret_p, r_p = _retention(zp, pos_p, jnp.zeros((nb, N_RET_HEADS, RET_DK, RET_DV), F32),
                                    ret_norm_g[e], chunk=RET_CHUNK)
            ret_s, r_s = _retention(zs, pos_s, state_ret[e].astype(F32), ret_norm_g[e], chunk=dseq)
            mixed_p = [attn_p.reshape(nb * seq, -1), ret_p.reshape(nb * seq, -1)]
            mixed_s = [attn_s.reshape(db * dseq, -1), ret_s.reshape(db * dseq, -1)]
            hp, xp = _matmul_norm_res(mixed_p, w_out_even, e, g[1], hp, g[2], bm=OUT_ROW_TILE, bk=OUT_K_BLK,
                                      resident=True, name="out_even_p")
            hs, xs = _matmul_norm_res(mixed_s, w_out_even, e, g[1], hs, g[2], bm=bm_s, bk=OUT_K_BLK,
                                      resident=False, name="out_even_s")
            k_new = zs[:, :, 1024:1280].reshape(db, dseq, N_KV_HEADS, HEAD_DIM)
            v_new = zs[:, :, 1280:1536].reshape(db, dseq, N_KV_HEADS, HEAD_DIM)
            n_keep = cache_swa_k.shape[2]
            kp_l.append(zp[:, seq - WINDOW:, 1024:1280].reshape(nb, WINDOW, N_KV_HEADS, HEAD_DIM))
            vp_l.append(zp[:, seq - WINDOW:, 1280:1536].reshape(nb, WINDOW, N_KV_HEADS, HEAD_DIM))
            ks_l.append(jnp.concatenate([cache_swa_k[e], k_new], axis=1)[:, -n_keep:])
            vs_l.append(jnp.concatenate([cache_swa_v[e], v_new], axis=1)[:, -n_keep:])
            rp_l.append(r_p)
            rs_l.append(r_s.astype(state_ret.dtype))
        else:
            o = layer // 2
            uvp = _xw(xp, w_in_odd, o, bm=bm_p, bn=IN_ODD_BLK, gelu=True, name="in_odd_p")
            uvs = _xw(xs, w_in_odd, o, bm=bm_s, bn=IN_ODD_BLK, gelu=True, name="in_odd_s")
            yp, _ = _spatial_gate(uvp.reshape(nb, seq, -1), gm_ln_g[o], gm_ln_b[o], gm_ws[o], gm_bs[o], rows=512,
                                  emit_vn=False)
            ys, gv = _spatial_gate(uvs.reshape(db, dseq, -1), gm_ln_g[o], gm_ln_b[o], gm_ws[o], gm_bs[o],
                                   rows=dseq, emit_vn=True)
            hp, xp = _matmul_norm_res([yp.reshape(nb * seq, -1)], w_out_odd, o, g[1], hp, g[2], bm=OUT_ROW_TILE,
                                      bk=OUT_K_BLK, resident=True, name="out_odd_p")
            hs, xs = _matmul_norm_res([ys.reshape(db * dseq, -1)], w_out_odd, o, g[1], hs, g[2], bm=bm_s,
                                      bk=OUT_K_BLK, resident=False, name="out_odd_s")
            gv_l.append(gv)
        zero_state = jnp.zeros((nb, CONV_W - 1, 2 * D_FF), F32)
        hp, xp, c_p, hs, xs, c_s = _conv_ffn(hp, xp, hs, xs, g[3], g_next, ffn_w_up, layer, ffn_conv_w[layer],
                                             ffn_conv_b[layer], ffn_w_down, zero_state, state_ffn_conv[layer])
        cp_l.append(c_p)
        cs_l.append(c_s)
    return (hp.reshape(nb, seq, d), hs.reshape(db, dseq, d),
            jnp.stack(kp_l), jnp.stack(vp_l), jnp.stack(rp_l), jnp.stack(cp_l),
            jnp.stack(ks_l), jnp.stack(vs_l), jnp.stack(rs_l), jnp.stack(cs_l), jnp.stack(gv_l))
```

```python
import functools

import jax
import jax.numpy as jnp
from jax import lax
from jax.experimental import pallas as pl
from jax.experimental.pallas import tpu as pltpu

F32 = jnp.float32
BF16 = jnp.bfloat16

D_MODEL = 2048
CHUNK = 64
HEAD_DIM = 64
N_Q_HEADS = 16
N_KV_HEADS = 4
Q_PER_KV = N_Q_HEADS // N_KV_HEADS
WINDOW = 128
N_RET_HEADS = 8
RET_DK = 128
RET_DV = 128
ROPE_BASE = 10000.0
GM_CHUNK = 128
GM_GROUPS = 8
GM_GROUP_DIM = D_MODEL // GM_GROUPS
D_FF = 5632
CONV_W = 3
EPS = 1e-6
PAST_LEN = 1024
EVEN_IN = 5632

SUBLANES = 8
VMEM_LIMIT_BYTES = 56 * 1024 * 1024

COL_BLK = 512
KV_BLK = 2
RQ_BLK, RK_BLK, RV_BLK, RG_BLK = 3, 5, 7, 9
RET_HEADS_PER_BLK = COL_BLK // RET_DK
RET_CHUNK = 256
SUB_COLS = 256
EPI_ROWS = 16
NORM_ROWS = 64


def _params(*sem):
    return pltpu.CompilerParams(dimension_semantics=sem, vmem_limit_bytes=VMEM_LIMIT_BYTES)


def _rmsnorm_rows(x_ref, g_ref, out_ref, rows, chunk):
    g = g_ref[...]

    def body(c, carry):
        r0 = pl.multiple_of(c * chunk, chunk)
        x = x_ref[pl.ds(r0, chunk), :]
        ms = jnp.mean(x * x, axis=-1, keepdims=True)
        out_ref[pl.ds(r0, chunk), :] = ((x * lax.rsqrt(ms + EPS)) * g).astype(out_ref.dtype)
        return carry

    lax.fori_loop(0, rows // chunk, body, 0)


def _zero_after(x):
    z = pltpu.bitcast(x, jnp.uint32)
    z = lax.shift_right_logical(lax.shift_right_logical(z, jnp.uint32(16)), jnp.uint32(16))
    return pltpu.bitcast(z, F32)


def _rmsnorm_kernel(x_ref, g_ref, o_ref, *, bm):
    _rmsnorm_rows(x_ref, g_ref, o_ref, bm, min(bm, 128))


def _rmsnorm(x, g, *, bm, name):
    m, k = x.shape
    return pl.pallas_call(
        functools.partial(_rmsnorm_kernel, bm=bm),
        out_shape=jax.ShapeDtypeStruct((m, k), BF16),
        grid=(m // bm,),
        in_specs=[pl.BlockSpec((bm, k), lambda i: (i, 0)), pl.BlockSpec((1, k), lambda i: (0, 0))],
        out_specs=pl.BlockSpec((bm, k), lambda i: (i, 0)),
        compiler_params=_params("arbitrary"),
        name=name,
    )(x, g.reshape(1, k))


def _xw_kernel(xn_ref, w_ref, o_ref, wb_ref, *, bm, rc, gelu):
    @pl.when(pl.program_id(1) == 0)
    def _():
        wb_ref[...] = w_ref[...].astype(BF16)

    for c in range(bm // rc):
        y = jnp.dot(xn_ref[c * rc:(c + 1) * rc, :], wb_ref[...], preferred_element_type=F32)
        if gelu:
            y = jax.nn.gelu(y, approximate=True)
        o_ref[c * rc:(c + 1) * rc, :] = y.astype(o_ref.dtype)


def _xw(xn, w, layer, *, bm, bn, gelu, name):
    m, k = xn.shape
    n = w.shape[2]
    rc = min(bm, 256)
    return pl.pallas_call(
        functools.partial(_xw_kernel, bm=bm, rc=rc, gelu=gelu),
        out_shape=jax.ShapeDtypeStruct((m, n), F32),
        grid=(n // bn, m // bm),
        in_specs=[
            pl.BlockSpec((bm, k), lambda j, i: (i, 0)),
            pl.BlockSpec((None, k, bn), lambda j, i: (layer, 0, j)),
        ],
        out_specs=pl.BlockSpec((bm, bn), lambda j, i: (i, j)),
        scratch_shapes=[pltpu.VMEM((k, bn), BF16)],
        compiler_params=_params("arbitrary", "arbitrary"),
        name=name,
    )(xn, w)


def _matmul_norm_res_kernel(*refs, splits, nk, bm, rc, emit_next, weights):
    na = len(splits)
    a_refs = refs[:na]
    rest = list(refs[na:])
    wb_ref = rest.pop() if weights != "bf16" else None
    if emit_next:
        w_ref, g_ref, r_ref, gn_ref, o_ref, xn_ref = rest
    else:
        w_ref, g_ref, r_ref, o_ref = rest
    i = pl.program_id(0)
    k = pl.program_id(1)
    if weights == "stream":
        wb_ref[...] = w_ref[...].astype(BF16)
        w_blk = wb_ref
    elif weights == "resident":
        @pl.when(i == 0)
        def _():
            wb_ref[k] = w_ref[...].astype(BF16)

        w_blk = wb_ref.at[k]
    else:
        w_blk = w_ref

    def finish(r0, y):
        rows = slice(r0, r0 + NORM_ROWS)
        ms = jnp.mean(y * y, axis=-1, keepdims=True)
        h = r_ref[rows, :] + (y * lax.rsqrt(ms + EPS)) * g_ref[...]
        o_ref[rows, :] = h
        if emit_next:
            ms2 = jnp.mean(h * h, axis=-1, keepdims=True)
            xn_ref[rows, :] = ((h * lax.rsqrt(ms2 + EPS)) * gn_ref[...]).astype(BF16)

    def step(a_ref, first, last):
        for c in range(bm // rc):
            rows = slice(c * rc, (c + 1) * rc)
            y = jnp.dot(a_ref[rows, :], w_blk[...], preferred_element_type=F32)
            if not first:
                y = o_ref[rows, :] + y
            if last:
                for r in range(0, rc, NORM_ROWS):
                    finish(c * rc + r, y[r:r + NORM_ROWS])
            else:
                o_ref[rows, :] = y

    for a_ref, (k0, k1) in zip(a_refs, splits):
        for first, last in sorted({(kk == 0, kk == nk - 1) for kk in range(k0, k1)}):
            ks = [kk for kk in range(k0, k1) if (kk == 0, kk == nk - 1) == (first, last)]
            pl.when((k >= ks[0]) & (k <= ks[-1]))(functools.partial(step, a_ref, first, last))


def _matmul_norm_res(a_list, w, layer, g, resid, g_next, *, bm, bk, resident, name):
    m = a_list[0].shape[0]
    n = w.shape[2]
    splits, k0 = [], 0
    for a in a_list:
        splits.append((k0, k0 + a.shape[1] // bk))
        k0 = splits[-1][1]
    nk = k0
    rc = min(bm, 256)
    emit_next = g_next is not None
    weights = "bf16" if w.dtype == BF16 else ("resident" if resident else "stream")

    def a_spec(k0, k1):
        return pl.BlockSpec((bm, bk), lambda i, k: (i, jnp.clip(k - k0, 0, k1 - k0 - 1)))

    if weights == "resident":
        w_spec = pl.BlockSpec((None, bk, n), lambda i, k: (layer, jnp.where(i == 0, k, nk - 1), 0))
        scratch = [pltpu.VMEM((nk, bk, n), BF16)]
    else:
        w_spec = pl.BlockSpec((None, bk, n), lambda i, k: (layer, k, 0))
        scratch = [pltpu.VMEM((bk, n), BF16)] if weights == "stream" else []
    row_vec = pl.BlockSpec((1, n), lambda i, k: (0, 0))
    tile = pl.BlockSpec((bm, n), lambda i, k: (i, 0))
    out = pl.pallas_call(
        functools.partial(_matmul_norm_res_kernel, splits=tuple(splits), nk=nk, bm=bm, rc=rc,
                          emit_next=emit_next, weights=weights),
        out_shape=((jax.ShapeDtypeStruct((m, n), F32), jax.ShapeDtypeStruct((m, n), BF16)) if emit_next
                   else jax.ShapeDtypeStruct((m, n), F32)),
        grid=(m // bm, nk),
        in_specs=[a_spec(*sp) for sp in splits] + [w_spec, row_vec, tile] + ([row_vec] if emit_next else []),
        out_specs=(tile, tile) if emit_next else tile,
        scratch_shapes=scratch,
        compiler_params=_params("arbitrary", "arbitrary"),
        name=name,
    )(*a_list, w, g.reshape(1, n), resid, *([g_next.reshape(1, n)] if emit_next else []))
    return out if emit_next else (out, None)


def _down_kernel(*refs, bm, emit_next):
    if emit_next:
        a_ref, w_hbm_ref, g_ref, r_ref, gn_ref, o_ref, xn_ref, w_ref, sem = refs
    else:
        a_ref, w_hbm_ref, g_ref, r_ref, o_ref, w_ref, sem = refs

    @pl.when(pl.program_id(0) == 0)
    def _():
        copy = pltpu.make_async_copy(w_hbm_ref, w_ref, sem)
        copy.start()
        copy.wait()

    y = jnp.dot(a_ref[...], w_ref[...], preferred_element_type=F32)
    for r in range(0, bm, NORM_ROWS):
        rows = slice(r, r + NORM_ROWS)
        yy = y[rows]
        ms = jnp.mean(yy * yy, axis=-1, keepdims=True)
        h = r_ref[rows, :] + (yy * lax.rsqrt(ms + EPS)) * g_ref[...]
        o_ref[rows, :] = h
        if emit_next:
            ms2 = jnp.mean(h * h, axis=-1, keepdims=True)
            xn_ref[rows, :] = ((h * lax.rsqrt(ms2 + EPS)) * gn_ref[...]).astype(BF16)


def _down_norm_res(a, w, g, resid, g_next, *, bm, name):
    m, kdim = a.shape
    n = w.shape[1]
    emit_next = g_next is not None
    row_vec = pl.BlockSpec((1, n), lambda i: (0, 0))
    tile = pl.BlockSpec((bm, n), lambda i: (i, 0))
    out = pl.pallas_call(
        functools.partial(_down_kernel, bm=bm, emit_next=emit_next),
        out_shape=((jax.ShapeDtypeStruct((m, n), F32), jax.ShapeDtypeStruct((m, n), BF16)) if emit_next
                   else jax.ShapeDtypeStruct((m, n), F32)),
        grid=(m // bm,),
        in_specs=[pl.BlockSpec((bm, kdim), lambda i: (i, 0)), pl.BlockSpec(memory_space=pl.ANY), row_vec, tile]
        + ([row_vec] if emit_next else []),
        out_specs=(tile, tile) if emit_next else tile,
        scratch_shapes=[pltpu.VMEM((kdim, n), BF16), pltpu.SemaphoreType.DMA(())],
        compiler_params=_params("arbitrary"),
        name=name,
    )(a, w, g.reshape(1, n), resid, *([g_next.reshape(1, n)] if emit_next else []))
    return out if emit_next else (out, None)


def _ffn_up_kernel(*refs, groups, rows_per_group, rc, tiles_per_seq, cast_down):
    if cast_down:
        (xn_ref, wa_ref, wg_ref, cwa_ref, cwg_ref, sa_ref, sg_ref, wd_ref,
         act_ref, la_ref, lg_ref, wdb_ref, wb_ref, carry_ref) = refs
    else:
        (xn_ref, wa_ref, wg_ref, cwa_ref, cwg_ref, sa_ref, sg_ref,
         act_ref, la_ref, lg_ref, wb_ref, carry_ref) = refs
    i = pl.program_id(1)
    bm = groups * rows_per_group
    bn = act_ref.shape[1]
    nsub = bn // SUB_COLS
    subs = [slice(u * SUB_COLS, (u + 1) * SUB_COLS) for u in range(nsub)]

    @pl.when(i == 0)
    def _():
        for u in range(nsub):
            wb_ref[u, :, :SUB_COLS] = wa_ref[:, subs[u]].astype(BF16)
            wb_ref[u, :, SUB_COLS:] = wg_ref[:, subs[u]].astype(BF16)
        if cast_down:
            wdb_ref[...] = wd_ref[...].astype(BF16)

    keep = SUBLANES - (CONV_W - 1)

    @pl.when(lax.rem(i, tiles_per_seq) == 0)
    def _():
        carry_ref[...] = jnp.zeros(carry_ref.shape, F32)
        carry_ref[0, :, keep:, :] = sa_ref[...]
        carry_ref[1, :, keep:, :] = sg_ref[...]

    cws = [jnp.concatenate([cwa_ref[:, subs[u]], cwg_ref[:, subs[u]]], axis=1) for u in range(nsub)]
    piece = min(EPI_ROWS, rows_per_group)
    prev = [None] * nsub
    after = [None] * nsub
    for c in range(bm // rc):
        xc = xn_ref[c * rc:(c + 1) * rc, :]
        for u in range(nsub):
            h = jnp.dot(xc, wb_ref[u], preferred_element_type=F32)
            for q in range(rc // piece):
                row = c * rc + q * piece
                grp = row // rows_per_group
                hcur = h[q * piece:(q + 1) * piece]
                if row % rows_per_group == 0:
                    prev8 = jnp.concatenate([carry_ref[0, grp, :, subs[u]], carry_ref[1, grp, :, subs[u]]], axis=1)
                else:
                    prev8 = prev[u]
                hext = jnp.concatenate([prev8, hcur], axis=0)
                s1 = pltpu.roll(hext, 1, 0)[SUBLANES:]
                s2 = pltpu.roll(hext, 2, 0)[SUBLANES:]
                cw = cws[u] if after[u] is None else cws[u] + jnp.concatenate([after[u]] * 2, axis=1)
                conv = cw[3:4] + s2 * cw[0:1] + s1 * cw[1:2] + hcur * cw[2:3]
                act = jax.nn.gelu(conv[:, SUB_COLS:], approximate=True) * conv[:, :SUB_COLS]
                act_ref[row:row + piece, subs[u]] = act.astype(BF16)
                after[u] = _zero_after(act[piece - SUBLANES:])
                prev[u] = hcur[piece - SUBLANES:]
                if (row + piece) % rows_per_group == 0:
                    la_ref[grp, :, subs[u]] = prev[u][keep:, :SUB_COLS]
                    lg_ref[grp, :, subs[u]] = prev[u][keep:, SUB_COLS:]
                    carry_ref[0, grp, :, subs[u]] = prev[u][:, :SUB_COLS]
                    carry_ref[1, grp, :, subs[u]] = prev[u][:, SUB_COLS:]


def _ffn_up(xn, w_up, layer, conv_w, conv_b, state, w_down, *, groups, rows_per_group, bn, name):
    m, k = xn.shape
    nseq = state.shape[0]
    bm = groups * rows_per_group
    ni = m // bm
    tiles_per_seq = ni * groups // nseq
    nj = D_FF // bn
    rc = min(bm, 256)
    cw = jnp.concatenate([conv_w, conv_b[None], jnp.zeros((SUBLANES - CONV_W - 1, 2 * D_FF), F32)], axis=0)
    cast_down = w_down is not None
    d_out = w_down.shape[2] if cast_down else 0
    out_shape = [jax.ShapeDtypeStruct((m, D_FF), BF16),
                 jax.ShapeDtypeStruct((nseq, CONV_W - 1, D_FF), F32),
                 jax.ShapeDtypeStruct((nseq, CONV_W - 1, D_FF), F32)]
    in_specs = [
        pl.BlockSpec((bm, k), lambda j, i: (i, 0)),
        pl.BlockSpec((None, k, bn), lambda j, i: (layer, 0, j)),
        pl.BlockSpec((None, k, bn), lambda j, i: (layer, 0, nj + j)),
        pl.BlockSpec((SUBLANES, bn), lambda j, i: (0, j)),
        pl.BlockSpec((SUBLANES, bn), lambda j, i: (0, nj + j)),
        pl.BlockSpec((groups, CONV_W - 1, bn), lambda j, i: (i // tiles_per_seq, 0, j)),
        pl.BlockSpec((groups, CONV_W - 1, bn), lambda j, i: (i // tiles_per_seq, 0, nj + j)),
    ]
    out_specs = [
        pl.BlockSpec((bm, bn), lambda j, i: (i, j)),
        pl.BlockSpec((groups, CONV_W - 1, bn), lambda j, i: (i // tiles_per_seq, 0, j)),
        pl.BlockSpec((groups, CONV_W - 1, bn), lambda j, i: (i // tiles_per_seq, 0, j)),
    ]
    operands = [xn, w_up, w_up, cw, cw, state, state]
    if cast_down:
        in_specs.append(pl.BlockSpec((None, bn, d_out), lambda j, i: (layer, j, 0)))
        out_specs.append(pl.BlockSpec((None, bn, d_out), lambda j, i: (0, j, 0)))
        out_shape.append(jax.ShapeDtypeStruct((1, D_FF, d_out), BF16))
        operands.append(w_down)
    outs = pl.pallas_call(
        functools.partial(_ffn_up_kernel, groups=groups, rows_per_group=rows_per_group, rc=rc,
                          tiles_per_seq=tiles_per_seq, cast_down=cast_down),
        out_shape=tuple(out_shape),
        grid=(nj, ni),
        in_specs=in_specs,
        out_specs=tuple(out_specs),
        scratch_shapes=[pltpu.VMEM((bn // SUB_COLS, k, 2 * SUB_COLS), BF16),
                        pltpu.VMEM((2, groups, SUBLANES, bn), F32)],
        compiler_params=_params("arbitrary", "arbitrary"),
        name=name,
    )(*operands)
    act, la, lg = outs[:3]
    return act, jnp.concatenate([la, lg], axis=-1), (outs[3] if cast_down else None)


def _attn_head(q, k, v, sink_ref, h, bias):
    nq = q.shape[0]
    kh = k[:, h * HEAD_DIM:(h + 1) * HEAD_DIM].astype(BF16)
    vh = v[:, h * HEAD_DIM:(h + 1) * HEAD_DIM].astype(BF16)
    qs, sk = [], []
    for gq in range(Q_PER_KV):
        c0 = (h * Q_PER_KV + gq) * HEAD_DIM
        qs.append(q[:, c0:c0 + HEAD_DIM])
        sk.append(jnp.full((nq, 1), sink_ref[h * Q_PER_KV + gq], F32))
    qh = (jnp.concatenate(qs, axis=0) * (HEAD_DIM ** -0.5)).astype(BF16)
    sk = jnp.concatenate(sk, axis=0)
    s = lax.dot_general(qh, kh, (((1,), (1,)), ((), ())), preferred_element_type=F32)
    if bias is not None:
        s = s + bias
    mx = jnp.maximum(jnp.max(s, axis=-1, keepdims=True), sk)
    p = jnp.exp(s - mx)
    den = jnp.sum(p, axis=-1, keepdims=True) + jnp.exp(sk - mx)
    o = jnp.dot((p / den).astype(BF16), vh, preferred_element_type=F32)
    return jnp.concatenate([o[gq * nq:(gq + 1) * nq] for gq in range(Q_PER_KV)], axis=1)


ATTN_UNIT = 2 * CHUNK


def _attn_prompt_kernel(sink_ref, q_ref, kv_ref, kvp_ref, o_ref, bias_ref, *, tq):
    i = pl.program_id(1)
    nk = ATTN_UNIT + WINDOW
    cols = Q_PER_KV * ATTN_UNIT
    kvw = N_KV_HEADS * HEAD_DIM
    @pl.when(i == 0)
    def _():
        r_k = lax.broadcasted_iota(jnp.int32, (nk, cols), 0)
        r_q = lax.broadcasted_iota(jnp.int32, (nk, cols), 1)
        lo = (r_q & (ATTN_UNIT - 1)) & ~(CHUNK - 1)
        band = (r_k >= lo) & (r_k < lo + WINDOW + CHUNK)
        bias_ref[1] = jnp.where(band, 0.0, -jnp.inf)
        bias_ref[0] = jnp.where(band & (r_k >= WINDOW), 0.0, -jnp.inf)

    first_unit_bias = jnp.where(i == 0, 0, 1)

    kv_all = jnp.concatenate([kvp_ref[0], kv_ref[0]], axis=0)
    k_all = kv_all[:, :kvw].astype(BF16)
    vt_all = jnp.transpose(kv_all[:, kvw:]).astype(BF16)
    for u in range(tq // ATTN_UNIT):
        rs = slice(u * ATTN_UNIT, (u + 1) * ATTN_UNIT)
        keys = slice(u * ATTN_UNIT, u * ATTN_UNIT + nk)
        k_u = k_all[keys]
        qt = jnp.transpose(q_ref[0, rs, :] * (HEAD_DIM ** -0.5)).astype(BF16)
        bias = bias_ref[first_unit_bias] if u == 0 else bias_ref[1]
        pieces = []
        for h in range(N_KV_HEADS):
            heads = [h * Q_PER_KV + gq for gq in range(Q_PER_KV)]
            qt_h = jnp.concatenate([qt[n * HEAD_DIM:(n + 1) * HEAD_DIM] for n in heads], axis=1)
            parts = [qt_h]
            if h > 0:
                parts.insert(0, jnp.zeros((h * HEAD_DIM, cols), BF16))
            if h < N_KV_HEADS - 1:
                parts.append(jnp.zeros(((N_KV_HEADS - 1 - h) * HEAD_DIM, cols), BF16))
            st = jnp.dot(k_u, jnp.concatenate(parts, axis=0), preferred_element_type=F32) + bias
            sk = jnp.concatenate([jnp.full((1, ATTN_UNIT), sink_ref[n], F32) for n in heads], axis=1)
            mx = jnp.maximum(jnp.max(st, axis=0, keepdims=True), sk)
            p = jnp.exp(st - mx)
            den = jnp.sum(p, axis=0, keepdims=True) + jnp.exp(sk - mx)
            ot = jnp.dot(vt_all[h * HEAD_DIM:(h + 1) * HEAD_DIM, keys], (p / den).astype(BF16),
                         preferred_element_type=F32)
            pieces += [ot[:, gq * ATTN_UNIT:(gq + 1) * ATTN_UNIT] for gq in range(Q_PER_KV)]
        o_ref[0, rs, :] = jnp.transpose(jnp.concatenate(pieces, axis=0)).astype(BF16)


def _attn_prompt(z, sinks, *, tq):
    b, t, _ = z.shape
    per = tq // WINDOW
    return pl.pallas_call(
        functools.partial(_attn_prompt_kernel, tq=tq),
        out_shape=jax.ShapeDtypeStruct((b, t, N_Q_HEADS * HEAD_DIM), BF16),
        grid=(b, t // tq),
        in_specs=[
            pl.BlockSpec(memory_space=pltpu.SMEM),
            pl.BlockSpec((1, tq, 1024), lambda bb, i: (bb, i, 0)),
            pl.BlockSpec((1, tq, COL_BLK), lambda bb, i: (bb, i, KV_BLK)),
            pl.BlockSpec((1, WINDOW, COL_BLK), lambda bb, i: (bb, jnp.maximum(i * per - 1, 0), KV_BLK)),
        ],
        out_specs=pl.BlockSpec((1, tq, 1024), lambda bb, i: (bb, i, 0)),
        scratch_shapes=[pltpu.VMEM((2, ATTN_UNIT + WINDOW, Q_PER_KV * ATTN_UNIT), F32)],
        compiler_params=_params("arbitrary", "arbitrary"),
        name="attn_prompt",
    )(sinks, z, z, z)


def _attn_sample_kernel(sink_ref, q_ref, kv_ref, ck_ref, cv_ref, o_ref):
    kv = kv_ref[0]
    k = jnp.concatenate([ck_ref[0], kv[:, :256]], axis=0)
    v = jnp.concatenate([cv_ref[0], kv[:, 256:]], axis=0)
    q = q_ref[0]
    for h in range(N_KV_HEADS):
        o_ref[0, :, h * 256:(h + 1) * 256] = _attn_head(q, k, v, sink_ref, h, None).astype(BF16)


def _attn_sample(z, cache_k, cache_v, sinks):
    b, t, _ = z.shape
    nc = cache_k.shape[1]
    return pl.pallas_call(
        _attn_sample_kernel,
        out_shape=jax.ShapeDtypeStruct((b, t, N_Q_HEADS * HEAD_DIM), BF16),
        grid=(b,),
        in_specs=[
            pl.BlockSpec(memory_space=pltpu.SMEM),
            pl.BlockSpec((1, t, 1024), lambda bb: (bb, 0, 0)),
            pl.BlockSpec((1, t, COL_BLK), lambda bb: (bb, 0, KV_BLK)),
            pl.BlockSpec((1, nc, 256), lambda bb: (bb, 0, 0)),
            pl.BlockSpec((1, nc, 256), lambda bb: (bb, 0, 0)),
        ],
        out_specs=pl.BlockSpec((1, t, 1024), lambda bb: (bb, 0, 0)),
        compiler_params=_params("arbitrary"),
        name="attn_sample",
    )(sinks, z, z, cache_k.reshape(b, nc, 256), cache_v.reshape(b, nc, 256))


def _retention_kernel(rq_ref, rk_ref, rv_ref, rg_ref, cs_ref, sn_ref, intra_ref, qd_ref, kd_ref, cd_ref,
                      ng_ref, s0_ref, o_ref, st_ref):
    c = pl.program_id(2)

    @pl.when(c == 0)
    def _():
        st_ref[...] = s0_ref[...]

    cs = cs_ref[...]
    sn = sn_ref[...]
    for hh in range(RET_HEADS_PER_BLK):
        lanes = slice(hh * RET_DK, (hh + 1) * RET_DK)
        q = rq_ref[0, :, lanes]
        k = rk_ref[0, :, lanes]
        qr = q * cs + pltpu.roll(q, RET_DK // 2, 1) * sn
        kr = (k * cs + pltpu.roll(k, RET_DK // 2, 1) * sn) * (RET_DK ** -0.5)
        qb = qr.astype(BF16)
        kb = kr.astype(BF16)
        vb = rv_ref[0, :, lanes].astype(BF16)
        sc = lax.dot_general(qb, kb, (((1,), (1,)), ((), ())), preferred_element_type=F32) * intra_ref[hh]
        inner = jnp.dot(sc.astype(BF16), vb, preferred_element_type=F32)
        state = st_ref[0, hh]
        cross = jnp.dot(qb, state.astype(BF16), preferred_element_type=F32) * qd_ref[hh]
        kdt = jnp.transpose(kr * kd_ref[hh]).astype(BF16)
        st_ref[0, hh] = cd_ref[hh, 0:1, :] * state + jnp.dot(kdt, vb, preferred_element_type=F32)
        r = inner + cross
        mu = jnp.mean(r, axis=-1, keepdims=True)
        yc = r - mu
        yn = yc * lax.rsqrt(jnp.mean(yc * yc, axis=-1, keepdims=True) + EPS)
        o_ref[0, :, lanes] = ((yn * ng_ref[:, lanes]) * jax.nn.silu(rg_ref[0, :, lanes])).astype(BF16)


def _ret_log_gamma():
    return jnp.log1p(-jnp.exp2(-5.0 - jnp.arange(N_RET_HEADS, dtype=F32)))


def _retention(z, pos, state0, norm_g, *, chunk):
    b, t, _ = z.shape
    log_g = _ret_log_gamma()
    idx = jnp.arange(chunk, dtype=F32)
    diff = idx[:, None] - idx[None, :]
    intra = jnp.where(diff[None] >= 0.0,
                      jnp.exp(log_g[:, None, None] * jnp.maximum(diff, 0.0)[None]), 0.0)
    ones = jnp.ones((1, 1, RET_DV), F32)
    q_decay = jnp.exp(log_g[:, None] * (idx[None, :] + 1.0))[:, :, None] * ones
    k_decay = jnp.exp(log_g[:, None] * (chunk - 1.0 - idx)[None, :])[:, :, None] * ones
    c_decay = jnp.exp(log_g * chunk)[:, None, None] * jnp.ones((1, SUBLANES, RET_DV), F32)
    half = RET_DK // 2
    freq = 1.0 / (ROPE_BASE ** (jnp.arange(half, dtype=F32) / half))
    ang = pos.astype(F32)[:, None] * freq[None, :]
    cos, sin = jnp.cos(ang), jnp.sin(ang)
    cs = jnp.concatenate([cos, cos], axis=-1)
    sn = jnp.concatenate([-sin, sin], axis=-1)
    nhb = N_RET_HEADS // RET_HEADS_PER_BLK
    hb = RET_HEADS_PER_BLK

    def zspec(blk):
        return pl.BlockSpec((1, chunk, COL_BLK), lambda bb, g, c: (bb, c, blk + g))

    def tab(rows):
        return pl.BlockSpec((hb, rows, RET_DV), lambda bb, g, c: (g, 0, 0))

    out, st = pl.pallas_call(
        _retention_kernel,
        out_shape=(jax.ShapeDtypeStruct((b, t, N_RET_HEADS * RET_DV), BF16),
                   jax.ShapeDtypeStruct((b, N_RET_HEADS, RET_DK, RET_DV), F32)),
        grid=(b, nhb, t // chunk),
        in_specs=[
            zspec(RQ_BLK), zspec(RK_BLK), zspec(RV_BLK), zspec(RG_BLK),
            pl.BlockSpec((chunk, RET_DK), lambda bb, g, c: (c, 0)),
            pl.BlockSpec((chunk, RET_DK), lambda bb, g, c: (c, 0)),
            pl.BlockSpec((hb, chunk, chunk), lambda bb, g, c: (g, 0, 0)),
            tab(chunk), tab(chunk), tab(SUBLANES),
            pl.BlockSpec((1, COL_BLK), lambda bb, g, c: (0, g)),
            pl.BlockSpec((1, hb, RET_DK, RET_DV), lambda bb, g, c: (bb, g, 0, 0)),
        ],
        out_specs=(
            pl.BlockSpec((1, chunk, COL_BLK), lambda bb, g, c: (bb, c, g)),
            pl.BlockSpec((1, hb, RET_DK, RET_DV), lambda bb, g, c: (bb, g, 0, 0)),
        ),
        compiler_params=_params("arbitrary", "arbitrary", "arbitrary"),
        name="retention",
    )(z, z, z, z, cs, sn, intra, q_decay, k_decay, c_decay, norm_g.reshape(1, -1), state0)
    return out, st


def _gate_kernel(u_ref, v_ref, lg_ref, lb_ref, ws_ref, bs_ref, y_ref, *maybe_vn_ref, rows, span):
    ri = lax.broadcasted_iota(jnp.int32, (span, span), 0)
    ci = lax.broadcasted_iota(jnp.int32, (span, span), 1)
    wt = [jnp.where(ri >= ci, ws_ref[g], 0.0).astype(BF16) for g in range(GM_GROUPS)]
    bs = bs_ref[...]
    lg = lg_ref[...]
    lb = lb_ref[...]
    for c in range(rows // span):
        rs = slice(c * span, (c + 1) * span)
        v = v_ref[0, rs, :]
        mu = jnp.mean(v, axis=-1, keepdims=True)
        xc = v - mu
        vn = (xc * lax.rsqrt(jnp.mean(xc * xc, axis=-1, keepdims=True) + EPS)) * lg + lb
        for vn_ref in maybe_vn_ref:
            vn_ref[0, rs, :] = vn
        vb = vn.astype(BF16)
        for g in range(GM_GROUPS):
            cols = slice(g * GM_GROUP_DIM, (g + 1) * GM_GROUP_DIM)
            mixed = jnp.dot(wt[g], vb[:, cols], preferred_element_type=F32) + bs[:, g:g + 1]
            y_ref[0, rs, cols] = (u_ref[0, rs, cols] * mixed).astype(BF16)


def _spatial_gate(uv, ln_g, ln_b, ws, bs, *, rows, emit_vn):
    b, t, _ = uv.shape
    span = min(t, GM_CHUNK)
    ws_l = ws[:, :span, :span]
    bs_t = jnp.transpose(bs[:, :span])
    tile = pl.BlockSpec((1, rows, D_MODEL), lambda bb, i: (bb, i, 0))
    outs = pl.pallas_call(
        functools.partial(_gate_kernel, rows=rows, span=span),
        out_shape=(jax.ShapeDtypeStruct((b, t, D_MODEL), BF16),)
        + ((jax.ShapeDtypeStruct((b, t, D_MODEL), F32),) if emit_vn else ()),
        grid=(b, t // rows),
        in_specs=[
            pl.BlockSpec((1, rows, D_MODEL), lambda bb, i: (bb, i, 0)),
            pl.BlockSpec((1, rows, D_MODEL), lambda bb, i: (bb, i, 1)),
            pl.BlockSpec((1, D_MODEL), lambda bb, i: (0, 0)),
            pl.BlockSpec((1, D_MODEL), lambda bb, i: (0, 0)),
            pl.BlockSpec((GM_GROUPS, span, span), lambda bb, i: (0, 0, 0)),
            pl.BlockSpec((span, GM_GROUPS), lambda bb, i: (0, 0)),
        ],
        out_specs=(tile, tile) if emit_vn else (tile,),
        compiler_params=_params("arbitrary", "arbitrary"),
        name="spatial_gate",
    )(uv, uv, ln_g.reshape(1, -1), ln_b.reshape(1, -1), ws_l, bs_t)
    return outs if emit_vn else (outs[0], None)


ROW_TILE = 1024
FFN_ROW_TILE = 2048
DOWN_ROW_TILE = 256
OUT_ROW_TILE = 512
OUT_K_BLK = 1024
IN_EVEN_BLK = 1408
IN_ODD_BLK = 1024
FFN_BLK = 512
ATTN_ROW_TILE = 512


def _conv_ffn(hp, xp, hs, xs, g3, g_next, w_up, layer, conv_w, conv_b, w_down, state_p, state_s):
    nseq_p, nseq_s = state_p.shape[0], state_s.shape[0]
    act_p, c_p, wdb = _ffn_up(xp, w_up, layer, conv_w, conv_b, state_p, w_down, groups=1,
                              rows_per_group=FFN_ROW_TILE, bn=FFN_BLK, name="ffn_up_p")
    act_s, c_s, _ = _ffn_up(xs, w_up, layer, conv_w, conv_b, state_s, None, groups=nseq_s,
                            rows_per_group=xs.shape[0] // nseq_s, bn=FFN_BLK, name="ffn_up_s")
    wdb = wdb.reshape(wdb.shape[1:])
    hp, xp = _down_norm_res(act_p, wdb, g3, hp, g_next, bm=DOWN_ROW_TILE, name="ffn_down_p")
    hs, xs = _down_norm_res(act_s, wdb, g3, hs, g_next, bm=DOWN_ROW_TILE, name="ffn_down_s")
    return hp, xp, c_p, hs, xs, c_s


def kernel(x_prompt, x_sample, cache_swa_k, cache_swa_v, state_ret, state_ffn_conv, norm_g, w_in_even,
           w_out_even, attn_sinks, ret_norm_g, w_in_odd, w_out_odd, gm_ln_g, gm_ln_b, gm_ws, gm_bs,
           ffn_w_up, ffn_conv_w, ffn_conv_b, ffn_w_down):
    nb, seq, d = x_prompt.shape
    db, dseq, _ = x_sample.shape
    depth = norm_g.shape[0]
    hp = x_prompt.reshape(nb * seq, d)
    hs = x_sample.reshape(db * dseq, d)
    bm_p, bm_s = ROW_TILE, db * dseq
    pos_p = jnp.arange(seq)
    pos_s = PAST_LEN + jnp.arange(dseq)
    kp_l, vp_l, rp_l, cp_l = [], [], [], []
    ks_l, vs_l, rs_l, cs_l, gv_l = [], [], [], [], []
    xp = _rmsnorm(hp, norm_g[0, 0], bm=ROW_TILE, name="norm_in_p")
    xs = _rmsnorm(hs, norm_g[0, 0], bm=bm_s, name="norm_in_s")
    for layer in range(depth):
        g = norm_g[layer]
        g_next = norm_g[layer + 1, 0] if layer + 1 < depth else None
        if layer % 2 == 0:
            e = layer // 2
            zp = _xw(xp, w_in_even, e, bm=bm_p, bn=IN_EVEN_BLK, gelu=False, name="in_even_p")
            zs = _xw(xs, w_in_even, e, bm=bm_s, bn=IN_EVEN_BLK, gelu=False, name="in_even_s")
            zp = zp.reshape(nb, seq, EVEN_IN)
            zs = zs.reshape(db, dseq, EVEN_IN)
            attn_p = _attn_prompt(zp, attn_sinks[e], tq=ATTN_ROW_TILE)
            attn_s = _attn_sample(zs, cache_swa_k[e], cache_swa_v[e], attn_sinks[e])
            ret_p, r_p = _retention(zp, pos_p, jnp.zeros((nb, N_RET_HEADS, RET_DK, RET_DV), F32),
                                    ret_norm_g[e], chunk=RET_CHUNK)
            ret_s, r_s = _retention(zs, pos_s, state_ret[e].astype(F32), ret_norm_g[e], chunk=dseq)
            mixed_p = [attn_p.reshape(nb * seq, -1), ret_p.reshape(nb * seq, -1)]
            mixed_s = [attn_s.reshape(db * dseq, -1), ret_s.reshape(db * dseq, -1)]
            hp, xp = _matmul_norm_res(mixed_p, w_out_even, e, g[1], hp, g[2], bm=OUT_ROW_TILE, bk=OUT_K_BLK,
                                      resident=True, name="out_even_p")
            hs, xs = _matmul_norm_res(mixed_s, w_out_even, e, g[1], hs, g[2], bm=bm_s, bk=OUT_K_BLK,
                                      resident=False, name="out_even_s")
            k_new = zs[:, :, 1024:1280].reshape(db, dseq, N_KV_HEADS, HEAD_DIM)
            v_new = zs[:, :, 1280:1536].reshape(db, dseq, N_KV_HEADS, HEAD_DIM)
            n_keep = cache_swa_k.shape[2]
            kp_l.append(zp[:, seq - WINDOW:, 1024:1280].reshape(nb, WINDOW, N_KV_HEADS, HEAD_DIM))
            vp_l.append(zp[:, seq - WINDOW:, 1280:1536].reshape(nb, WINDOW, N_KV_HEADS, HEAD_DIM))
            ks_l.append(jnp.concatenate([cache_swa_k[e], k_new], axis=1)[:, -n_keep:])
            vs_l.append(jnp.concatenate([cache_swa_v[e], v_new], axis=1)[:, -n_keep:])
            rp_l.append(r_p)
            rs_l.append(r_s.astype(state_ret.dtype))
        else:
            o = layer // 2
            uvp = _xw(xp, w_in_odd, o, bm=bm_p, bn=IN_ODD_BLK, gelu=True, name="in_odd_p")
            uvs = _xw(xs, w_in_odd, o, bm=bm_s, bn=IN_ODD_BLK, gelu=True, name="in_odd_s")
            yp, _ = _spatial_gate(uvp.reshape(nb, seq, -1), gm_ln_g[o], gm_ln_b[o], gm_ws[o], gm_bs[o], rows=512,
                                  emit_vn=False)
            ys, gv = _spatial_gate(uvs.reshape(db, dseq, -1), gm_ln_g[o], gm_ln_b[o], gm_ws[o], gm_bs[o],
                                   rows=dseq, emit_vn=True)
            hp, xp = _matmul_norm_res([yp.reshape(nb * seq, -1)], w_out_odd, o, g[1], hp, g[2], bm=OUT_ROW_TILE,
                                      bk=OUT_K_BLK, resident=True, name="out_odd_p")
            hs, xs = _matmul_norm_res([ys.reshape(db * dseq, -1)], w_out_odd, o, g[1], hs, g[2], bm=bm_s,
                                      bk=OUT_K_BLK, resident=False, name="out_odd_s")
            gv_l.append(gv)
        zero_state = jnp.zeros((nb, CONV_W - 1, 2 * D_FF), F32)
        hp, xp, c_p, hs, xs, c_s = _conv_ffn(hp, xp, hs, xs, g[3], g_next, ffn_w_up, layer, ffn_conv_w[layer],
                                             ffn_conv_b[layer], ffn_w_down, zero_state, state_ffn_conv[layer])
        cp_l.append(c_p)
        cs_l.append(c_s)
    return (hp.reshape(nb, seq, d), hs.reshape(db, dseq, d),
            jnp.stack(kp_l), jnp.stack(vp_l), jnp.stack(rp_l), jnp.stack(cp_l),
            jnp.stack(ks_l), jnp.stack(vs_l), jnp.stack(rs_l), jnp.stack(cs_l), jnp.stack(gv_l))
```

```python
import functools

import jax
import jax.numpy as jnp
from jax import lax
from jax.experimental import pallas as pl
from jax.experimental.pallas import tpu as pltpu

F32 = jnp.float32
BF16 = jnp.bfloat16

D_MODEL = 2048
CHUNK = 64
HEAD_DIM = 64
N_Q_HEADS = 16
N_KV_HEADS = 4
Q_PER_KV = N_Q_HEADS // N_KV_HEADS
WINDOW = 128
N_RET_HEADS = 8
RET_DK = 128
RET_DV = 128
ROPE_BASE = 10000.0
GM_CHUNK = 128
GM_GROUPS = 8
GM_GROUP_DIM = D_MODEL // GM_GROUPS
D_FF = 5632
CONV_W = 3
EPS = 1e-6
PAST_LEN = 1024
EVEN_IN = 5632

SUBLANES = 8
VMEM_LIMIT_BYTES = 56 * 1024 * 1024

COL_BLK = 512
KV_BLK = 2
RQ_BLK, RK_BLK, RV_BLK, RG_BLK = 3, 5, 7, 9
RET_HEADS_PER_BLK = COL_BLK // RET_DK
RET_CHUNK = 256
SUB_COLS = 256
EPI_ROWS = 16
NORM_ROWS = 64
SAMPLE_SEQS_PER_STEP = 4


def _params(*sem):
    return pltpu.CompilerParams(dimension_semantics=sem, vmem_limit_bytes=VMEM_LIMIT_BYTES)


def _rmsnorm_rows(x_ref, g_ref, out_ref, rows, chunk):
    g = g_ref[...]

    def body(c, carry):
        r0 = pl.multiple_of(c * chunk, chunk)
        x = x_ref[pl.ds(r0, chunk), :]
        ms = jnp.mean(x * x, axis=-1, keepdims=True)
        out_ref[pl.ds(r0, chunk), :] = ((x * lax.rsqrt(ms + EPS)) * g).astype(out_ref.dtype)
        return carry

    lax.fori_loop(0, rows // chunk, body, 0)


def _zero_after(x):
    z = pltpu.bitcast(x, jnp.uint32)
    z = lax.shift_right_logical(lax.shift_right_logical(z, jnp.uint32(16)), jnp.uint32(16))
    return pltpu.bitcast(z, F32)


def _rmsnorm_kernel(x_ref, g_ref, o_ref, *, bm):
    _rmsnorm_rows(x_ref, g_ref, o_ref, bm, min(bm, 128))


def _rmsnorm(x, g, *, bm, name):
    m, k = x.shape
    return pl.pallas_call(
        functools.partial(_rmsnorm_kernel, bm=bm),
        out_shape=jax.ShapeDtypeStruct((m, k), BF16),
        grid=(m // bm,),
        in_specs=[pl.BlockSpec((bm, k), lambda i: (i, 0)), pl.BlockSpec((1, k), lambda i: (0, 0))],
        out_specs=pl.BlockSpec((bm, k), lambda i: (i, 0)),
        compiler_params=_params("arbitrary"),
        name=name,
    )(x, g.reshape(1, k))


def _xw_kernel(xn_ref, w_ref, o_ref, wb_ref, *, bm, rc, gelu):
    @pl.when(pl.program_id(1) == 0)
    def _():
        wb_ref[...] = w_ref[...].astype(BF16)

    for c in range(bm // rc):
        y = jnp.dot(xn_ref[c * rc:(c + 1) * rc, :], wb_ref[...], preferred_element_type=F32)
        if gelu:
            y = jax.nn.gelu(y, approximate=True)
        o_ref[c * rc:(c + 1) * rc, :] = y.astype(o_ref.dtype)


def _xw(xn, w, layer, *, bm, bn, gelu, name):
    m, k = xn.shape
    n = w.shape[2]
    rc = min(bm, 256)
    return pl.pallas_call(
        functools.partial(_xw_kernel, bm=bm, rc=rc, gelu=gelu),
        out_shape=jax.ShapeDtypeStruct((m, n), F32),
        grid=(n // bn, m // bm),
        in_specs=[
            pl.BlockSpec((bm, k), lambda j, i: (i, 0)),
            pl.BlockSpec((None, k, bn), lambda j, i: (layer, 0, j)),
        ],
        out_specs=pl.BlockSpec((bm, bn), lambda j, i: (i, j)),
        scratch_shapes=[pltpu.VMEM((k, bn), BF16)],
        compiler_params=_params("arbitrary", "arbitrary"),
        name=name,
    )(xn, w)


def _matmul_norm_res_kernel(*refs, splits, nk, bm, rc, emit_next, weights):
    na = len(splits)
    a_refs = refs[:na]
    rest = list(refs[na:])
    wb_ref = rest.pop() if weights != "bf16" else None
    if emit_next:
        w_ref, g_ref, r_ref, gn_ref, o_ref, xn_ref = rest
    else:
        w_ref, g_ref, r_ref, o_ref = rest
    i = pl.program_id(0)
    k = pl.program_id(1)
    if weights == "stream":
        wb_ref[...] = w_ref[...].astype(BF16)
        w_blk = wb_ref
    elif weights == "resident":
        @pl.when(i == 0)
        def _():
            wb_ref[k] = w_ref[...].astype(BF16)

        w_blk = wb_ref.at[k]
    else:
        w_blk = w_ref

    def finish(r0, y):
        rows = slice(r0, r0 + NORM_ROWS)
        ms = jnp.mean(y * y, axis=-1, keepdims=True)
        h = r_ref[rows, :] + (y * lax.rsqrt(ms + EPS)) * g_ref[...]
        o_ref[rows, :] = h
        if emit_next:
            ms2 = jnp.mean(h * h, axis=-1, keepdims=True)
            xn_ref[rows, :] = ((h * lax.rsqrt(ms2 + EPS)) * gn_ref[...]).astype(BF16)

    def step(a_ref, first, last):
        for c in range(bm // rc):
            rows = slice(c * rc, (c + 1) * rc)
            y = jnp.dot(a_ref[rows, :], w_blk[...], preferred_element_type=F32)
            if not first:
                y = o_ref[rows, :] + y
            if last:
                for r in range(0, rc, NORM_ROWS):
                    finish(c * rc + r, y[r:r + NORM_ROWS])
            else:
                o_ref[rows, :] = y

    for a_ref, (k0, k1) in zip(a_refs, splits):
        for first, last in sorted({(kk == 0, kk == nk - 1) for kk in range(k0, k1)}):
            ks = [kk for kk in range(k0, k1) if (kk == 0, kk == nk - 1) == (first, last)]
            pl.when((k >= ks[0]) & (k <= ks[-1]))(functools.partial(step, a_ref, first, last))


def _matmul_norm_res(a_list, w, layer, g, resid, g_next, *, bm, bk, resident, name):
    m = a_list[0].shape[0]
    n = w.shape[2]
    splits, k0 = [], 0
    for a in a_list:
        splits.append((k0, k0 + a.shape[1] // bk))
        k0 = splits[-1][1]
    nk = k0
    rc = min(bm, 256)
    emit_next = g_next is not None
    weights = "bf16" if w.dtype == BF16 else ("resident" if resident else "stream")

    def a_spec(k0, k1):
        return pl.BlockSpec((bm, bk), lambda i, k: (i, jnp.clip(k - k0, 0, k1 - k0 - 1)))

    if weights == "resident":
        w_spec = pl.BlockSpec((None, bk, n), lambda i, k: (layer, jnp.where(i == 0, k, nk - 1), 0))
        scratch = [pltpu.VMEM((nk, bk, n), BF16)]
    else:
        w_spec = pl.BlockSpec((None, bk, n), lambda i, k: (layer, k, 0))
        scratch = [pltpu.VMEM((bk, n), BF16)] if weights == "stream" else []
    row_vec = pl.BlockSpec((1, n), lambda i, k: (0, 0))
    tile = pl.BlockSpec((bm, n), lambda i, k: (i, 0))
    out = pl.pallas_call(
        functools.partial(_matmul_norm_res_kernel, splits=tuple(splits), nk=nk, bm=bm, rc=rc,
                          emit_next=emit_next, weights=weights),
        out_shape=((jax.ShapeDtypeStruct((m, n), F32), jax.ShapeDtypeStruct((m, n), BF16)) if emit_next
                   else jax.ShapeDtypeStruct((m, n), F32)),
        grid=(m // bm, nk),
        in_specs=[a_spec(*sp) for sp in splits] + [w_spec, row_vec, tile] + ([row_vec] if emit_next else []),
        out_specs=(tile, tile) if emit_next else tile,
        scratch_shapes=scratch,
        compiler_params=_params("arbitrary", "arbitrary"),
        name=name,
    )(*a_list, w, g.reshape(1, n), resid, *([g_next.reshape(1, n)] if emit_next else []))
    return out if emit_next else (out, None)


def _down_kernel(*refs, bm, emit_next):
    if emit_next:
        a_ref, w_hbm_ref, g_ref, r_ref, gn_ref, o_ref, xn_ref, w_ref, sem = refs
    else:
        a_ref, w_hbm_ref, g_ref, r_ref, o_ref, w_ref, sem = refs

    @pl.when(pl.program_id(0) == 0)
    def _():
        copy = pltpu.make_async_copy(w_hbm_ref, w_ref, sem)
        copy.start()
        copy.wait()

    y = jnp.dot(a_ref[...], w_ref[...], preferred_element_type=F32)
    for r in range(0, bm, NORM_ROWS):
        rows = slice(r, r + NORM_ROWS)
        yy = y[rows]
        ms = jnp.mean(yy * yy, axis=-1, keepdims=True)
        h = r_ref[rows, :] + (yy * lax.rsqrt(ms + EPS)) * g_ref[...]
        o_ref[rows, :] = h
        if emit_next:
            ms2 = jnp.mean(h * h, axis=-1, keepdims=True)
            xn_ref[rows, :] = ((h * lax.rsqrt(ms2 + EPS)) * gn_ref[...]).astype(BF16)


def _down_norm_res(a, w, g, resid, g_next, *, bm, name):
    m, kdim = a.shape
    n = w.shape[1]
    emit_next = g_next is not None
    row_vec = pl.BlockSpec((1, n), lambda i: (0, 0))
    tile = pl.BlockSpec((bm, n), lambda i: (i, 0))
    out = pl.pallas_call(
        functools.partial(_down_kernel, bm=bm, emit_next=emit_next),
        out_shape=((jax.ShapeDtypeStruct((m, n), F32), jax.ShapeDtypeStruct((m, n), BF16)) if emit_next
                   else jax.ShapeDtypeStruct((m, n), F32)),
        grid=(m // bm,),
        in_specs=[pl.BlockSpec((bm, kdim), lambda i: (i, 0)), pl.BlockSpec(memory_space=pl.ANY), row_vec, tile]
        + ([row_vec] if emit_next else []),
        out_specs=(tile, tile) if emit_next else tile,
        scratch_shapes=[pltpu.VMEM((kdim, n), BF16), pltpu.SemaphoreType.DMA(())],
        compiler_params=_params("arbitrary"),
        name=name,
    )(a, w, g.reshape(1, n), resid, *([g_next.reshape(1, n)] if emit_next else []))
    return out if emit_next else (out, None)


def _ffn_up_kernel(*refs, groups, rows_per_group, rc, tiles_per_seq, cast_down):
    if cast_down:
        (xn_ref, wa_ref, wg_ref, cwa_ref, cwg_ref, sa_ref, sg_ref, wd_ref,
         act_ref, la_ref, lg_ref, wdb_ref, wb_ref, carry_ref) = refs
    else:
        (xn_ref, wa_ref, wg_ref, cwa_ref, cwg_ref, sa_ref, sg_ref,
         act_ref, la_ref, lg_ref, wb_ref, carry_ref) = refs
    i = pl.program_id(1)
    bm = groups * rows_per_group
    bn = act_ref.shape[1]
    nsub = bn // SUB_COLS
    subs = [slice(u * SUB_COLS, (u + 1) * SUB_COLS) for u in range(nsub)]

    @pl.when(i == 0)
    def _():
        for u in range(nsub):
            wb_ref[u, :, :SUB_COLS] = wa_ref[:, subs[u]].astype(BF16)
            wb_ref[u, :, SUB_COLS:] = wg_ref[:, subs[u]].astype(BF16)
        if cast_down:
            wdb_ref[...] = wd_ref[...].astype(BF16)

    keep = SUBLANES - (CONV_W - 1)

    @pl.when(lax.rem(i, tiles_per_seq) == 0)
    def _():
        carry_ref[...] = jnp.zeros(carry_ref.shape, F32)
        carry_ref[0, :, keep:, :] = sa_ref[...]
        carry_ref[1, :, keep:, :] = sg_ref[...]

    cws = [jnp.concatenate([cwa_ref[:, subs[u]], cwg_ref[:, subs[u]]], axis=1) for u in range(nsub)]
    piece = min(EPI_ROWS, rows_per_group)
    prev = [None] * nsub
    after = [None] * nsub
    for c in range(bm // rc):
        xc = xn_ref[c * rc:(c + 1) * rc, :]
        for u in range(nsub):
            h = jnp.dot(xc, wb_ref[u], preferred_element_type=F32)
            for q in range(rc // piece):
                row = c * rc + q * piece
                grp = row // rows_per_group
                hcur = h[q * piece:(q + 1) * piece]
                if row % rows_per_group == 0:
                    prev8 = jnp.concatenate([carry_ref[0, grp, :, subs[u]], carry_ref[1, grp, :, subs[u]]], axis=1)
                else:
                    prev8 = prev[u]
                hext = jnp.concatenate([prev8, hcur], axis=0)
                s1 = pltpu.roll(hext, 1, 0)[SUBLANES:]
                s2 = pltpu.roll(hext, 2, 0)[SUBLANES:]
                cw = cws[u] if after[u] is None else cws[u] + jnp.concatenate([after[u]] * 2, axis=1)
                conv = cw[3:4] + s2 * cw[0:1] + s1 * cw[1:2] + hcur * cw[2:3]
                act = jax.nn.gelu(conv[:, SUB_COLS:], approximate=True) * conv[:, :SUB_COLS]
                act_ref[row:row + piece, subs[u]] = act.astype(BF16)
                after[u] = _zero_after(act[piece - SUBLANES:])
                prev[u] = hcur[piece - SUBLANES:]
                if (row + piece) % rows_per_group == 0:
                    la_ref[grp, :, subs[u]] = prev[u][keep:, :SUB_COLS]
                    lg_ref[grp, :, subs[u]] = prev[u][keep:, SUB_COLS:]
                    carry_ref[0, grp, :, subs[u]] = prev[u][:, :SUB_COLS]
                    carry_ref[1, grp, :, subs[u]] = prev[u][:, SUB_COLS:]


def _ffn_up(xn, w_up, layer, conv_w, conv_b, state, w_down, *, groups, rows_per_group, bn, name):
    m, k = xn.shape
    nseq = state.shape[0]
    bm = groups * rows_per_group
    ni = m // bm
    tiles_per_seq = ni * groups // nseq
    nj = D_FF // bn
    rc = min(bm, 256)
    cw = jnp.concatenate([conv_w, conv_b[None], jnp.zeros((SUBLANES - CONV_W - 1, 2 * D_FF), F32)], axis=0)
    cast_down = w_down is not None
    d_out = w_down.shape[2] if cast_down else 0
    out_shape = [jax.ShapeDtypeStruct((m, D_FF), BF16),
                 jax.ShapeDtypeStruct((nseq, CONV_W - 1, D_FF), F32),
                 jax.ShapeDtypeStruct((nseq, CONV_W - 1, D_FF), F32)]
    in_specs = [
        pl.BlockSpec((bm, k), lambda j, i: (i, 0)),
        pl.BlockSpec((None, k, bn), lambda j, i: (layer, 0, j)),
        pl.BlockSpec((None, k, bn), lambda j, i: (layer, 0, nj + j)),
        pl.BlockSpec((SUBLANES, bn), lambda j, i: (0, j)),
        pl.BlockSpec((SUBLANES, bn), lambda j, i: (0, nj + j)),
        pl.BlockSpec((groups, CONV_W - 1, bn), lambda j, i: (i // tiles_per_seq, 0, j)),
        pl.BlockSpec((groups, CONV_W - 1, bn), lambda j, i: (i // tiles_per_seq, 0, nj + j)),
    ]
    out_specs = [
        pl.BlockSpec((bm, bn), lambda j, i: (i, j)),
        pl.BlockSpec((groups, CONV_W - 1, bn), lambda j, i: (i // tiles_per_seq, 0, j)),
        pl.BlockSpec((groups, CONV_W - 1, bn), lambda j, i: (i // tiles_per_seq, 0, j)),
    ]
    operands = [xn, w_up, w_up, cw, cw, state, state]
    if cast_down:
        in_specs.append(pl.BlockSpec((None, bn, d_out), lambda j, i: (layer, j, 0)))
        out_specs.append(pl.BlockSpec((None, bn, d_out), lambda j, i: (0, j, 0)))
        out_shape.append(jax.ShapeDtypeStruct((1, D_FF, d_out), BF16))
        operands.append(w_down)
    outs = pl.pallas_call(
        functools.partial(_ffn_up_kernel, groups=groups, rows_per_group=rows_per_group, rc=rc,
                          tiles_per_seq=tiles_per_seq, cast_down=cast_down),
        out_shape=tuple(out_shape),
        grid=(nj, ni),
        in_specs=in_specs,
        out_specs=tuple(out_specs),
        scratch_shapes=[pltpu.VMEM((bn // SUB_COLS, k, 2 * SUB_COLS), BF16),
                        pltpu.VMEM((2, groups, SUBLANES, bn), F32)],
        compiler_params=_params("arbitrary", "arbitrary"),
        name=name,
    )(*operands)
    act, la, lg = outs[:3]
    return act, jnp.concatenate([la, lg], axis=-1), (outs[3] if cast_down else None)


def _attn_head(q, k, v, sink_ref, h, bias):
    nq = q.shape[0]
    kh = k[:, h * HEAD_DIM:(h + 1) * HEAD_DIM].astype(BF16)
    vh = v[:, h * HEAD_DIM:(h + 1) * HEAD_DIM].astype(BF16)
    qs, sk = [], []
    for gq in range(Q_PER_KV):
        c0 = (h * Q_PER_KV + gq) * HEAD_DIM
        qs.append(q[:, c0:c0 + HEAD_DIM])
        sk.append(jnp.full((nq, 1), sink_ref[h * Q_PER_KV + gq], F32))
    qh = (jnp.concatenate(qs, axis=0) * (HEAD_DIM ** -0.5)).astype(BF16)
    sk = jnp.concatenate(sk, axis=0)
    s = lax.dot_general(qh, kh, (((1,), (1,)), ((), ())), preferred_element_type=F32)
    if bias is not None:
        s = s + bias
    mx = jnp.maximum(jnp.max(s, axis=-1, keepdims=True), sk)
    p = jnp.exp(s - mx)
    den = jnp.sum(p, axis=-1, keepdims=True) + jnp.exp(sk - mx)
    o = jnp.dot((p / den).astype(BF16), vh, preferred_element_type=F32)
    return jnp.concatenate([o[gq * nq:(gq + 1) * nq] for gq in range(Q_PER_KV)], axis=1)


ATTN_UNIT = 2 * CHUNK


def _attn_prompt_kernel(sink_ref, q_ref, kv_ref, kvp_ref, o_ref, bias_ref, *, tq):
    i = pl.program_id(1)
    nk = ATTN_UNIT + WINDOW
    cols = Q_PER_KV * ATTN_UNIT
    kvw = N_KV_HEADS * HEAD_DIM
    @pl.when(i == 0)
    def _():
        r_k = lax.broadcasted_iota(jnp.int32, (nk, cols), 0)
        r_q = lax.broadcasted_iota(jnp.int32, (nk, cols), 1)
        lo = (r_q & (ATTN_UNIT - 1)) & ~(CHUNK - 1)
        band = (r_k >= lo) & (r_k < lo + WINDOW + CHUNK)
        bias_ref[1] = jnp.where(band, 0.0, -jnp.inf)
        bias_ref[0] = jnp.where(band & (r_k >= WINDOW), 0.0, -jnp.inf)

    first_unit_bias = jnp.where(i == 0, 0, 1)

    kv_all = jnp.concatenate([kvp_ref[0], kv_ref[0]], axis=0)
    k_all = kv_all[:, :kvw].astype(BF16)
    vt_all = jnp.transpose(kv_all[:, kvw:]).astype(BF16)
    for u in range(tq // ATTN_UNIT):
        rs = slice(u * ATTN_UNIT, (u + 1) * ATTN_UNIT)
        keys = slice(u * ATTN_UNIT, u * ATTN_UNIT + nk)
        k_u = k_all[keys]
        qt = jnp.transpose(q_ref[0, rs, :] * (HEAD_DIM ** -0.5)).astype(BF16)
        bias = bias_ref[first_unit_bias] if u == 0 else bias_ref[1]
        pieces = []
        for h in range(N_KV_HEADS):
            heads = [h * Q_PER_KV + gq for gq in range(Q_PER_KV)]
            qt_h = jnp.concatenate([qt[n * HEAD_DIM:(n + 1) * HEAD_DIM] for n in heads], axis=1)
            parts = [qt_h]
            if h > 0:
                parts.insert(0, jnp.zeros((h * HEAD_DIM, cols), BF16))
            if h < N_KV_HEADS - 1:
                parts.append(jnp.zeros(((N_KV_HEADS - 1 - h) * HEAD_DIM, cols), BF16))
            st = jnp.dot(k_u, jnp.concatenate(parts, axis=0), preferred_element_type=F32) + bias
            sk = jnp.concatenate([jnp.full((1, ATTN_UNIT), sink_ref[n], F32) for n in heads], axis=1)
            mx = jnp.maximum(jnp.max(st, axis=0, keepdims=True), sk)
            p = jnp.exp(st - mx)
            den = jnp.sum(p, axis=0, keepdims=True) + jnp.exp(sk - mx)
            ot = jnp.dot(vt_all[h * HEAD_DIM:(h + 1) * HEAD_DIM, keys], (p / den).astype(BF16),
                         preferred_element_type=F32)
            pieces += [ot[:, gq * ATTN_UNIT:(gq + 1) * ATTN_UNIT] for gq in range(Q_PER_KV)]
        o_ref[0, rs, :] = jnp.transpose(jnp.concatenate(pieces, axis=0)).astype(BF16)


def _attn_prompt(z, sinks, *, tq):
    b, t, _ = z.shape
    per = tq // WINDOW
    return pl.pallas_call(
        functools.partial(_attn_prompt_kernel, tq=tq),
        out_shape=jax.ShapeDtypeStruct((b, t, N_Q_HEADS * HEAD_DIM), BF16),
        grid=(b, t // tq),
        in_specs=[
            pl.BlockSpec(memory_space=pltpu.SMEM),
            pl.BlockSpec((1, tq, 1024), lambda bb, i: (bb, i, 0)),
            pl.BlockSpec((1, tq, COL_BLK), lambda bb, i: (bb, i, KV_BLK)),
            pl.BlockSpec((1, WINDOW, COL_BLK), lambda bb, i: (bb, jnp.maximum(i * per - 1, 0), KV_BLK)),
        ],
        out_specs=pl.BlockSpec((1, tq, 1024), lambda bb, i: (bb, i, 0)),
        scratch_shapes=[pltpu.VMEM((2, ATTN_UNIT + WINDOW, Q_PER_KV * ATTN_UNIT), F32)],
        compiler_params=_params("arbitrary", "arbitrary"),
        name="attn_prompt",
    )(sinks, z, z, z)


def _attn_sample_kernel(sink_ref, q_ref, kv_ref, ck_ref, cv_ref, o_ref):
    for s in range(q_ref.shape[0]):
        kv = kv_ref[s]
        k = jnp.concatenate([ck_ref[s], kv[:, :256]], axis=0)
        v = jnp.concatenate([cv_ref[s], kv[:, 256:]], axis=0)
        q = q_ref[s]
        for h in range(N_KV_HEADS):
            o_ref[s, :, h * 256:(h + 1) * 256] = _attn_head(q, k, v, sink_ref, h, None).astype(BF16)


def _attn_sample(z, cache_k, cache_v, sinks):
    b, t, _ = z.shape
    nc = cache_k.shape[1]
    nb = SAMPLE_SEQS_PER_STEP
    return pl.pallas_call(
        _attn_sample_kernel,
        out_shape=jax.ShapeDtypeStruct((b, t, N_Q_HEADS * HEAD_DIM), BF16),
        grid=(b // nb,),
        in_specs=[
            pl.BlockSpec(memory_space=pltpu.SMEM),
            pl.BlockSpec((nb, t, 1024), lambda bb: (bb, 0, 0)),
            pl.BlockSpec((nb, t, COL_BLK), lambda bb: (bb, 0, KV_BLK)),
            pl.BlockSpec((nb, nc, 256), lambda bb: (bb, 0, 0)),
            pl.BlockSpec((nb, nc, 256), lambda bb: (bb, 0, 0)),
        ],
        out_specs=pl.BlockSpec((nb, t, 1024), lambda bb: (bb, 0, 0)),
        compiler_params=_params("arbitrary"),
        name="attn_sample",
    )(sinks, z, z, cache_k.reshape(b, nc, 256), cache_v.reshape(b, nc, 256))


def _retention_kernel(rq_ref, rk_ref, rv_ref, rg_ref, cs_ref, sn_ref, intra_ref, qd_ref, kd_ref, cd_ref,
                      ng_ref, s0_ref, o_ref, st_ref):
    c = pl.program_id(2)

    @pl.when(c == 0)
    def _():
        st_ref[...] = s0_ref[...]

    cs = cs_ref[...]
    sn = sn_ref[...]
    for s, hh in [(s, hh) for s in range(rq_ref.shape[0]) for hh in range(RET_HEADS_PER_BLK)]:
        lanes = slice(hh * RET_DK, (hh + 1) * RET_DK)
        q = rq_ref[s, :, lanes]
        k = rk_ref[s, :, lanes]
        qr = q * cs + pltpu.roll(q, RET_DK // 2, 1) * sn
        kr = (k * cs + pltpu.roll(k, RET_DK // 2, 1) * sn) * (RET_DK ** -0.5)
        qb = qr.astype(BF16)
        kb = kr.astype(BF16)
        vb = rv_ref[s, :, lanes].astype(BF16)
        sc = lax.dot_general(qb, kb, (((1,), (1,)), ((), ())), preferred_element_type=F32) * intra_ref[hh]
        inner = jnp.dot(sc.astype(BF16), vb, preferred_element_type=F32)
        state = st_ref[s, hh]
        cross = jnp.dot(qb, state.astype(BF16), preferred_element_type=F32) * qd_ref[hh]
        kdt = jnp.transpose(kr * kd_ref[hh]).astype(BF16)
        st_ref[s, hh] = cd_ref[hh, 0:1, :] * state + jnp.dot(kdt, vb, preferred_element_type=F32)
        r = inner + cross
        mu = jnp.mean(r, axis=-1, keepdims=True)
        yc = r - mu
        yn = yc * lax.rsqrt(jnp.mean(yc * yc, axis=-1, keepdims=True) + EPS)
        o_ref[s, :, lanes] = ((yn * ng_ref[:, lanes]) * jax.nn.silu(rg_ref[s, :, lanes])).astype(BF16)


def _ret_log_gamma():
    return jnp.log1p(-jnp.exp2(-5.0 - jnp.arange(N_RET_HEADS, dtype=F32)))


def _retention(z, pos, state0, norm_g, *, chunk, nbatch):
    b, t, _ = z.shape
    log_g = _ret_log_gamma()
    idx = jnp.arange(chunk, dtype=F32)
    diff = idx[:, None] - idx[None, :]
    intra = jnp.where(diff[None] >= 0.0,
                      jnp.exp(log_g[:, None, None] * jnp.maximum(diff, 0.0)[None]), 0.0)
    ones = jnp.ones((1, 1, RET_DV), F32)
    q_decay = jnp.exp(log_g[:, None] * (idx[None, :] + 1.0))[:, :, None] * ones
    k_decay = jnp.exp(log_g[:, None] * (chunk - 1.0 - idx)[None, :])[:, :, None] * ones
    c_decay = jnp.exp(log_g * chunk)[:, None, None] * jnp.ones((1, SUBLANES, RET_DV), F32)
    half = RET_DK // 2
    freq = 1.0 / (ROPE_BASE ** (jnp.arange(half, dtype=F32) / half))
    ang = pos.astype(F32)[:, None] * freq[None, :]
    cos, sin = jnp.cos(ang), jnp.sin(ang)
    cs = jnp.concatenate([cos, cos], axis=-1)
    sn = jnp.concatenate([-sin, sin], axis=-1)
    nhb = N_RET_HEADS // RET_HEADS_PER_BLK
    hb = RET_HEADS_PER_BLK

    def zspec(blk):
        return pl.BlockSpec((nbatch, chunk, COL_BLK), lambda bb, g, c: (bb, c, blk + g))

    def tab(rows):
        return pl.BlockSpec((hb, rows, RET_DV), lambda bb, g, c: (g, 0, 0))

    state_spec = pl.BlockSpec((nbatch, hb, RET_DK, RET_DV), lambda bb, g, c: (bb, g, 0, 0))
    out, st = pl.pallas_call(
        _retention_kernel,
        out_shape=(jax.ShapeDtypeStruct((b, t, N_RET_HEADS * RET_DV), BF16),
                   jax.ShapeDtypeStruct((b, N_RET_HEADS, RET_DK, RET_DV), F32)),
        grid=(b // nbatch, nhb, t // chunk),
        in_specs=[
            zspec(RQ_BLK), zspec(RK_BLK), zspec(RV_BLK), zspec(RG_BLK),
            pl.BlockSpec((chunk, RET_DK), lambda bb, g, c: (c, 0)),
            pl.BlockSpec((chunk, RET_DK), lambda bb, g, c: (c, 0)),
            pl.BlockSpec((hb, chunk, chunk), lambda bb, g, c: (g, 0, 0)),
            tab(chunk), tab(chunk), tab(SUBLANES),
            pl.BlockSpec((1, COL_BLK), lambda bb, g, c: (0, g)),
            state_spec,
        ],
        out_specs=(pl.BlockSpec((nbatch, chunk, COL_BLK), lambda bb, g, c: (bb, c, g)), state_spec),
        compiler_params=_params("arbitrary", "arbitrary", "arbitrary"),
        name="retention",
    )(z, z, z, z, cs, sn, intra, q_decay, k_decay, c_decay, norm_g.reshape(1, -1), state0)
    return out, st


def _gate_kernel(u_ref, v_ref, lg_ref, lb_ref, ws_ref, bs_ref, y_ref, *maybe_vn_ref, rows, span):
    ri = lax.broadcasted_iota(jnp.int32, (span, span), 0)
    ci = lax.broadcasted_iota(jnp.int32, (span, span), 1)
    wt = [jnp.where(ri >= ci, ws_ref[g], 0.0).astype(BF16) for g in range(GM_GROUPS)]
    bs = bs_ref[...]
    lg = lg_ref[...]
    lb = lb_ref[...]
    for s, c in [(s, c) for s in range(u_ref.shape[0]) for c in range(rows // span)]:
        rs = slice(c * span, (c + 1) * span)
        v = v_ref[s, rs, :]
        mu = jnp.mean(v, axis=-1, keepdims=True)
        xc = v - mu
        vn = (xc * lax.rsqrt(jnp.mean(xc * xc, axis=-1, keepdims=True) + EPS)) * lg + lb
        for vn_ref in maybe_vn_ref:
            vn_ref[s, rs, :] = vn
        vb = vn.astype(BF16)
        for g in range(GM_GROUPS):
            cols = slice(g * GM_GROUP_DIM, (g + 1) * GM_GROUP_DIM)
            mixed = jnp.dot(wt[g], vb[:, cols], preferred_element_type=F32) + bs[:, g:g + 1]
            y_ref[s, rs, cols] = (u_ref[s, rs, cols] * mixed).astype(BF16)


def _spatial_gate(uv, ln_g, ln_b, ws, bs, *, rows, nbatch, emit_vn):
    b, t, _ = uv.shape
    span = min(t, GM_CHUNK)
    ws_l = ws[:, :span, :span]
    bs_t = jnp.transpose(bs[:, :span])
    tile = pl.BlockSpec((nbatch, rows, D_MODEL), lambda bb, i: (bb, i, 0))
    outs = pl.pallas_call(
        functools.partial(_gate_kernel, rows=rows, span=span),
        out_shape=(jax.ShapeDtypeStruct((b, t, D_MODEL), BF16),)
        + ((jax.ShapeDtypeStruct((b, t, D_MODEL), F32),) if emit_vn else ()),
        grid=(b // nbatch, t // rows),
        in_specs=[
            pl.BlockSpec((nbatch, rows, D_MODEL), lambda bb, i: (bb, i, 0)),
            pl.BlockSpec((nbatch, rows, D_MODEL), lambda bb, i: (bb, i, 1)),
            pl.BlockSpec((1, D_MODEL), lambda bb, i: (0, 0)),
            pl.BlockSpec((1, D_MODEL), lambda bb, i: (0, 0)),
            pl.BlockSpec((GM_GROUPS, span, span), lambda bb, i: (0, 0, 0)),
            pl.BlockSpec((span, GM_GROUPS), lambda bb, i: (0, 0)),
        ],
        out_specs=(tile, tile) if emit_vn else (tile,),
        compiler_params=_params("arbitrary", "arbitrary"),
        name="spatial_gate",
    )(uv, uv, ln_g.reshape(1, -1), ln_b.reshape(1, -1), ws_l, bs_t)
    return outs if emit_vn else (outs[0], None)


ROW_TILE = 1024
FFN_ROW_TILE = 2048
DOWN_ROW_TILE = 256
OUT_ROW_TILE = 512
OUT_K_BLK = 1024
IN_EVEN_BLK = 1408
IN_ODD_BLK = 1024
FFN_BLK = 512
ATTN_ROW_TILE = 512


def _conv_ffn(hp, xp, hs, xs, g3, g_next, w_up, layer, conv_w, conv_b, w_down, state_p, state_s):
    nseq_p, nseq_s = state_p.shape[0], state_s.shape[0]
    act_p, c_p, wdb = _ffn_up(xp, w_up, layer, conv_w, conv_b, state_p, w_down, groups=1,
                              rows_per_group=FFN_ROW_TILE, bn=FFN_BLK, name="ffn_up_p")
    act_s, c_s, _ = _ffn_up(xs, w_up, layer, conv_w, conv_b, state_s, None, groups=nseq_s,
                            rows_per_group=xs.shape[0] // nseq_s, bn=FFN_BLK, name="ffn_up_s")
    wdb = wdb.reshape(wdb.shape[1:])
    hp, xp = _down_norm_res(act_p, wdb, g3, hp, g_next, bm=DOWN_ROW_TILE, name="ffn_down_p")
    hs, xs = _down_norm_res(act_s, wdb, g3, hs, g_next, bm=DOWN_ROW_TILE, name="ffn_down_s")
    return hp, xp, c_p, hs, xs, c_s


def kernel(x_prompt, x_sample, cache_swa_k, cache_swa_v, state_ret, state_ffn_conv, norm_g, w_in_even,
           w_out_even, attn_sinks, ret_norm_g, w_in_odd, w_out_odd, gm_ln_g, gm_ln_b, gm_ws, gm_bs,
           ffn_w_up, ffn_conv_w, ffn_conv_b, ffn_w_down):
    nb, seq, d = x_prompt.shape
    db, dseq, _ = x_sample.shape
    depth = norm_g.shape[0]
    hp = x_prompt.reshape(nb * seq, d)
    hs = x_sample.reshape(db * dseq, d)
    bm_p, bm_s = ROW_TILE, db * dseq
    pos_p = jnp.arange(seq)
    pos_s = PAST_LEN + jnp.arange(dseq)
    kp_l, vp_l, rp_l, cp_l = [], [], [], []
    ks_l, vs_l, rs_l, cs_l, gv_l = [], [], [], [], []
    xp = _rmsnorm(hp, norm_g[0, 0], bm=ROW_TILE, name="norm_in_p")
    xs = _rmsnorm(hs, norm_g[0, 0], bm=bm_s, name="norm_in_s")
    for layer in range(depth):
        g = norm_g[layer]
        g_next = norm_g[layer + 1, 0] if layer + 1 < depth else None
        if layer % 2 == 0:
            e = layer // 2
            zp = _xw(xp, w_in_even, e, bm=bm_p, bn=IN_EVEN_BLK, gelu=False, name="in_even_p")
            zs = _xw(xs, w_in_even, e, bm=bm_s, bn=IN_EVEN_BLK, gelu=False, name="in_even_s")
            zp = zp.reshape(nb, seq, EVEN_IN)
            zs = zs.reshape(db, dseq, EVEN_IN)
            attn_p = _attn_prompt(zp, attn_sinks[e], tq=ATTN_ROW_TILE)
            attn_s = _attn_sample(zs, cache_swa_k[e], cache_swa_v[e], attn_sinks[e])
            ret_p, r_p = _retention(zp, pos_p, jnp.zeros((nb, N_RET_HEADS, RET_DK, RET_DV), F32),
                                    ret_norm_g[e], chunk=RET_CHUNK, nbatch=1)
            ret_s, r_s = _retention(zs, pos_s, state_ret[e].astype(F32), ret_norm_g[e], chunk=dseq,
                                    nbatch=SAMPLE_SEQS_PER_STEP)
            mixed_p = [attn_p.reshape(nb * seq, -1), ret_p.reshape(nb * seq, -1)]
            mixed_s = [attn_s.reshape(db * dseq, -1), ret_s.reshape(db * dseq, -1)]
            hp, xp = _matmul_norm_res(mixed_p, w_out_even, e, g[1], hp, g[2], bm=OUT_ROW_TILE, bk=OUT_K_BLK,
                                      resident=True, name="out_even_p")
            hs, xs = _matmul_norm_res(mixed_s, w_out_even, e, g[1], hs, g[2], bm=bm_s, bk=OUT_K_BLK,
                                      resident=False, name="out_even_s")
            k_new = zs[:, :, 1024:1280].reshape(db, dseq, N_KV_HEADS, HEAD_DIM)
            v_new = zs[:, :, 1280:1536].reshape(db, dseq, N_KV_HEADS, HEAD_DIM)
            n_keep = cache_swa_k.shape[2]
            kp_l.append(zp[:, seq - WINDOW:, 1024:1280].reshape(nb, WINDOW, N_KV_HEADS, HEAD_DIM))
            vp_l.append(zp[:, seq - WINDOW:, 1280:1536].reshape(nb, WINDOW, N_KV_HEADS, HEAD_DIM))
            ks_l.append(jnp.concatenate([cache_swa_k[e], k_new], axis=1)[:, -n_keep:])
            vs_l.append(jnp.concatenate([cache_swa_v[e], v_new], axis=1)[:, -n_keep:])
            rp_l.append(r_p)
            rs_l.append(r_s.astype(state_ret.dtype))
        else:
            o = layer // 2
            uvp = _xw(xp, w_in_odd, o, bm=bm_p, bn=IN_ODD_BLK, gelu=True, name="in_odd_p")
            uvs = _xw(xs, w_in_odd, o, bm=bm_s, bn=IN_ODD_BLK, gelu=True, name="in_odd_s")
            yp, _ = _spatial_gate(uvp.reshape(nb, seq, -1), gm_ln_g[o], gm_ln_b[o], gm_ws[o], gm_bs[o], rows=512,
                                  nbatch=1, emit_vn=False)
            ys, gv = _spatial_gate(uvs.reshape(db, dseq, -1), gm_ln_g[o], gm_ln_b[o], gm_ws[o], gm_bs[o],
                                   rows=dseq, nbatch=SAMPLE_SEQS_PER_STEP, emit_vn=True)
            hp, xp = _matmul_norm_res([yp.reshape(nb * seq, -1)], w_out_odd, o, g[1], hp, g[2], bm=OUT_ROW_TILE,
                                      bk=OUT_K_BLK, resident=True, name="out_odd_p")
            hs, xs = _matmul_norm_res([ys.reshape(db * dseq, -1)], w_out_odd, o, g[1], hs, g[2], bm=bm_s,
                                      bk=OUT_K_BLK, resident=False, name="out_odd_s")
            gv_l.append(gv)
        zero_state = jnp.zeros((nb, CONV_W - 1, 2 * D_FF), F32)
        hp, xp, c_p, hs, xs, c_s = _conv_ffn(hp, xp, hs, xs, g[3], g_next, ffn_w_up, layer, ffn_conv_w[layer],
                                             ffn_conv_b[layer], ffn_w_down, zero_state, state_ffn_conv[layer])
        cp_l.append(c_p)
        cs_l.append(c_s)
    return (hp.reshape(nb, seq, d), hs.reshape(db, dseq, d),
            jnp.stack(kp_l), jnp.stack(vp_l), jnp.stack(rp_l), jnp.stack(cp_l),
            jnp.stack(ks_l), jnp.stack(vs_l), jnp.stack(rs_l), jnp.stack(cs_l), jnp.stack(gv_l))
```

```python
import functools

import jax
import jax.numpy as jnp
from jax import lax
from jax.experimental import pallas as pl
from jax.experimental.pallas import tpu as pltpu

F32 = jnp.float32
BF16 = jnp.bfloat16

D_MODEL = 2048
CHUNK = 64
HEAD_DIM = 64
N_Q_HEADS = 16
N_KV_HEADS = 4
Q_PER_KV = N_Q_HEADS // N_KV_HEADS
WINDOW = 128
N_RET_HEADS = 8
RET_DK = 128
RET_DV = 128
ROPE_BASE = 10000.0
GM_CHUNK = 128
GM_GROUPS = 8
GM_GROUP_DIM = D_MODEL // GM_GROUPS
D_FF = 5632
CONV_W = 3
EPS = 1e-6
PAST_LEN = 1024
EVEN_IN = 5632

SUBLANES = 8
VMEM_LIMIT_BYTES = 56 * 1024 * 1024

COL_BLK = 512
KV_BLK = 2
RQ_BLK, RK_BLK, RV_BLK, RG_BLK = 3, 5, 7, 9
RET_HEADS_PER_BLK = COL_BLK // RET_DK
RET_CHUNK = 256
SUB_COLS = 256
EPI_ROWS = 16
NORM_ROWS = 64
RET_SAMPLE_SEQS_PER_STEP = 4
GATE_SAMPLE_SEQS_PER_STEP = 8
ATTN_SAMPLE_SEQS_PER_STEP = 1


def _params(*sem):
    return pltpu.CompilerParams(dimension_semantics=sem, vmem_limit_bytes=VMEM_LIMIT_BYTES)


def _rmsnorm_rows(x_ref, g_ref, out_ref, rows, chunk):
    g = g_ref[...]

    def body(c, carry):
        r0 = pl.multiple_of(c * chunk, chunk)
        x = x_ref[pl.ds(r0, chunk), :]
        ms = jnp.mean(x * x, axis=-1, keepdims=True)
        out_ref[pl.ds(r0, chunk), :] = ((x * lax.rsqrt(ms + EPS)) * g).astype(out_ref.dtype)
        return carry

    lax.fori_loop(0, rows // chunk, body, 0)


def _zero_after(x):
    z = pltpu.bitcast(x, jnp.uint32)
    z = lax.shift_right_logical(lax.shift_right_logical(z, jnp.uint32(16)), jnp.uint32(16))
    return pltpu.bitcast(z, F32)


def _rmsnorm_kernel(x_ref, g_ref, o_ref, *, bm):
    _rmsnorm_rows(x_ref, g_ref, o_ref, bm, min(bm, 128))


def _rmsnorm(x, g, *, bm, name):
    m, k = x.shape
    return pl.pallas_call(
        functools.partial(_rmsnorm_kernel, bm=bm),
        out_shape=jax.ShapeDtypeStruct((m, k), BF16),
        grid=(m // bm,),
        in_specs=[pl.BlockSpec((bm, k), lambda i: (i, 0)), pl.BlockSpec((1, k), lambda i: (0, 0))],
        out_specs=pl.BlockSpec((bm, k), lambda i: (i, 0)),
        compiler_params=_params("arbitrary"),
        name=name,
    )(x, g.reshape(1, k))


def _xw_kernel(xn_ref, w_ref, o_ref, wb_ref, *, bm, rc, gelu):
    @pl.when(pl.program_id(1) == 0)
    def _():
        wb_ref[...] = w_ref[...].astype(BF16)

    for c in range(bm // rc):
        y = jnp.dot(xn_ref[c * rc:(c + 1) * rc, :], wb_ref[...], preferred_element_type=F32)
        if gelu:
            y = jax.nn.gelu(y, approximate=True)
        o_ref[c * rc:(c + 1) * rc, :] = y.astype(o_ref.dtype)


def _xw(xn, w, layer, *, bm, bn, gelu, name):
    m, k = xn.shape
    n = w.shape[2]
    rc = min(bm, 256)
    return pl.pallas_call(
        functools.partial(_xw_kernel, bm=bm, rc=rc, gelu=gelu),
        out_shape=jax.ShapeDtypeStruct((m, n), F32),
        grid=(n // bn, m // bm),
        in_specs=[
            pl.BlockSpec((bm, k), lambda j, i: (i, 0)),
            pl.BlockSpec((None, k, bn), lambda j, i: (layer, 0, j)),
        ],
        out_specs=pl.BlockSpec((bm, bn), lambda j, i: (i, j)),
        scratch_shapes=[pltpu.VMEM((k, bn), BF16)],
        compiler_params=_params("arbitrary", "arbitrary"),
        name=name,
    )(xn, w)


def _matmul_norm_res_kernel(*refs, splits, nk, bm, rc, emit_next, weights):
    na = len(splits)
    a_refs = refs[:na]
    rest = list(refs[na:])
    wb_ref = rest.pop() if weights != "bf16" else None
    if emit_next:
        w_ref, g_ref, r_ref, gn_ref, o_ref, xn_ref = rest
    else:
        w_ref, g_ref, r_ref, o_ref = rest
    i = pl.program_id(0)
    k = pl.program_id(1)
    if weights == "stream":
        wb_ref[...] = w_ref[...].astype(BF16)
        w_blk = wb_ref
    elif weights == "resident":
        @pl.when(i == 0)
        def _():
            wb_ref[k] = w_ref[...].astype(BF16)

        w_blk = wb_ref.at[k]
    else:
        w_blk = w_ref

    def finish(r0, y):
        rows = slice(r0, r0 + NORM_ROWS)
        ms = jnp.mean(y * y, axis=-1, keepdims=True)
        h = r_ref[rows, :] + (y * lax.rsqrt(ms + EPS)) * g_ref[...]
        o_ref[rows, :] = h
        if emit_next:
            ms2 = jnp.mean(h * h, axis=-1, keepdims=True)
            xn_ref[rows, :] = ((h * lax.rsqrt(ms2 + EPS)) * gn_ref[...]).astype(BF16)

    def step(a_ref, first, last):
        for c in range(bm // rc):
            rows = slice(c * rc, (c + 1) * rc)
            y = jnp.dot(a_ref[rows, :], w_blk[...], preferred_element_type=F32)
            if not first:
                y = o_ref[rows, :] + y
            if last:
                for r in range(0, rc, NORM_ROWS):
                    finish(c * rc + r, y[r:r + NORM_ROWS])
            else:
                o_ref[rows, :] = y

    for a_ref, (k0, k1) in zip(a_refs, splits):
        for first, last in sorted({(kk == 0, kk == nk - 1) for kk in range(k0, k1)}):
            ks = [kk for kk in range(k0, k1) if (kk == 0, kk == nk - 1) == (first, last)]
            pl.when((k >= ks[0]) & (k <= ks[-1]))(functools.partial(step, a_ref, first, last))


def _matmul_norm_res(a_list, w, layer, g, resid, g_next, *, bm, bk, resident, name):
    m = a_list[0].shape[0]
    n = w.shape[2]
    splits, k0 = [], 0
    for a in a_list:
        splits.append((k0, k0 + a.shape[1] // bk))
        k0 = splits[-1][1]
    nk = k0
    rc = min(bm, 256)
    emit_next = g_next is not None
    weights = "bf16" if w.dtype == BF16 else ("resident" if resident else "stream")

    def a_spec(k0, k1):
        return pl.BlockSpec((bm, bk), lambda i, k: (i, jnp.clip(k - k0, 0, k1 - k0 - 1)))

    if weights == "resident":
        w_spec = pl.BlockSpec((None, bk, n), lambda i, k: (layer, jnp.where(i == 0, k, nk - 1), 0))
        scratch = [pltpu.VMEM((nk, bk, n), BF16)]
    else:
        w_spec = pl.BlockSpec((None, bk, n), lambda i, k: (layer, k, 0))
        scratch = [pltpu.VMEM((bk, n), BF16)] if weights == "stream" else []
    row_vec = pl.BlockSpec((1, n), lambda i, k: (0, 0))
    tile = pl.BlockSpec((bm, n), lambda i, k: (i, 0))
    out = pl.pallas_call(
        functools.partial(_matmul_norm_res_kernel, splits=tuple(splits), nk=nk, bm=bm, rc=rc,
                          emit_next=emit_next, weights=weights),
        out_shape=((jax.ShapeDtypeStruct((m, n), F32), jax.ShapeDtypeStruct((m, n), BF16)) if emit_next
                   else jax.ShapeDtypeStruct((m, n), F32)),
        grid=(m // bm, nk),
        in_specs=[a_spec(*sp) for sp in splits] + [w_spec, row_vec, tile] + ([row_vec] if emit_next else []),
        out_specs=(tile, tile) if emit_next else tile,
        scratch_shapes=scratch,
        compiler_params=_params("arbitrary", "arbitrary"),
        name=name,
    )(*a_list, w, g.reshape(1, n), resid, *([g_next.reshape(1, n)] if emit_next else []))
    return out if emit_next else (out, None)


def _down_kernel(*refs, bm, emit_next):
    if emit_next:
        a_ref, w_hbm_ref, g_ref, r_ref, gn_ref, o_ref, xn_ref, w_ref, sem = refs
    else:
        a_ref, w_hbm_ref, g_ref, r_ref, o_ref, w_ref, sem = refs

    @pl.when(pl.program_id(0) == 0)
    def _():
        copy = pltpu.make_async_copy(w_hbm_ref, w_ref, sem)
        copy.start()
        copy.wait()

    y = jnp.dot(a_ref[...], w_ref[...], preferred_element_type=F32)
    for r in range(0, bm, NORM_ROWS):
        rows = slice(r, r + NORM_ROWS)
        yy = y[rows]
        ms = jnp.mean(yy * yy, axis=-1, keepdims=True)
        h = r_ref[rows, :] + (yy * lax.rsqrt(ms + EPS)) * g_ref[...]
        o_ref[rows, :] = h
        if emit_next:
            ms2 = jnp.mean(h * h, axis=-1, keepdims=True)
            xn_ref[rows, :] = ((h * lax.rsqrt(ms2 + EPS)) * gn_ref[...]).astype(BF16)


def _down_norm_res(a, w, g, resid, g_next, *, bm, name):
    m, kdim = a.shape
    n = w.shape[1]
    emit_next = g_next is not None
    row_vec = pl.BlockSpec((1, n), lambda i: (0, 0))
    tile = pl.BlockSpec((bm, n), lambda i: (i, 0))
    out = pl.pallas_call(
        functools.partial(_down_kernel, bm=bm, emit_next=emit_next),
        out_shape=((jax.ShapeDtypeStruct((m, n), F32), jax.ShapeDtypeStruct((m, n), BF16)) if emit_next
                   else jax.ShapeDtypeStruct((m, n), F32)),
        grid=(m // bm,),
        in_specs=[pl.BlockSpec((bm, kdim), lambda i: (i, 0)), pl.BlockSpec(memory_space=pl.ANY), row_vec, tile]
        + ([row_vec] if emit_next else []),
        out_specs=(tile, tile) if emit_next else tile,
        scratch_shapes=[pltpu.VMEM((kdim, n), BF16), pltpu.SemaphoreType.DMA(())],
        compiler_params=_params("arbitrary"),
        name=name,
    )(a, w, g.reshape(1, n), resid, *([g_next.reshape(1, n)] if emit_next else []))
    return out if emit_next else (out, None)


def _ffn_up_kernel(*refs, groups, rows_per_group, rc, tiles_per_seq, cast_down):
    if cast_down:
        (xn_ref, wa_ref, wg_ref, cwa_ref, cwg_ref, sa_ref, sg_ref, wd_ref,
         act_ref, la_ref, lg_ref, wdb_ref, wb_ref, carry_ref) = refs
    else:
        (xn_ref, wa_ref, wg_ref, cwa_ref, cwg_ref, sa_ref, sg_ref,
         act_ref, la_ref, lg_ref, wb_ref, carry_ref) = refs
    i = pl.program_id(1)
    bm = groups * rows_per_group
    bn = act_ref.shape[1]
    nsub = bn // SUB_COLS
    subs = [slice(u * SUB_COLS, (u + 1) * SUB_COLS) for u in range(nsub)]

    @pl.when(i == 0)
    def _():
        for u in range(nsub):
            wb_ref[u, :, :SUB_COLS] = wa_ref[:, subs[u]].astype(BF16)
            wb_ref[u, :, SUB_COLS:] = wg_ref[:, subs[u]].astype(BF16)
        if cast_down:
            wdb_ref[...] = wd_ref[...].astype(BF16)

    keep = SUBLANES - (CONV_W - 1)

    @pl.when(lax.rem(i, tiles_per_seq) == 0)
    def _():
        carry_ref[...] = jnp.zeros(carry_ref.shape, F32)
        carry_ref[0, :, keep:, :] = sa_ref[...]
        carry_ref[1, :, keep:, :] = sg_ref[...]

    cws = [jnp.concatenate([cwa_ref[:, subs[u]], cwg_ref[:, subs[u]]], axis=1) for u in range(nsub)]
    piece = min(EPI_ROWS, rows_per_group)
    prev = [None] * nsub
    after = [None] * nsub
    for c in range(bm // rc):
        xc = xn_ref[c * rc:(c + 1) * rc, :]
        for u in range(nsub):
            h = jnp.dot(xc, wb_ref[u], preferred_element_type=F32)
            for q in range(rc // piece):
                row = c * rc + q * piece
                grp = row // rows_per_group
                hcur = h[q * piece:(q + 1) * piece]
                if row % rows_per_group == 0:
                    prev8 = jnp.concatenate([carry_ref[0, grp, :, subs[u]], carry_ref[1, grp, :, subs[u]]], axis=1)
                else:
                    prev8 = prev[u]
                hext = jnp.concatenate([prev8, hcur], axis=0)
                s1 = pltpu.roll(hext, 1, 0)[SUBLANES:]
                s2 = pltpu.roll(hext, 2, 0)[SUBLANES:]
                cw = cws[u] if after[u] is None else cws[u] + jnp.concatenate([after[u]] * 2, axis=1)
                conv = cw[3:4] + s2 * cw[0:1] + s1 * cw[1:2] + hcur * cw[2:3]
                act = jax.nn.gelu(conv[:, SUB_COLS:], approximate=True) * conv[:, :SUB_COLS]
                act_ref[row:row + piece, subs[u]] = act.astype(BF16)
                after[u] = _zero_after(act[piece - SUBLANES:])
                prev[u] = hcur[piece - SUBLANES:]
                if (row + piece) % rows_per_group == 0:
                    la_ref[grp, :, subs[u]] = prev[u][keep:, :SUB_COLS]
                    lg_ref[grp, :, subs[u]] = prev[u][keep:, SUB_COLS:]
                    carry_ref[0, grp, :, subs[u]] = prev[u][:, :SUB_COLS]
                    carry_ref[1, grp, :, subs[u]] = prev[u][:, SUB_COLS:]


def _ffn_up(xn, w_up, layer, conv_w, conv_b, state, w_down, *, groups, rows_per_group, bn, name):
    m, k = xn.shape
    nseq = state.shape[0]
    bm = groups * rows_per_group
    ni = m // bm
    tiles_per_seq = ni * groups // nseq
    nj = D_FF // bn
    rc = min(bm, 256)
    cw = jnp.concatenate([conv_w, conv_b[None], jnp.zeros((SUBLANES - CONV_W - 1, 2 * D_FF), F32)], axis=0)
    cast_down = w_down is not None
    d_out = w_down.shape[2] if cast_down else 0
    out_shape = [jax.ShapeDtypeStruct((m, D_FF), BF16),
                 jax.ShapeDtypeStruct((nseq, CONV_W - 1, D_FF), F32),
                 jax.ShapeDtypeStruct((nseq, CONV_W - 1, D_FF), F32)]
    in_specs = [
        pl.BlockSpec((bm, k), lambda j, i: (i, 0)),
        pl.BlockSpec((None, k, bn), lambda j, i: (layer, 0, j)),
        pl.BlockSpec((None, k, bn), lambda j, i: (layer, 0, nj + j)),
        pl.BlockSpec((SUBLANES, bn), lambda j, i: (0, j)),
        pl.BlockSpec((SUBLANES, bn), lambda j, i: (0, nj + j)),
        pl.BlockSpec((groups, CONV_W - 1, bn), lambda j, i: (i // tiles_per_seq, 0, j)),
        pl.BlockSpec((groups, CONV_W - 1, bn), lambda j, i: (i // tiles_per_seq, 0, nj + j)),
    ]
    out_specs = [
        pl.BlockSpec((bm, bn), lambda j, i: (i, j)),
        pl.BlockSpec((groups, CONV_W - 1, bn), lambda j, i: (i // tiles_per_seq, 0, j)),
        pl.BlockSpec((groups, CONV_W - 1, bn), lambda j, i: (i // tiles_per_seq, 0, j)),
    ]
    operands = [xn, w_up, w_up, cw, cw, state, state]
    if cast_down:
        in_specs.append(pl.BlockSpec((None, bn, d_out), lambda j, i: (layer, j, 0)))
        out_specs.append(pl.BlockSpec((None, bn, d_out), lambda j, i: (0, j, 0)))
        out_shape.append(jax.ShapeDtypeStruct((1, D_FF, d_out), BF16))
        operands.append(w_down)
    outs = pl.pallas_call(
        functools.partial(_ffn_up_kernel, groups=groups, rows_per_group=rows_per_group, rc=rc,
                          tiles_per_seq=tiles_per_seq, cast_down=cast_down),
        out_shape=tuple(out_shape),
        grid=(nj, ni),
        in_specs=in_specs,
        out_specs=tuple(out_specs),
        scratch_shapes=[pltpu.VMEM((bn // SUB_COLS, k, 2 * SUB_COLS), BF16),
                        pltpu.VMEM((2, groups, SUBLANES, bn), F32)],
        compiler_params=_params("arbitrary", "arbitrary"),
        name=name,
    )(*operands)
    act, la, lg = outs[:3]
    return act, jnp.concatenate([la, lg], axis=-1), (outs[3] if cast_down else None)


def _attn_head(q, k, v, sink_ref, h, bias):
    nq = q.shape[0]
    kh = k[:, h * HEAD_DIM:(h + 1) * HEAD_DIM].astype(BF16)
    vh = v[:, h * HEAD_DIM:(h + 1) * HEAD_DIM].astype(BF16)
    qs, sk = [], []
    for gq in range(Q_PER_KV):
        c0 = (h * Q_PER_KV + gq) * HEAD_DIM
        qs.append(q[:, c0:c0 + HEAD_DIM])
        sk.append(jnp.full((nq, 1), sink_ref[h * Q_PER_KV + gq], F32))
    qh = (jnp.concatenate(qs, axis=0) * (HEAD_DIM ** -0.5)).astype(BF16)
    sk = jnp.concatenate(sk, axis=0)
    s = lax.dot_general(qh, kh, (((1,), (1,)), ((), ())), preferred_element_type=F32)
    if bias is not None:
        s = s + bias
    mx = jnp.maximum(jnp.max(s, axis=-1, keepdims=True), sk)
    p = jnp.exp(s - mx)
    den = jnp.sum(p, axis=-1, keepdims=True) + jnp.exp(sk - mx)
    o = jnp.dot((p / den).astype(BF16), vh, preferred_element_type=F32)
    return jnp.concatenate([o[gq * nq:(gq + 1) * nq] for gq in range(Q_PER_KV)], axis=1)


ATTN_UNIT = 2 * CHUNK


def _attn_prompt_kernel(sink_ref, q_ref, kv_ref, kvp_ref, o_ref, bias_ref, *, tq):
    i = pl.program_id(1)
    nk = ATTN_UNIT + WINDOW
    cols = Q_PER_KV * ATTN_UNIT
    kvw = N_KV_HEADS * HEAD_DIM
    @pl.when(i == 0)
    def _():
        r_k = lax.broadcasted_iota(jnp.int32, (nk, cols), 0)
        r_q = lax.broadcasted_iota(jnp.int32, (nk, cols), 1)
        lo = (r_q & (ATTN_UNIT - 1)) & ~(CHUNK - 1)
        band = (r_k >= lo) & (r_k < lo + WINDOW + CHUNK)
        bias_ref[1] = jnp.where(band, 0.0, -jnp.inf)
        bias_ref[0] = jnp.where(band & (r_k >= WINDOW), 0.0, -jnp.inf)

    first_unit_bias = jnp.where(i == 0, 0, 1)

    kv_all = jnp.concatenate([kvp_ref[0], kv_ref[0]], axis=0)
    k_all = kv_all[:, :kvw].astype(BF16)
    vt_all = jnp.transpose(kv_all[:, kvw:]).astype(BF16)
    for u in range(tq // ATTN_UNIT):
        rs = slice(u * ATTN_UNIT, (u + 1) * ATTN_UNIT)
        keys = slice(u * ATTN_UNIT, u * ATTN_UNIT + nk)
        k_u = k_all[keys]
        qt = jnp.transpose(q_ref[0, rs, :] * (HEAD_DIM ** -0.5)).astype(BF16)
        bias = bias_ref[first_unit_bias] if u == 0 else bias_ref[1]
        pieces = []
        for h in range(N_KV_HEADS):
            heads = [h * Q_PER_KV + gq for gq in range(Q_PER_KV)]
            qt_h = jnp.concatenate([qt[n * HEAD_DIM:(n + 1) * HEAD_DIM] for n in heads], axis=1)
            parts = [qt_h]
            if h > 0:
                parts.insert(0, jnp.zeros((h * HEAD_DIM, cols), BF16))
            if h < N_KV_HEADS - 1:
                parts.append(jnp.zeros(((N_KV_HEADS - 1 - h) * HEAD_DIM, cols), BF16))
            st = jnp.dot(k_u, jnp.concatenate(parts, axis=0), preferred_element_type=F32) + bias
            sk = jnp.concatenate([jnp.full((1, ATTN_UNIT), sink_ref[n], F32) for n in heads], axis=1)
            mx = jnp.maximum(jnp.max(st, axis=0, keepdims=True), sk)
            p = jnp.exp(st - mx)
            den = jnp.sum(p, axis=0, keepdims=True) + jnp.exp(sk - mx)
            ot = jnp.dot(vt_all[h * HEAD_DIM:(h + 1) * HEAD_DIM, keys], (p / den).astype(BF16),
                         preferred_element_type=F32)
            pieces += [ot[:, gq * ATTN_UNIT:(gq + 1) * ATTN_UNIT] for gq in range(Q_PER_KV)]
        o_ref[0, rs, :] = jnp.transpose(jnp.concatenate(pieces, axis=0)).astype(BF16)


def _attn_prompt(z, sinks, *, tq):
    b, t, _ = z.shape
    per = tq // WINDOW
    return pl.pallas_call(
        functools.partial(_attn_prompt_kernel, tq=tq),
        out_shape=jax.ShapeDtypeStruct((b, t, N_Q_HEADS * HEAD_DIM), BF16),
        grid=(b, t // tq),
        in_specs=[
            pl.BlockSpec(memory_space=pltpu.SMEM),
            pl.BlockSpec((1, tq, 1024), lambda bb, i: (bb, i, 0)),
            pl.BlockSpec((1, tq, COL_BLK), lambda bb, i: (bb, i, KV_BLK)),
            pl.BlockSpec((1, WINDOW, COL_BLK), lambda bb, i: (bb, jnp.maximum(i * per - 1, 0), KV_BLK)),
        ],
        out_specs=pl.BlockSpec((1, tq, 1024), lambda bb, i: (bb, i, 0)),
        scratch_shapes=[pltpu.VMEM((2, ATTN_UNIT + WINDOW, Q_PER_KV * ATTN_UNIT), F32)],
        compiler_params=_params("arbitrary", "arbitrary"),
        name="attn_prompt",
    )(sinks, z, z, z)


def _attn_sample_kernel(sink_ref, q_ref, kv_ref, ck_ref, cv_ref, o_ref):
    for s in range(q_ref.shape[0]):
        kv = kv_ref[s]
        k = jnp.concatenate([ck_ref[s], kv[:, :256]], axis=0)
        v = jnp.concatenate([cv_ref[s], kv[:, 256:]], axis=0)
        q = q_ref[s]
        for h in range(N_KV_HEADS):
            o_ref[s, :, h * 256:(h + 1) * 256] = _attn_head(q, k, v, sink_ref, h, None).astype(BF16)


def _attn_sample(z, cache_k, cache_v, sinks):
    b, t, _ = z.shape
    nc = cache_k.shape[1]
    nb = ATTN_SAMPLE_SEQS_PER_STEP
    return pl.pallas_call(
        _attn_sample_kernel,
        out_shape=jax.ShapeDtypeStruct((b, t, N_Q_HEADS * HEAD_DIM), BF16),
        grid=(b // nb,),
        in_specs=[
            pl.BlockSpec(memory_space=pltpu.SMEM),
            pl.BlockSpec((nb, t, 1024), lambda bb: (bb, 0, 0)),
            pl.BlockSpec((nb, t, COL_BLK), lambda bb: (bb, 0, KV_BLK)),
            pl.BlockSpec((nb, nc, 256), lambda bb: (bb, 0, 0)),
            pl.BlockSpec((nb, nc, 256), lambda bb: (bb, 0, 0)),
        ],
        out_specs=pl.BlockSpec((nb, t, 1024), lambda bb: (bb, 0, 0)),
        compiler_params=_params("arbitrary"),
        name="attn_sample",
    )(sinks, z, z, cache_k.reshape(b, nc, 256), cache_v.reshape(b, nc, 256))


def _retention_kernel(rq_ref, rk_ref, rv_ref, rg_ref, cs_ref, sn_ref, intra_ref, qd_ref, kd_ref, cd_ref,
                      ng_ref, s0_ref, o_ref, st_ref):
    c = pl.program_id(2)

    @pl.when(c == 0)
    def _():
        st_ref[...] = s0_ref[...]

    cs = cs_ref[...]
    sn = sn_ref[...]
    for s, hh in [(s, hh) for s in range(rq_ref.shape[0]) for hh in range(RET_HEADS_PER_BLK)]:
        lanes = slice(hh * RET_DK, (hh + 1) * RET_DK)
        q = rq_ref[s, :, lanes]
        k = rk_ref[s, :, lanes]
        qr = q * cs + pltpu.roll(q, RET_DK // 2, 1) * sn
        kr = (k * cs + pltpu.roll(k, RET_DK // 2, 1) * sn) * (RET_DK ** -0.5)
        qb = qr.astype(BF16)
        kb = kr.astype(BF16)
        vb = rv_ref[s, :, lanes].astype(BF16)
        sc = lax.dot_general(qb, kb, (((1,), (1,)), ((), ())), preferred_element_type=F32) * intra_ref[hh]
        inner = jnp.dot(sc.astype(BF16), vb, preferred_element_type=F32)
        state = st_ref[s, hh]
        cross = jnp.dot(qb, state.astype(BF16), preferred_element_type=F32) * qd_ref[hh]
        kdt = jnp.transpose(kr * kd_ref[hh]).astype(BF16)
        st_ref[s, hh] = cd_ref[hh, 0:1, :] * state + jnp.dot(kdt, vb, preferred_element_type=F32)
        r = inner + cross
        mu = jnp.mean(r, axis=-1, keepdims=True)
        yc = r - mu
        yn = yc * lax.rsqrt(jnp.mean(yc * yc, axis=-1, keepdims=True) + EPS)
        o_ref[s, :, lanes] = ((yn * ng_ref[:, lanes]) * jax.nn.silu(rg_ref[s, :, lanes])).astype(BF16)


def _ret_log_gamma():
    return jnp.log1p(-jnp.exp2(-5.0 - jnp.arange(N_RET_HEADS, dtype=F32)))


def _retention(z, pos, state0, norm_g, *, chunk, nbatch):
    b, t, _ = z.shape
    log_g = _ret_log_gamma()
    idx = jnp.arange(chunk, dtype=F32)
    diff = idx[:, None] - idx[None, :]
    intra = jnp.where(diff[None] >= 0.0,
                      jnp.exp(log_g[:, None, None] * jnp.maximum(diff, 0.0)[None]), 0.0)
    ones = jnp.ones((1, 1, RET_DV), F32)
    q_decay = jnp.exp(log_g[:, None] * (idx[None, :] + 1.0))[:, :, None] * ones
    k_decay = jnp.exp(log_g[:, None] * (chunk - 1.0 - idx)[None, :])[:, :, None] * ones
    c_decay = jnp.exp(log_g * chunk)[:, None, None] * jnp.ones((1, SUBLANES, RET_DV), F32)
    half = RET_DK // 2
    freq = 1.0 / (ROPE_BASE ** (jnp.arange(half, dtype=F32) / half))
    ang = pos.astype(F32)[:, None] * freq[None, :]
    cos, sin = jnp.cos(ang), jnp.sin(ang)
    cs = jnp.concatenate([cos, cos], axis=-1)
    sn = jnp.concatenate([-sin, sin], axis=-1)
    nhb = N_RET_HEADS // RET_HEADS_PER_BLK
    hb = RET_HEADS_PER_BLK

    def zspec(blk):
        return pl.BlockSpec((nbatch, chunk, COL_BLK), lambda bb, g, c: (bb, c, blk + g))

    def tab(rows):
        return pl.BlockSpec((hb, rows, RET_DV), lambda bb, g, c: (g, 0, 0))

    state_spec = pl.BlockSpec((nbatch, hb, RET_DK, RET_DV), lambda bb, g, c: (bb, g, 0, 0))
    out, st = pl.pallas_call(
        _retention_kernel,
        out_shape=(jax.ShapeDtypeStruct((b, t, N_RET_HEADS * RET_DV), BF16),
                   jax.ShapeDtypeStruct((b, N_RET_HEADS, RET_DK, RET_DV), F32)),
        grid=(b // nbatch, nhb, t // chunk),
        in_specs=[
            zspec(RQ_BLK), zspec(RK_BLK), zspec(RV_BLK), zspec(RG_BLK),
            pl.BlockSpec((chunk, RET_DK), lambda bb, g, c: (c, 0)),
            pl.BlockSpec((chunk, RET_DK), lambda bb, g, c: (c, 0)),
            pl.BlockSpec((hb, chunk, chunk), lambda bb, g, c: (g, 0, 0)),
            tab(chunk), tab(chunk), tab(SUBLANES),
            pl.BlockSpec((1, COL_BLK), lambda bb, g, c: (0, g)),
            state_spec,
        ],
        out_specs=(pl.BlockSpec((nbatch, chunk, COL_BLK), lambda bb, g, c: (bb, c, g)), state_spec),
        compiler_params=_params("arbitrary", "arbitrary", "arbitrary"),
        name="retention",
    )(z, z, z, z, cs, sn, intra, q_decay, k_decay, c_decay, norm_g.reshape(1, -1), state0)
    return out, st


def _gate_kernel(u_ref, v_ref, lg_ref, lb_ref, ws_ref, bs_ref, y_ref, *maybe_vn_ref, rows, span):
    ri = lax.broadcasted_iota(jnp.int32, (span, span), 0)
    ci = lax.broadcasted_iota(jnp.int32, (span, span), 1)
    wt = [jnp.where(ri >= ci, ws_ref[g], 0.0).astype(BF16) for g in range(GM_GROUPS)]
    bs = bs_ref[...]
    lg = lg_ref[...]
    lb = lb_ref[...]
    for s, c in [(s, c) for s in range(u_ref.shape[0]) for c in range(rows // span)]:
        rs = slice(c * span, (c + 1) * span)
        v = v_ref[s, rs, :]
        mu = jnp.mean(v, axis=-1, keepdims=True)
        xc = v - mu
        vn = (xc * lax.rsqrt(jnp.mean(xc * xc, axis=-1, keepdims=True) + EPS)) * lg + lb
        for vn_ref in maybe_vn_ref:
            vn_ref[s, rs, :] = vn
        vb = vn.astype(BF16)
        for g in range(GM_GROUPS):
            cols = slice(g * GM_GROUP_DIM, (g + 1) * GM_GROUP_DIM)
            mixed = jnp.dot(wt[g], vb[:, cols], preferred_element_type=F32) + bs[:, g:g + 1]
            y_ref[s, rs, cols] = (u_ref[s, rs, cols] * mixed).astype(BF16)


def _spatial_gate(uv, ln_g, ln_b, ws, bs, *, rows, nbatch, emit_vn):
    b, t, _ = uv.shape
    span = min(t, GM_CHUNK)
    ws_l = ws[:, :span, :span]
    bs_t = jnp.transpose(bs[:, :span])
    tile = pl.BlockSpec((nbatch, rows, D_MODEL), lambda bb, i: (bb, i, 0))
    outs = pl.pallas_call(
        functools.partial(_gate_kernel, rows=rows, span=span),
        out_shape=(jax.ShapeDtypeStruct((b, t, D_MODEL), BF16),)
        + ((jax.ShapeDtypeStruct((b, t, D_MODEL), F32),) if emit_vn else ()),
        grid=(b // nbatch, t // rows),
        in_specs=[
            pl.BlockSpec((nbatch, rows, D_MODEL), lambda bb, i: (bb, i, 0)),
            pl.BlockSpec((nbatch, rows, D_MODEL), lambda bb, i: (bb, i, 1)),
            pl.BlockSpec((1, D_MODEL), lambda bb, i: (0, 0)),
            pl.BlockSpec((1, D_MODEL), lambda bb, i: (0, 0)),
            pl.BlockSpec((GM_GROUPS, span, span), lambda bb, i: (0, 0, 0)),
            pl.BlockSpec((span, GM_GROUPS), lambda bb, i: (0, 0)),
        ],
        out_specs=(tile, tile) if emit_vn else (tile,),
        compiler_params=_params("arbitrary", "arbitrary"),
        name="spatial_gate",
    )(uv, uv, ln_g.reshape(1, -1), ln_b.reshape(1, -1), ws_l, bs_t)
    return outs if emit_vn else (outs[0], None)


ROW_TILE = 1024
FFN_ROW_TILE = 2048
DOWN_ROW_TILE = 256
OUT_ROW_TILE = 512
OUT_K_BLK = 1024
IN_EVEN_BLK = 1408
IN_ODD_BLK = 1024
FFN_BLK = 512
ATTN_ROW_TILE = 512


def _conv_ffn(hp, xp, hs, xs, g3, g_next, w_up, layer, conv_w, conv_b, w_down, state_p, state_s):
    nseq_p, nseq_s = state_p.shape[0], state_s.shape[0]
    act_p, c_p, wdb = _ffn_up(xp, w_up, layer, conv_w, conv_b, state_p, w_down, groups=1,
                              rows_per_group=FFN_ROW_TILE, bn=FFN_BLK, name="ffn_up_p")
    act_s, c_s, _ = _ffn_up(xs, w_up, layer, conv_w, conv_b, state_s, None, groups=nseq_s,
                            rows_per_group=xs.shape[0] // nseq_s, bn=FFN_BLK, name="ffn_up_s")
    wdb = wdb.reshape(wdb.shape[1:])
    hp, xp = _down_norm_res(act_p, wdb, g3, hp, g_next, bm=DOWN_ROW_TILE, name="ffn_down_p")
    hs, xs = _down_norm_res(act_s, wdb, g3, hs, g_next, bm=DOWN_ROW_TILE, name="ffn_down_s")
    return hp, xp, c_p, hs, xs, c_s


def kernel(x_prompt, x_sample, cache_swa_k, cache_swa_v, state_ret, state_ffn_conv, norm_g, w_in_even,
           w_out_even, attn_sinks, ret_norm_g, w_in_odd, w_out_odd, gm_ln_g, gm_ln_b, gm_ws, gm_bs,
           ffn_w_up, ffn_conv_w, ffn_conv_b, ffn_w_down):
    nb, seq, d = x_prompt.shape
    db, dseq, _ = x_sample.shape
    depth = norm_g.shape[0]
    hp = x_prompt.reshape(nb * seq, d)
    hs = x_sample.reshape(db * dseq, d)
    bm_p, bm_s = ROW_TILE, db * dseq
    pos_p = jnp.arange(seq)
    pos_s = PAST_LEN + jnp.arange(dseq)
    kp_l, vp_l, rp_l, cp_l = [], [], [], []
    ks_l, vs_l, rs_l, cs_l, gv_l = [], [], [], [], []
    xp = _rmsnorm(hp, norm_g[0, 0], bm=ROW_TILE, name="norm_in_p")
    xs = _rmsnorm(hs, norm_g[0, 0], bm=bm_s, name="norm_in_s")
    for layer in range(depth):
        g = norm_g[layer]
        g_next = norm_g[layer + 1, 0] if layer + 1 < depth else None
        if layer % 2 == 0:
            e = layer // 2
            zp = _xw(xp, w_in_even, e, bm=bm_p, bn=IN_EVEN_BLK, gelu=False, name="in_even_p")
            zs = _xw(xs, w_in_even, e, bm=bm_s, bn=IN_EVEN_BLK, gelu=False, name="in_even_s")
            zp = zp.reshape(nb, seq, EVEN_IN)
            zs = zs.reshape(db, dseq, EVEN_IN)
            attn_p = _attn_prompt(zp, attn_sinks[e], tq=ATTN_ROW_TILE)
            attn_s = _attn_sample(zs, cache_swa_k[e], cache_swa_v[e], attn_sinks[e])
            ret_p, r_p = _retention(zp, pos_p, jnp.zeros((nb, N_RET_HEADS, RET_DK, RET_DV), F32),
                                    ret_norm_g[e], chunk=RET_CHUNK, nbatch=1)
            ret_s, r_s = _retention(zs, pos_s, state_ret[e].astype(F32), ret_norm_g[e], chunk=dseq,
                                    nbatch=RET_SAMPLE_SEQS_PER_STEP)
            mixed_p = [attn_p.reshape(nb * seq, -1), ret_p.reshape(nb * seq, -1)]
            mixed_s = [attn_s.reshape(db * dseq, -1), ret_s.reshape(db * dseq, -1)]
            hp, xp = _matmul_norm_res(mixed_p, w_out_even, e, g[1], hp, g[2], bm=OUT_ROW_TILE, bk=OUT_K_BLK,
                                      resident=True, name="out_even_p")
            hs, xs = _matmul_norm_res(mixed_s, w_out_even, e, g[1], hs, g[2], bm=bm_s, bk=OUT_K_BLK,
                                      resident=False, name="out_even_s")
            k_new = zs[:, :, 1024:1280].reshape(db, dseq, N_KV_HEADS, HEAD_DIM)
            v_new = zs[:, :, 1280:1536].reshape(db, dseq, N_KV_HEADS, HEAD_DIM)
            n_keep = cache_swa_k.shape[2]
            kp_l.append(zp[:, seq - WINDOW:, 1024:1280].reshape(nb, WINDOW, N_KV_HEADS, HEAD_DIM))
            vp_l.append(zp[:, seq - WINDOW:, 1280:1536].reshape(nb, WINDOW, N_KV_HEADS, HEAD_DIM))
            ks_l.append(jnp.concatenate([cache_swa_k[e], k_new], axis=1)[:, -n_keep:])
            vs_l.append(jnp.concatenate([cache_swa_v[e], v_new], axis=1)[:, -n_keep:])
            rp_l.append(r_p)
            rs_l.append(r_s.astype(state_ret.dtype))
        else:
            o = layer // 2
            uvp = _xw(xp, w_in_odd, o, bm=bm_p, bn=IN_ODD_BLK, gelu=True, name="in_odd_p")
            uvs = _xw(xs, w_in_odd, o, bm=bm_s, bn=IN_ODD_BLK, gelu=True, name="in_odd_s")
            yp, _ = _spatial_gate(uvp.reshape(nb, seq, -1), gm_ln_g[o], gm_ln_b[o], gm_ws[o], gm_bs[o], rows=1024,
                                  nbatch=1, emit_vn=False)
            ys, gv = _spatial_gate(uvs.reshape(db, dseq, -1), gm_ln_g[o], gm_ln_b[o], gm_ws[o], gm_bs[o],
                                   rows=dseq, nbatch=GATE_SAMPLE_SEQS_PER_STEP, emit_vn=True)
            hp, xp = _matmul_norm_res([yp.reshape(nb * seq, -1)], w_out_odd, o, g[1], hp, g[2], bm=OUT_ROW_TILE,
                                      bk=OUT_K_BLK, resident=True, name="out_odd_p")
            hs, xs = _matmul_norm_res([ys.reshape(db * dseq, -1)], w_out_odd, o, g[1], hs, g[2], bm=bm_s,
                                      bk=OUT_K_BLK, resident=False, name="out_odd_s")
            gv_l.append(gv)
        zero_state = jnp.zeros((nb, CONV_W - 1, 2 * D_FF), F32)
        hp, xp, c_p, hs, xs, c_s = _conv_ffn(hp, xp, hs, xs, g[3], g_next, ffn_w_up, layer, ffn_conv_w[layer],
                                             ffn_conv_b[layer], ffn_w_down, zero_state, state_ffn_conv[layer])
        cp_l.append(c_p)
        cs_l.append(c_s)
    return (hp.reshape(nb, seq, d), hs.reshape(db, dseq, d),
            jnp.stack(kp_l), jnp.stack(vp_l), jnp.stack(rp_l), jnp.stack(cp_l),
            jnp.stack(ks_l), jnp.stack(vs_l), jnp.stack(rs_l), jnp.stack(cs_l), jnp.stack(gv_l))
```

```python
import functools

import jax
import jax.numpy as jnp
from jax import lax
from jax.experimental import pallas as pl
from jax.experimental.pallas import tpu as pltpu

F32 = jnp.float32
BF16 = jnp.bfloat16

D_MODEL = 2048
CHUNK = 64
HEAD_DIM = 64
N_Q_HEADS = 16
N_KV_HEADS = 4
Q_PER_KV = N_Q_HEADS // N_KV_HEADS
WINDOW = 128
N_RET_HEADS = 8
RET_DK = 128
RET_DV = 128
ROPE_BASE = 10000.0
GM_CHUNK = 128
GM_GROUPS = 8
GM_GROUP_DIM = D_MODEL // GM_GROUPS
D_FF = 5632
CONV_W = 3
EPS = 1e-6
PAST_LEN = 1024
EVEN_IN = 5632

SUBLANES = 8
VMEM_LIMIT_BYTES = 56 * 1024 * 1024

COL_BLK = 512
KV_BLK = 2
RQ_BLK, RK_BLK, RV_BLK, RG_BLK = 3, 5, 7, 9
RET_HEADS_PER_BLK = COL_BLK // RET_DK
RET_CHUNK = 256
SUB_COLS = 256
EPI_ROWS = 16
NORM_ROWS = 64
RET_SAMPLE_SEQS_PER_STEP = 4
GATE_SAMPLE_SEQS_PER_STEP = 8
ATTN_SAMPLE_SEQS_PER_STEP = 1


def _params(*sem):
    return pltpu.CompilerParams(dimension_semantics=sem, vmem_limit_bytes=VMEM_LIMIT_BYTES)


def _rmsnorm_rows(x_ref, g_ref, out_ref, rows, chunk):
    g = g_ref[...]

    def body(c, carry):
        r0 = pl.multiple_of(c * chunk, chunk)
        x = x_ref[pl.ds(r0, chunk), :]
        ms = jnp.mean(x * x, axis=-1, keepdims=True)
        out_ref[pl.ds(r0, chunk), :] = ((x * lax.rsqrt(ms + EPS)) * g).astype(out_ref.dtype)
        return carry

    lax.fori_loop(0, rows // chunk, body, 0)


def _zero_after(x):
    z = pltpu.bitcast(x, jnp.uint32)
    z = lax.shift_right_logical(lax.shift_right_logical(z, jnp.uint32(16)), jnp.uint32(16))
    return pltpu.bitcast(z, F32)


def _rmsnorm_kernel(x_ref, g_ref, o_ref, *, bm):
    _rmsnorm_rows(x_ref, g_ref, o_ref, bm, min(bm, 128))


def _rmsnorm(x, g, *, bm, name):
    m, k = x.shape
    return pl.pallas_call(
        functools.partial(_rmsnorm_kernel, bm=bm),
        out_shape=jax.ShapeDtypeStruct((m, k), BF16),
        grid=(m // bm,),
        in_specs=[pl.BlockSpec((bm, k), lambda i: (i, 0)), pl.BlockSpec((1, k), lambda i: (0, 0))],
        out_specs=pl.BlockSpec((bm, k), lambda i: (i, 0)),
        compiler_params=_params("arbitrary"),
        name=name,
    )(x, g.reshape(1, k))


def _xw_kernel(xn_ref, w_ref, o_ref, wb_ref, *, bm, rc, gelu):
    @pl.when(pl.program_id(1) == 0)
    def _():
        wb_ref[...] = w_ref[...].astype(BF16)

    for c in range(bm // rc):
        y = jnp.dot(xn_ref[c * rc:(c + 1) * rc, :], wb_ref[...], preferred_element_type=F32)
        if gelu:
            y = jax.nn.gelu(y, approximate=True)
        o_ref[c * rc:(c + 1) * rc, :] = y.astype(o_ref.dtype)


def _xw(xn, w, layer, *, bm, bn, gelu, name):
    m, k = xn.shape
    n = w.shape[2]
    rc = min(bm, 256)
    return pl.pallas_call(
        functools.partial(_xw_kernel, bm=bm, rc=rc, gelu=gelu),
        out_shape=jax.ShapeDtypeStruct((m, n), F32),
        grid=(n // bn, m // bm),
        in_specs=[
            pl.BlockSpec((bm, k), lambda j, i: (i, 0)),
            pl.BlockSpec((None, k, bn), lambda j, i: (layer, 0, j)),
        ],
        out_specs=pl.BlockSpec((bm, bn), lambda j, i: (i, j)),
        scratch_shapes=[pltpu.VMEM((k, bn), BF16)],
        compiler_params=_params("arbitrary", "arbitrary"),
        name=name,
    )(xn, w)


def _matmul_norm_res_kernel(*refs, splits, nk, bm, rc, emit_next, weights):
    na = len(splits)
    a_refs = refs[:na]
    rest = list(refs[na:])
    wb_ref = rest.pop() if weights != "bf16" else None
    if emit_next:
        w_ref, g_ref, r_ref, gn_ref, o_ref, xn_ref = rest
    else:
        w_ref, g_ref, r_ref, o_ref = rest
    i = pl.program_id(0)
    k = pl.program_id(1)
    if weights == "stream":
        wb_ref[...] = w_ref[...].astype(BF16)
        w_blk = wb_ref
    elif weights == "resident":
        @pl.when(i == 0)
        def _():
            wb_ref[k] = w_ref[...].astype(BF16)

        w_blk = wb_ref.at[k]
    else:
        w_blk = w_ref

    def finish(r0, y):
        rows = slice(r0, r0 + NORM_ROWS)
        ms = jnp.mean(y * y, axis=-1, keepdims=True)
        h = r_ref[rows, :] + (y * lax.rsqrt(ms + EPS)) * g_ref[...]
        o_ref[rows, :] = h
        if emit_next:
            ms2 = jnp.mean(h * h, axis=-1, keepdims=True)
            xn_ref[rows, :] = ((h * lax.rsqrt(ms2 + EPS)) * gn_ref[...]).astype(BF16)

    def step(a_ref, first, last):
        for c in range(bm // rc):
            rows = slice(c * rc, (c + 1) * rc)
            y = jnp.dot(a_ref[rows, :], w_blk[...], preferred_element_type=F32)
            if not first:
                y = o_ref[rows, :] + y
            if last:
                for r in range(0, rc, NORM_ROWS):
                    finish(c * rc + r, y[r:r + NORM_ROWS])
            else:
                o_ref[rows, :] = y

    for a_ref, (k0, k1) in zip(a_refs, splits):
        for first, last in sorted({(kk == 0, kk == nk - 1) for kk in range(k0, k1)}):
            ks = [kk for kk in range(k0, k1) if (kk == 0, kk == nk - 1) == (first, last)]
            pl.when((k >= ks[0]) & (k <= ks[-1]))(functools.partial(step, a_ref, first, last))


def _matmul_norm_res(a_list, w, layer, g, resid, g_next, *, bm, bk, resident, name):
    m = a_list[0].shape[0]
    n = w.shape[2]
    splits, k0 = [], 0
    for a in a_list:
        splits.append((k0, k0 + a.shape[1] // bk))
        k0 = splits[-1][1]
    nk = k0
    rc = min(bm, 256)
    emit_next = g_next is not None
    weights = "bf16" if w.dtype == BF16 else ("resident" if resident else "stream")

    def a_spec(k0, k1):
        return pl.BlockSpec((bm, bk), lambda i, k: (i, jnp.clip(k - k0, 0, k1 - k0 - 1)))

    if weights == "resident":
        w_spec = pl.BlockSpec((None, bk, n), lambda i, k: (layer, jnp.where(i == 0, k, nk - 1), 0))
        scratch = [pltpu.VMEM((nk, bk, n), BF16)]
    else:
        w_spec = pl.BlockSpec((None, bk, n), lambda i, k: (layer, k, 0))
        scratch = [pltpu.VMEM((bk, n), BF16)] if weights == "stream" else []
    row_vec = pl.BlockSpec((1, n), lambda i, k: (0, 0))
    tile = pl.BlockSpec((bm, n), lambda i, k: (i, 0))
    out = pl.pallas_call(
        functools.partial(_matmul_norm_res_kernel, splits=tuple(splits), nk=nk, bm=bm, rc=rc,
                          emit_next=emit_next, weights=weights),
        out_shape=((jax.ShapeDtypeStruct((m, n), F32), jax.ShapeDtypeStruct((m, n), BF16)) if emit_next
                   else jax.ShapeDtypeStruct((m, n), F32)),
        grid=(m // bm, nk),
        in_specs=[a_spec(*sp) for sp in splits] + [w_spec, row_vec, tile] + ([row_vec] if emit_next else []),
        out_specs=(tile, tile) if emit_next else tile,
        scratch_shapes=scratch,
        compiler_params=_params("arbitrary", "arbitrary"),
        name=name,
    )(*a_list, w, g.reshape(1, n), resid, *([g_next.reshape(1, n)] if emit_next else []))
    return out if emit_next else (out, None)


def _down_kernel(*refs, bm, emit_next):
    if emit_next:
        a_ref, w_hbm_ref, g_ref, r_ref, gn_ref, o_ref, xn_ref, w_ref, sem = refs
    else:
        a_ref, w_hbm_ref, g_ref, r_ref, o_ref, w_ref, sem = refs

    @pl.when(pl.program_id(0) == 0)
    def _():
        copy = pltpu.make_async_copy(w_hbm_ref, w_ref, sem)
        copy.start()
        copy.wait()

    y = jnp.dot(a_ref[...], w_ref[...], preferred_element_type=F32)
    for r in range(0, bm, NORM_ROWS):
        rows = slice(r, r + NORM_ROWS)
        yy = y[rows]
        ms = jnp.mean(yy * yy, axis=-1, keepdims=True)
        h = r_ref[rows, :] + (yy * lax.rsqrt(ms + EPS)) * g_ref[...]
        o_ref[rows, :] = h
        if emit_next:
            ms2 = jnp.mean(h * h, axis=-1, keepdims=True)
            xn_ref[rows, :] = ((h * lax.rsqrt(ms2 + EPS)) * gn_ref[...]).astype(BF16)


def _down_norm_res(a, w, g, resid, g_next, *, bm, name):
    m, kdim = a.shape
    n = w.shape[1]
    emit_next = g_next is not None
    row_vec = pl.BlockSpec((1, n), lambda i: (0, 0))
    tile = pl.BlockSpec((bm, n), lambda i: (i, 0))
    out = pl.pallas_call(
        functools.partial(_down_kernel, bm=bm, emit_next=emit_next),
        out_shape=((jax.ShapeDtypeStruct((m, n), F32), jax.ShapeDtypeStruct((m, n), BF16)) if emit_next
                   else jax.ShapeDtypeStruct((m, n), F32)),
        grid=(m // bm,),
        in_specs=[pl.BlockSpec((bm, kdim), lambda i: (i, 0)), pl.BlockSpec(memory_space=pl.ANY), row_vec, tile]
        + ([row_vec] if emit_next else []),
        out_specs=(tile, tile) if emit_next else tile,
        scratch_shapes=[pltpu.VMEM((kdim, n), BF16), pltpu.SemaphoreType.DMA(())],
        compiler_params=_params("arbitrary"),
        name=name,
    )(a, w, g.reshape(1, n), resid, *([g_next.reshape(1, n)] if emit_next else []))
    return out if emit_next else (out, None)


def _ffn_up_kernel(*refs, groups, rows_per_group, rc, tiles_per_seq, cast_down):
    if cast_down:
        (xn_ref, wa_ref, wg_ref, cwa_ref, cwg_ref, sa_ref, sg_ref, wd_ref,
         act_ref, la_ref, lg_ref, wdb_ref, wb_ref, carry_ref) = refs
    else:
        (xn_ref, wa_ref, wg_ref, cwa_ref, cwg_ref, sa_ref, sg_ref,
         act_ref, la_ref, lg_ref, wb_ref, carry_ref) = refs
    i = pl.program_id(1)
    bm = groups * rows_per_group
    bn = act_ref.shape[1]
    nsub = bn // SUB_COLS
    subs = [slice(u * SUB_COLS, (u + 1) * SUB_COLS) for u in range(nsub)]

    @pl.when(i == 0)
    def _():
        for u in range(nsub):
            wb_ref[u, :, :SUB_COLS] = wa_ref[:, subs[u]].astype(BF16)
            wb_ref[u, :, SUB_COLS:] = wg_ref[:, subs[u]].astype(BF16)
        if cast_down:
            wdb_ref[...] = wd_ref[...].astype(BF16)

    keep = SUBLANES - (CONV_W - 1)

    @pl.when(lax.rem(i, tiles_per_seq) == 0)
    def _():
        carry_ref[...] = jnp.zeros(carry_ref.shape, F32)
        carry_ref[0, :, keep:, :] = sa_ref[...]
        carry_ref[1, :, keep:, :] = sg_ref[...]

    cws = [jnp.concatenate([cwa_ref[:, subs[u]], cwg_ref[:, subs[u]]], axis=1) for u in range(nsub)]
    piece = min(EPI_ROWS, rows_per_group)
    prev = [None] * nsub
    after = [None] * nsub
    for c in range(bm // rc):
        xc = xn_ref[c * rc:(c + 1) * rc, :]
        for u in range(nsub):
            h = jnp.dot(xc, wb_ref[u], preferred_element_type=F32)
            for q in range(rc // piece):
                row = c * rc + q * piece
                grp = row // rows_per_group
                hcur = h[q * piece:(q + 1) * piece]
                if row % rows_per_group == 0:
                    prev8 = jnp.concatenate([carry_ref[0, grp, :, subs[u]], carry_ref[1, grp, :, subs[u]]], axis=1)
                else:
                    prev8 = prev[u]
                hext = jnp.concatenate([prev8, hcur], axis=0)
                s1 = pltpu.roll(hext, 1, 0)[SUBLANES:]
                s2 = pltpu.roll(hext, 2, 0)[SUBLANES:]
                cw = cws[u] if after[u] is None else cws[u] + jnp.concatenate([after[u]] * 2, axis=1)
                conv = cw[3:4] + s2 * cw[0:1] + s1 * cw[1:2] + hcur * cw[2:3]
                act = jax.nn.gelu(conv[:, SUB_COLS:], approximate=True) * conv[:, :SUB_COLS]
                act_ref[row:row + piece, subs[u]] = act.astype(BF16)
                after[u] = _zero_after(act[piece - SUBLANES:])
                prev[u] = hcur[piece - SUBLANES:]
                if (row + piece) % rows_per_group == 0:
                    la_ref[grp, :, subs[u]] = prev[u][keep:, :SUB_COLS]
                    lg_ref[grp, :, subs[u]] = prev[u][keep:, SUB_COLS:]
                    carry_ref[0, grp, :, subs[u]] = prev[u][:, :SUB_COLS]
                    carry_ref[1, grp, :, subs[u]] = prev[u][:, SUB_COLS:]


def _ffn_up(xn, w_up, layer, conv_w, conv_b, state, w_down, *, groups, rows_per_group, bn, name):
    m, k = xn.shape
    nseq = state.shape[0]
    bm = groups * rows_per_group
    ni = m // bm
    tiles_per_seq = ni * groups // nseq
    nj = D_FF // bn
    rc = min(bm, 256)
    cw = jnp.concatenate([conv_w, conv_b[None], jnp.zeros((SUBLANES - CONV_W - 1, 2 * D_FF), F32)], axis=0)
    cast_down = w_down is not None
    d_out = w_down.shape[2] if cast_down else 0
    out_shape = [jax.ShapeDtypeStruct((m, D_FF), BF16),
                 jax.ShapeDtypeStruct((nseq, CONV_W - 1, D_FF), F32),
                 jax.ShapeDtypeStruct((nseq, CONV_W - 1, D_FF), F32)]
    in_specs = [
        pl.BlockSpec((bm, k), lambda j, i: (i, 0)),
        pl.BlockSpec((None, k, bn), lambda j, i: (layer, 0, j)),
        pl.BlockSpec((None, k, bn), lambda j, i: (layer, 0, nj + j)),
        pl.BlockSpec((SUBLANES, bn), lambda j, i: (0, j)),
        pl.BlockSpec((SUBLANES, bn), lambda j, i: (0, nj + j)),
        pl.BlockSpec((groups, CONV_W - 1, bn), lambda j, i: (i // tiles_per_seq, 0, j)),
        pl.BlockSpec((groups, CONV_W - 1, bn), lambda j, i: (i // tiles_per_seq, 0, nj + j)),
    ]
    out_specs = [
        pl.BlockSpec((bm, bn), lambda j, i: (i, j)),
        pl.BlockSpec((groups, CONV_W - 1, bn), lambda j, i: (i // tiles_per_seq, 0, j)),
        pl.BlockSpec((groups, CONV_W - 1, bn), lambda j, i: (i // tiles_per_seq, 0, j)),
    ]
    operands = [xn, w_up, w_up, cw, cw, state, state]
    if cast_down:
        in_specs.append(pl.BlockSpec((None, bn, d_out), lambda j, i: (layer, j, 0)))
        out_specs.append(pl.BlockSpec((None, bn, d_out), lambda j, i: (0, j, 0)))
        out_shape.append(jax.ShapeDtypeStruct((1, D_FF, d_out), BF16))
        operands.append(w_down)
    outs = pl.pallas_call(
        functools.partial(_ffn_up_kernel, groups=groups, rows_per_group=rows_per_group, rc=rc,
                          tiles_per_seq=tiles_per_seq, cast_down=cast_down),
        out_shape=tuple(out_shape),
        grid=(nj, ni),
        in_specs=in_specs,
        out_specs=tuple(out_specs),
        scratch_shapes=[pltpu.VMEM((bn // SUB_COLS, k, 2 * SUB_COLS), BF16),
                        pltpu.VMEM((2, groups, SUBLANES, bn), F32)],
        compiler_params=_params("arbitrary", "arbitrary"),
        name=name,
    )(*operands)
    act, la, lg = outs[:3]
    return act, jnp.concatenate([la, lg], axis=-1), (outs[3] if cast_down else None)


def _attn_head(q, k, v, sink_ref, h, bias):
    nq = q.shape[0]
    kh = k[:, h * HEAD_DIM:(h + 1) * HEAD_DIM].astype(BF16)
    vh = v[:, h * HEAD_DIM:(h + 1) * HEAD_DIM].astype(BF16)
    qs, sk = [], []
    for gq in range(Q_PER_KV):
        c0 = (h * Q_PER_KV + gq) * HEAD_DIM
        qs.append(q[:, c0:c0 + HEAD_DIM])
        sk.append(jnp.full((nq, 1), sink_ref[h * Q_PER_KV + gq], F32))
    qh = (jnp.concatenate(qs, axis=0) * (HEAD_DIM ** -0.5)).astype(BF16)
    sk = jnp.concatenate(sk, axis=0)
    s = lax.dot_general(qh, kh, (((1,), (1,)), ((), ())), preferred_element_type=F32)
    if bias is not None:
        s = s + bias
    mx = jnp.maximum(jnp.max(s, axis=-1, keepdims=True), sk)
    p = jnp.exp(s - mx)
    den = jnp.sum(p, axis=-1, keepdims=True) + jnp.exp(sk - mx)
    o = jnp.dot((p / den).astype(BF16), vh, preferred_element_type=F32)
    return jnp.concatenate([o[gq * nq:(gq + 1) * nq] for gq in range(Q_PER_KV)], axis=1)


ATTN_UNIT = 2 * CHUNK


def _attn_prompt_kernel(sink_ref, q_ref, kv_ref, kvp_ref, o_ref, bias_ref, *, tq):
    i = pl.program_id(1)
    nk = ATTN_UNIT + WINDOW
    cols = Q_PER_KV * ATTN_UNIT
    kvw = N_KV_HEADS * HEAD_DIM
    @pl.when(i == 0)
    def _():
        r_k = lax.broadcasted_iota(jnp.int32, (nk, cols), 0)
        r_q = lax.broadcasted_iota(jnp.int32, (nk, cols), 1)
        lo = (r_q & (ATTN_UNIT - 1)) & ~(CHUNK - 1)
        band = (r_k >= lo) & (r_k < lo + WINDOW + CHUNK)
        bias_ref[1] = jnp.where(band, 0.0, -jnp.inf)
        bias_ref[0] = jnp.where(band & (r_k >= WINDOW), 0.0, -jnp.inf)

    first_unit_bias = jnp.where(i == 0, 0, 1)

    kv_all = jnp.concatenate([kvp_ref[0], kv_ref[0]], axis=0)
    k_all = kv_all[:, :kvw].astype(BF16)
    vt_all = jnp.transpose(kv_all[:, kvw:]).astype(BF16)
    for u in range(tq // ATTN_UNIT):
        rs = slice(u * ATTN_UNIT, (u + 1) * ATTN_UNIT)
        keys = slice(u * ATTN_UNIT, u * ATTN_UNIT + nk)
        k_u = k_all[keys]
        qt = jnp.transpose(q_ref[0, rs, :] * (HEAD_DIM ** -0.5)).astype(BF16)
        bias = bias_ref[first_unit_bias] if u == 0 else bias_ref[1]
        pieces = []
        for h in range(N_KV_HEADS):
            heads = [h * Q_PER_KV + gq for gq in range(Q_PER_KV)]
            qt_h = jnp.concatenate([qt[n * HEAD_DIM:(n + 1) * HEAD_DIM] for n in heads], axis=1)
            parts = [qt_h]
            if h > 0:
                parts.insert(0, jnp.zeros((h * HEAD_DIM, cols), BF16))
            if h < N_KV_HEADS - 1:
                parts.append(jnp.zeros(((N_KV_HEADS - 1 - h) * HEAD_DIM, cols), BF16))
            st = jnp.dot(k_u, jnp.concatenate(parts, axis=0), preferred_element_type=F32) + bias
            sk = jnp.concatenate([jnp.full((1, ATTN_UNIT), sink_ref[n], F32) for n in heads], axis=1)
            mx = jnp.maximum(jnp.max(st, axis=0, keepdims=True), sk)
            p = jnp.exp(st - mx)
            den = jnp.sum(p, axis=0, keepdims=True) + jnp.exp(sk - mx)
            ot = jnp.dot(vt_all[h * HEAD_DIM:(h + 1) * HEAD_DIM, keys], (p / den).astype(BF16),
                         preferred_element_type=F32)
            pieces += [ot[:, gq * ATTN_UNIT:(gq + 1) * ATTN_UNIT] for gq in range(Q_PER_KV)]
        o_ref[0, rs, :] = jnp.transpose(jnp.concatenate(pieces, axis=0)).astype(BF16)


def _attn_prompt(z, sinks, *, tq):
    b, t, _ = z.shape
    per = tq // WINDOW
    return pl.pallas_call(
        functools.partial(_attn_prompt_kernel, tq=tq),
        out_shape=jax.ShapeDtypeStruct((b, t, N_Q_HEADS * HEAD_DIM), BF16),
        grid=(b, t // tq),
        in_specs=[
            pl.BlockSpec(memory_space=pltpu.SMEM),
            pl.BlockSpec((1, tq, 1024), lambda bb, i: (bb, i, 0)),
            pl.BlockSpec((1, tq, COL_BLK), lambda bb, i: (bb, i, KV_BLK)),
            pl.BlockSpec((1, WINDOW, COL_BLK), lambda bb, i: (bb, jnp.maximum(i * per - 1, 0), KV_BLK)),
        ],
        out_specs=pl.BlockSpec((1, tq, 1024), lambda bb, i: (bb, i, 0)),
        scratch_shapes=[pltpu.VMEM((2, ATTN_UNIT + WINDOW, Q_PER_KV * ATTN_UNIT), F32)],
        compiler_params=_params("arbitrary", "arbitrary"),
        name="attn_prompt",
    )(sinks, z, z, z)


def _attn_sample_kernel(sink_ref, q_ref, kv_ref, ck_ref, cv_ref, o_ref):
    for s in range(q_ref.shape[0]):
        kv = kv_ref[s]
        k = jnp.concatenate([ck_ref[s], kv[:, :256]], axis=0)
        v = jnp.concatenate([cv_ref[s], kv[:, 256:]], axis=0)
        q = q_ref[s]
        for h in range(N_KV_HEADS):
            o_ref[s, :, h * 256:(h + 1) * 256] = _attn_head(q, k, v, sink_ref, h, None).astype(BF16)


def _attn_sample(z, cache_k, cache_v, sinks):
    b, t, _ = z.shape
    nc = cache_k.shape[1]
    nb = ATTN_SAMPLE_SEQS_PER_STEP
    return pl.pallas_call(
        _attn_sample_kernel,
        out_shape=jax.ShapeDtypeStruct((b, t, N_Q_HEADS * HEAD_DIM), BF16),
        grid=(b // nb,),
        in_specs=[
            pl.BlockSpec(memory_space=pltpu.SMEM),
            pl.BlockSpec((nb, t, 1024), lambda bb: (bb, 0, 0)),
            pl.BlockSpec((nb, t, COL_BLK), lambda bb: (bb, 0, KV_BLK)),
            pl.BlockSpec((nb, nc, 256), lambda bb: (bb, 0, 0)),
            pl.BlockSpec((nb, nc, 256), lambda bb: (bb, 0, 0)),
        ],
        out_specs=pl.BlockSpec((nb, t, 1024), lambda bb: (bb, 0, 0)),
        compiler_params=_params("arbitrary"),
        name="attn_sample",
    )(sinks, z, z, cache_k.reshape(b, nc, 256), cache_v.reshape(b, nc, 256))


def _retention_kernel(rq_ref, rk_ref, rv_ref, rg_ref, cs_ref, sn_ref, intra_ref, qd_ref, kd_ref, cd_ref,
                      ng_ref, s0_ref, o_ref, st_ref):
    c = pl.program_id(2)

    @pl.when(c == 0)
    def _():
        st_ref[...] = s0_ref[...]

    cs = cs_ref[...]
    sn = sn_ref[...]
    for s, hh in [(s, hh) for s in range(rq_ref.shape[0]) for hh in range(RET_HEADS_PER_BLK)]:
        lanes = slice(hh * RET_DK, (hh + 1) * RET_DK)
        q = rq_ref[s, :, lanes]
        k = rk_ref[s, :, lanes]
        qr = q * cs + pltpu.roll(q, RET_DK // 2, 1) * sn
        kr = (k * cs + pltpu.roll(k, RET_DK // 2, 1) * sn) * (RET_DK ** -0.5)
        qb = qr.astype(BF16)
        kb = kr.astype(BF16)
        vb = rv_ref[s, :, lanes].astype(BF16)
        sc = lax.dot_general(qb, kb, (((1,), (1,)), ((), ())), preferred_element_type=F32) * intra_ref[hh]
        inner = jnp.dot(sc.astype(BF16), vb, preferred_element_type=F32)
        state = st_ref[s, hh]
        cross = jnp.dot(qb, state.astype(BF16), preferred_element_type=F32) * qd_ref[hh]
        kdt = jnp.transpose(kr * kd_ref[hh]).astype(BF16)
        st_ref[s, hh] = cd_ref[hh, 0:1, :] * state + jnp.dot(kdt, vb, preferred_element_type=F32)
        r = inner + cross
        mu = jnp.mean(r, axis=-1, keepdims=True)
        yc = r - mu
        yn = yc * lax.rsqrt(jnp.mean(yc * yc, axis=-1, keepdims=True) + EPS)
        o_ref[s, :, lanes] = ((yn * ng_ref[:, lanes]) * jax.nn.silu(rg_ref[s, :, lanes])).astype(BF16)


def _ret_log_gamma():
    return jnp.log1p(-jnp.exp2(-5.0 - jnp.arange(N_RET_HEADS, dtype=F32)))


def _retention(z, pos, state0, norm_g, *, chunk, nbatch):
    b, t, _ = z.shape
    log_g = _ret_log_gamma()
    idx = jnp.arange(chunk, dtype=F32)
    diff = idx[:, None] - idx[None, :]
    intra = jnp.where(diff[None] >= 0.0,
                      jnp.exp(log_g[:, None, None] * jnp.maximum(diff, 0.0)[None]), 0.0)
    ones = jnp.ones((1, 1, RET_DV), F32)
    q_decay = jnp.exp(log_g[:, None] * (idx[None, :] + 1.0))[:, :, None] * ones
    k_decay = jnp.exp(log_g[:, None] * (chunk - 1.0 - idx)[None, :])[:, :, None] * ones
    c_decay = jnp.exp(log_g * chunk)[:, None, None] * jnp.ones((1, SUBLANES, RET_DV), F32)
    half = RET_DK // 2
    freq = 1.0 / (ROPE_BASE ** (jnp.arange(half, dtype=F32) / half))
    ang = pos.astype(F32)[:, None] * freq[None, :]
    cos, sin = jnp.cos(ang), jnp.sin(ang)
    cs = jnp.concatenate([cos, cos], axis=-1)
    sn = jnp.concatenate([-sin, sin], axis=-1)
    nhb = N_RET_HEADS // RET_HEADS_PER_BLK
    hb = RET_HEADS_PER_BLK

    def zspec(blk):
        return pl.BlockSpec((nbatch, chunk, COL_BLK), lambda bb, g, c: (bb, c, blk + g))

    def tab(rows):
        return pl.BlockSpec((hb, rows, RET_DV), lambda bb, g, c: (g, 0, 0))

    state_spec = pl.BlockSpec((nbatch, hb, RET_DK, RET_DV), lambda bb, g, c: (bb, g, 0, 0))
    out, st = pl.pallas_call(
        _retention_kernel,
        out_shape=(jax.ShapeDtypeStruct((b, t, N_RET_HEADS * RET_DV), BF16),
                   jax.ShapeDtypeStruct((b, N_RET_HEADS, RET_DK, RET_DV), F32)),
        grid=(b // nbatch, nhb, t // chunk),
        in_specs=[
            zspec(RQ_BLK), zspec(RK_BLK), zspec(RV_BLK), zspec(RG_BLK),
            pl.BlockSpec((chunk, RET_DK), lambda bb, g, c: (c, 0)),
            pl.BlockSpec((chunk, RET_DK), lambda bb, g, c: (c, 0)),
            pl.BlockSpec((hb, chunk, chunk), lambda bb, g, c: (g, 0, 0)),
            tab(chunk), tab(chunk), tab(SUBLANES),
            pl.BlockSpec((1, COL_BLK), lambda bb, g, c: (0, g)),
            state_spec,
        ],
        out_specs=(pl.BlockSpec((nbatch, chunk, COL_BLK), lambda bb, g, c: (bb, c, g)), state_spec),
        compiler_params=_params("arbitrary", "arbitrary", "arbitrary"),
        name="retention",
    )(z, z, z, z, cs, sn, intra, q_decay, k_decay, c_decay, norm_g.reshape(1, -1), state0)
    return out, st


def _gate_kernel(u_ref, v_ref, lg_ref, lb_ref, ws_ref, bs_ref, y_ref, *maybe_vn_ref, rows, span):
    ri = lax.broadcasted_iota(jnp.int32, (span, span), 0)
    ci = lax.broadcasted_iota(jnp.int32, (span, span), 1)
    wt = [jnp.where(ri >= ci, ws_ref[g], 0.0).astype(BF16) for g in range(GM_GROUPS)]
    bs = bs_ref[...]
    lg = lg_ref[...]
    lb = lb_ref[...]
    for s, c in [(s, c) for s in range(u_ref.shape[0]) for c in range(rows // span)]:
        rs = slice(c * span, (c + 1) * span)
        v = v_ref[s, rs, :]
        mu = jnp.mean(v, axis=-1, keepdims=True)
        xc = v - mu
        vn = (xc * lax.rsqrt(jnp.mean(xc * xc, axis=-1, keepdims=True) + EPS)) * lg + lb
        for vn_ref in maybe_vn_ref:
            vn_ref[s, rs, :] = vn
        vb = vn.astype(BF16)
        for g in range(GM_GROUPS):
            cols = slice(g * GM_GROUP_DIM, (g + 1) * GM_GROUP_DIM)
            mixed = jnp.dot(wt[g], vb[:, cols], preferred_element_type=F32) + bs[:, g:g + 1]
            y_ref[s, rs, cols] = (u_ref[s, rs, cols] * mixed).astype(BF16)


def _spatial_gate(uv, ln_g, ln_b, ws, bs, *, rows, nbatch, emit_vn):
    b, t, _ = uv.shape
    span = min(t, GM_CHUNK)
    ws_l = ws[:, :span, :span]
    bs_t = jnp.transpose(bs[:, :span])
    tile = pl.BlockSpec((nbatch, rows, D_MODEL), lambda bb, i: (bb, i, 0))
    outs = pl.pallas_call(
        functools.partial(_gate_kernel, rows=rows, span=span),
        out_shape=(jax.ShapeDtypeStruct((b, t, D_MODEL), BF16),)
        + ((jax.ShapeDtypeStruct((b, t, D_MODEL), F32),) if emit_vn else ()),
        grid=(b // nbatch, t // rows),
        in_specs=[
            pl.BlockSpec((nbatch, rows, D_MODEL), lambda bb, i: (bb, i, 0)),
            pl.BlockSpec((nbatch, rows, D_MODEL), lambda bb, i: (bb, i, 1)),
            pl.BlockSpec((1, D_MODEL), lambda bb, i: (0, 0)),
            pl.BlockSpec((1, D_MODEL), lambda bb, i: (0, 0)),
            pl.BlockSpec((GM_GROUPS, span, span), lambda bb, i: (0, 0, 0)),
            pl.BlockSpec((span, GM_GROUPS), lambda bb, i: (0, 0)),
        ],
        out_specs=(tile, tile) if emit_vn else (tile,),
        compiler_params=_params("arbitrary", "arbitrary"),
        name="spatial_gate",
    )(uv, uv, ln_g.reshape(1, -1), ln_b.reshape(1, -1), ws_l, bs_t)
    return outs if emit_vn else (outs[0], None)


ROW_TILE = 1024
FFN_ROW_TILE = 2048
DOWN_ROW_TILE = 256
OUT_ROW_TILE = 512
OUT_K_BLK = 1024
IN_EVEN_BLK = 1408
IN_ODD_BLK = 1024
FFN_BLK = 512
ATTN_ROW_TILE = 1024


def _conv_ffn(hp, xp, hs, xs, g3, g_next, w_up, layer, conv_w, conv_b, w_down, state_p, state_s):
    nseq_s = state_s.shape[0]
    act_p, c_p, wdb = _ffn_up(xp, w_up, layer, conv_w, conv_b, state_p, w_down, groups=1,
                              rows_per_group=FFN_ROW_TILE, bn=FFN_BLK, name="ffn_up_p")
    act_s, c_s, _ = _ffn_up(xs, w_up, layer, conv_w, conv_b, state_s, None, groups=nseq_s,
                            rows_per_group=xs.shape[0] // nseq_s, bn=FFN_BLK, name="ffn_up_s")
    wdb = wdb.reshape(wdb.shape[1:])
    hp, xp = _down_norm_res(act_p, wdb, g3, hp, g_next, bm=DOWN_ROW_TILE, name="ffn_down_p")
    hs, xs = _down_norm_res(act_s, wdb, g3, hs, g_next, bm=DOWN_ROW_TILE, name="ffn_down_s")
    return hp, xp, c_p, hs, xs, c_s


def kernel(x_prompt, x_sample, cache_swa_k, cache_swa_v, state_ret, state_ffn_conv, norm_g, w_in_even,
           w_out_even, attn_sinks, ret_norm_g, w_in_odd, w_out_odd, gm_ln_g, gm_ln_b, gm_ws, gm_bs,
           ffn_w_up, ffn_conv_w, ffn_conv_b, ffn_w_down):
    nb, seq, d = x_prompt.shape
    db, dseq, _ = x_sample.shape
    depth = norm_g.shape[0]
    hp = x_prompt.reshape(nb * seq, d)
    hs = x_sample.reshape(db * dseq, d)
    bm_p, bm_s = ROW_TILE, db * dseq
    pos_p = jnp.arange(seq)
    pos_s = PAST_LEN + jnp.arange(dseq)
    kp_l, vp_l, rp_l, cp_l = [], [], [], []
    ks_l, vs_l, rs_l, cs_l, gv_l = [], [], [], [], []
    xp = _rmsnorm(hp, norm_g[0, 0], bm=ROW_TILE, name="norm_in_p")
    xs = _rmsnorm(hs, norm_g[0, 0], bm=bm_s, name="norm_in_s")
    for layer in range(depth):
        g = norm_g[layer]
        g_next = norm_g[layer + 1, 0] if layer + 1 < depth else None
        if layer % 2 == 0:
            e = layer // 2
            zp = _xw(xp, w_in_even, e, bm=bm_p, bn=IN_EVEN_BLK, gelu=False, name="in_even_p")
            zs = _xw(xs, w_in_even, e, bm=bm_s, bn=IN_EVEN_BLK, gelu=False, name="in_even_s")
            zp = zp.reshape(nb, seq, EVEN_IN)
            zs = zs.reshape(db, dseq, EVEN_IN)
            attn_p = _attn_prompt(zp, attn_sinks[e], tq=ATTN_ROW_TILE)
            attn_s = _attn_sample(zs, cache_swa_k[e], cache_swa_v[e], attn_sinks[e])
            ret_p, r_p = _retention(zp, pos_p, jnp.zeros((nb, N_RET_HEADS, RET_DK, RET_DV), F32),
                                    ret_norm_g[e], chunk=RET_CHUNK, nbatch=1)
            ret_s, r_s = _retention(zs, pos_s, state_ret[e].astype(F32), ret_norm_g[e], chunk=dseq,
                                    nbatch=RET_SAMPLE_SEQS_PER_STEP)
            mixed_p = [attn_p.reshape(nb * seq, -1), ret_p.reshape(nb * seq, -1)]
            mixed_s = [attn_s.reshape(db * dseq, -1), ret_s.reshape(db * dseq, -1)]
            hp, xp = _matmul_norm_res(mixed_p, w_out_even, e, g[1], hp, g[2], bm=OUT_ROW_TILE, bk=OUT_K_BLK,
                                      resident=True, name="out_even_p")
            hs, xs = _matmul_norm_res(mixed_s, w_out_even, e, g[1], hs, g[2], bm=bm_s, bk=OUT_K_BLK,
                                      resident=False, name="out_even_s")
            k_new = zs[:, :, 1024:1280].reshape(db, dseq, N_KV_HEADS, HEAD_DIM)
            v_new = zs[:, :, 1280:1536].reshape(db, dseq, N_KV_HEADS, HEAD_DIM)
            n_keep = cache_swa_k.shape[2]
            kp_l.append(zp[:, seq - WINDOW:, 1024:1280].reshape(nb, WINDOW, N_KV_HEADS, HEAD_DIM))
            vp_l.append(zp[:, seq - WINDOW:, 1280:1536].reshape(nb, WINDOW, N_KV_HEADS, HEAD_DIM))
            ks_l.append(jnp.concatenate([cache_swa_k[e], k_new], axis=1)[:, -n_keep:])
            vs_l.append(jnp.concatenate([cache_swa_v[e], v_new], axis=1)[:, -n_keep:])
            rp_l.append(r_p)
            rs_l.append(r_s.astype(state_ret.dtype))
        else:
            o = layer // 2
            uvp = _xw(xp, w_in_odd, o, bm=bm_p, bn=IN_ODD_BLK, gelu=True, name="in_odd_p")
            uvs = _xw(xs, w_in_odd, o, bm=bm_s, bn=IN_ODD_BLK, gelu=True, name="in_odd_s")
            yp, _ = _spatial_gate(uvp.reshape(nb, seq, -1), gm_ln_g[o], gm_ln_b[o], gm_ws[o], gm_bs[o], rows=1024,
                                  nbatch=1, emit_vn=False)
            ys, gv = _spatial_gate(uvs.reshape(db, dseq, -1), gm_ln_g[o], gm_ln_b[o], gm_ws[o], gm_bs[o],
                                   rows=dseq, nbatch=GATE_SAMPLE_SEQS_PER_STEP, emit_vn=True)
            hp, xp = _matmul_norm_res([yp.reshape(nb * seq, -1)], w_out_odd, o, g[1], hp, g[2], bm=OUT_ROW_TILE,
                                      bk=OUT_K_BLK, resident=True, name="out_odd_p")
            hs, xs = _matmul_norm_res([ys.reshape(db * dseq, -1)], w_out_odd, o, g[1], hs, g[2], bm=bm_s,
                                      bk=OUT_K_BLK, resident=False, name="out_odd_s")
            gv_l.append(gv)
        zero_state = jnp.zeros((nb, CONV_W - 1, 2 * D_FF), F32)
        hp, xp, c_p, hs, xs, c_s = _conv_ffn(hp, xp, hs, xs, g[3], g_next, ffn_w_up, layer, ffn_conv_w[layer],
                                             ffn_conv_b[layer], ffn_w_down, zero_state, state_ffn_conv[layer])
        cp_l.append(c_p)
        cs_l.append(c_s)
    return (hp.reshape(nb, seq, d), hs.reshape(db, dseq, d),
            jnp.stack(kp_l), jnp.stack(vp_l), jnp.stack(rp_l), jnp.stack(cp_l),
            jnp.stack(ks_l), jnp.stack(vs_l), jnp.stack(rs_l), jnp.stack(cs_l), jnp.stack(gv_l))
```

```python
import functools

import jax
import jax.numpy as jnp
from jax import lax
from jax.experimental import pallas as pl
from jax.experimental.pallas import tpu as pltpu

F32 = jnp.float32
BF16 = jnp.bfloat16

D_MODEL = 2048
CHUNK = 64
HEAD_DIM = 64
N_Q_HEADS = 16
N_KV_HEADS = 4
Q_PER_KV = N_Q_HEADS // N_KV_HEADS
WINDOW = 128
N_RET_HEADS = 8
RET_DK = 128
RET_DV = 128
ROPE_BASE = 10000.0
GM_CHUNK = 128
GM_GROUPS = 8
GM_GROUP_DIM = D_MODEL // GM_GROUPS
D_FF = 5632
CONV_W = 3
EPS = 1e-6
PAST_LEN = 1024
EVEN_IN = 5632

SUBLANES = 8
VMEM_LIMIT_BYTES = 56 * 1024 * 1024

COL_BLK = 512
KV_BLK = 2
RQ_BLK, RK_BLK, RV_BLK, RG_BLK = 3, 5, 7, 9
RET_HEADS_PER_BLK = COL_BLK // RET_DK
RET_CHUNK = 256
SUB_COLS = 256
EPI_ROWS = 16
NORM_ROWS = 64
RET_SAMPLE_SEQS_PER_STEP = 4
GATE_SAMPLE_SEQS_PER_STEP = 8
ATTN_SAMPLE_SEQS_PER_STEP = 1


def _params(*sem):
    return pltpu.CompilerParams(dimension_semantics=sem, vmem_limit_bytes=VMEM_LIMIT_BYTES)


def _rmsnorm_rows(x_ref, g_ref, out_ref, rows, chunk):
    g = g_ref[...]

    def body(c, carry):
        r0 = pl.multiple_of(c * chunk, chunk)
        x = x_ref[pl.ds(r0, chunk), :]
        ms = jnp.mean(x * x, axis=-1, keepdims=True)
        out_ref[pl.ds(r0, chunk), :] = ((x * lax.rsqrt(ms + EPS)) * g).astype(out_ref.dtype)
        return carry

    lax.fori_loop(0, rows // chunk, body, 0)


def _zero_after(x):
    z = pltpu.bitcast(x, jnp.uint32)
    z = lax.shift_right_logical(lax.shift_right_logical(z, jnp.uint32(16)), jnp.uint32(16))
    return pltpu.bitcast(z, F32)


def _rmsnorm_kernel(x_ref, g_ref, o_ref, *, bm):
    _rmsnorm_rows(x_ref, g_ref, o_ref, bm, min(bm, 128))


def _rmsnorm(x, g, *, bm, name):
    m, k = x.shape
    return pl.pallas_call(
        functools.partial(_rmsnorm_kernel, bm=bm),
        out_shape=jax.ShapeDtypeStruct((m, k), BF16),
        grid=(m // bm,),
        in_specs=[pl.BlockSpec((bm, k), lambda i: (i, 0)), pl.BlockSpec((1, k), lambda i: (0, 0))],
        out_specs=pl.BlockSpec((bm, k), lambda i: (i, 0)),
        compiler_params=_params("arbitrary"),
        name=name,
    )(x, g.reshape(1, k))


def _xw_kernel(xn_ref, w_ref, o_ref, wb_ref, *, bm, rc, gelu):
    @pl.when(pl.program_id(1) == 0)
    def _():
        wb_ref[...] = w_ref[...].astype(BF16)

    for c in range(bm // rc):
        y = jnp.dot(xn_ref[c * rc:(c + 1) * rc, :], wb_ref[...], preferred_element_type=F32)
        if gelu:
            y = jax.nn.gelu(y, approximate=True)
        o_ref[c * rc:(c + 1) * rc, :] = y.astype(o_ref.dtype)


def _xw(xn, w, layer, *, bm, bn, gelu, name):
    m, k = xn.shape
    n = w.shape[2]
    rc = min(bm, 256)
    return pl.pallas_call(
        functools.partial(_xw_kernel, bm=bm, rc=rc, gelu=gelu),
        out_shape=jax.ShapeDtypeStruct((m, n), F32),
        grid=(n // bn, m // bm),
        in_specs=[
            pl.BlockSpec((bm, k), lambda j, i: (i, 0)),
            pl.BlockSpec((None, k, bn), lambda j, i: (layer, 0, j)),
        ],
        out_specs=pl.BlockSpec((bm, bn), lambda j, i: (i, j)),
        scratch_shapes=[pltpu.VMEM((k, bn), BF16)],
        compiler_params=_params("arbitrary", "arbitrary"),
        name=name,
    )(xn, w)


def _matmul_norm_res_kernel(*refs, splits, nk, bm, rc, emit_next, weights):
    na = len(splits)
    a_refs = refs[:na]
    rest = list(refs[na:])
    wb_ref = rest.pop() if weights != "bf16" else None
    if emit_next:
        w_ref, g_ref, r_ref, gn_ref, o_ref, xn_ref = rest
    else:
        w_ref, g_ref, r_ref, o_ref = rest
    i = pl.program_id(0)
    k = pl.program_id(1)
    if weights == "stream":
        wb_ref[...] = w_ref[...].astype(BF16)
        w_blk = wb_ref
    elif weights == "resident":
        @pl.when(i == 0)
        def _():
            wb_ref[k] = w_ref[...].astype(BF16)

        w_blk = wb_ref.at[k]
    else:
        w_blk = w_ref

    def finish(r0, y):
        rows = slice(r0, r0 + NORM_ROWS)
        ms = jnp.mean(y * y, axis=-1, keepdims=True)
        h = r_ref[rows, :] + (y * lax.rsqrt(ms + EPS)) * g_ref[...]
        o_ref[rows, :] = h
        if emit_next:
            ms2 = jnp.mean(h * h, axis=-1, keepdims=True)
            xn_ref[rows, :] = ((h * lax.rsqrt(ms2 + EPS)) * gn_ref[...]).astype(BF16)

    def step(a_ref, first, last):
        for c in range(bm // rc):
            rows = slice(c * rc, (c + 1) * rc)
            y = jnp.dot(a_ref[rows, :], w_blk[...], preferred_element_type=F32)
            if not first:
                y = o_ref[rows, :] + y
            if last:
                for r in range(0, rc, NORM_ROWS):
                    finish(c * rc + r, y[r:r + NORM_ROWS])
            else:
                o_ref[rows, :] = y

    for a_ref, (k0, k1) in zip(a_refs, splits):
        for first, last in sorted({(kk == 0, kk == nk - 1) for kk in range(k0, k1)}):
            ks = [kk for kk in range(k0, k1) if (kk == 0, kk == nk - 1) == (first, last)]
            pl.when((k >= ks[0]) & (k <= ks[-1]))(functools.partial(step, a_ref, first, last))


def _matmul_norm_res(a_list, w, layer, g, resid, g_next, *, bm, bk, resident, name):
    m = a_list[0].shape[0]
    n = w.shape[2]
    splits, k0 = [], 0
    for a in a_list:
        splits.append((k0, k0 + a.shape[1] // bk))
        k0 = splits[-1][1]
    nk = k0
    rc = min(bm, 256)
    emit_next = g_next is not None
    weights = "bf16" if w.dtype == BF16 else ("resident" if resident else "stream")

    def a_spec(k0, k1):
        return pl.BlockSpec((bm, bk), lambda i, k: (i, jnp.clip(k - k0, 0, k1 - k0 - 1)))

    if weights == "resident":
        w_spec = pl.BlockSpec((None, bk, n), lambda i, k: (layer, jnp.where(i == 0, k, nk - 1), 0))
        scratch = [pltpu.VMEM((nk, bk, n), BF16)]
    else:
        w_spec = pl.BlockSpec((None, bk, n), lambda i, k: (layer, k, 0))
        scratch = [pltpu.VMEM((bk, n), BF16)] if weights == "stream" else []
    row_vec = pl.BlockSpec((1, n), lambda i, k: (0, 0))
    tile = pl.BlockSpec((bm, n), lambda i, k: (i, 0))
    out = pl.pallas_call(
        functools.partial(_matmul_norm_res_kernel, splits=tuple(splits), nk=nk, bm=bm, rc=rc,
                          emit_next=emit_next, weights=weights),
        out_shape=((jax.ShapeDtypeStruct((m, n), F32), jax.ShapeDtypeStruct((m, n), BF16)) if emit_next
                   else jax.ShapeDtypeStruct((m, n), F32)),
        grid=(m // bm, nk),
        in_specs=[a_spec(*sp) for sp in splits] + [w_spec, row_vec, tile] + ([row_vec] if emit_next else []),
        out_specs=(tile, tile) if emit_next else tile,
        scratch_shapes=scratch,
        compiler_params=_params("arbitrary", "arbitrary"),
        name=name,
    )(*a_list, w, g.reshape(1, n), resid, *([g_next.reshape(1, n)] if emit_next else []))
    return out if emit_next else (out, None)


def _down_kernel(*refs, bm, emit_next):
    if emit_next:
        a_ref, w_hbm_ref, g_ref, r_ref, gn_ref, o_ref, xn_ref, w_ref, sem = refs
    else:
        a_ref, w_hbm_ref, g_ref, r_ref, o_ref, w_ref, sem = refs

    @pl.when(pl.program_id(0) == 0)
    def _():
        copy = pltpu.make_async_copy(w_hbm_ref, w_ref, sem)
        copy.start()
        copy.wait()

    y = jnp.dot(a_ref[...], w_ref[...], preferred_element_type=F32)
    for r in range(0, bm, NORM_ROWS):
        rows = slice(r, r + NORM_ROWS)
        yy = y[rows]
        ms = jnp.mean(yy * yy, axis=-1, keepdims=True)
        h = r_ref[rows, :] + (yy * lax.rsqrt(ms + EPS)) * g_ref[...]
        o_ref[rows, :] = h
        if emit_next:
            ms2 = jnp.mean(h * h, axis=-1, keepdims=True)
            xn_ref[rows, :] = ((h * lax.rsqrt(ms2 + EPS)) * gn_ref[...]).astype(BF16)


def _down_norm_res(a, w, g, resid, g_next, *, bm, name):
    m, kdim = a.shape
    n = w.shape[1]
    emit_next = g_next is not None
    row_vec = pl.BlockSpec((1, n), lambda i: (0, 0))
    tile = pl.BlockSpec((bm, n), lambda i: (i, 0))
    out = pl.pallas_call(
        functools.partial(_down_kernel, bm=bm, emit_next=emit_next),
        out_shape=((jax.ShapeDtypeStruct((m, n), F32), jax.ShapeDtypeStruct((m, n), BF16)) if emit_next
                   else jax.ShapeDtypeStruct((m, n), F32)),
        grid=(m // bm,),
        in_specs=[pl.BlockSpec((bm, kdim), lambda i: (i, 0)), pl.BlockSpec(memory_space=pl.ANY), row_vec, tile]
        + ([row_vec] if emit_next else []),
        out_specs=(tile, tile) if emit_next else tile,
        scratch_shapes=[pltpu.VMEM((kdim, n), BF16), pltpu.SemaphoreType.DMA(())],
        compiler_params=_params("arbitrary"),
        name=name,
    )(a, w, g.reshape(1, n), resid, *([g_next.reshape(1, n)] if emit_next else []))
    return out if emit_next else (out, None)


def _ffn_up_kernel(*refs, groups, rows_per_group, rc, tiles_per_seq, cast_down):
    if cast_down:
        (xn_ref, wa_ref, wg_ref, cwa_ref, cwg_ref, sa_ref, sg_ref, wd_ref,
         act_ref, la_ref, lg_ref, wdb_ref, wb_ref, carry_ref) = refs
    else:
        (xn_ref, wa_ref, wg_ref, cwa_ref, cwg_ref, sa_ref, sg_ref,
         act_ref, la_ref, lg_ref, wb_ref, carry_ref) = refs
    i = pl.program_id(1)
    bm = groups * rows_per_group
    bn = act_ref.shape[1]
    nsub = bn // SUB_COLS
    subs = [slice(u * SUB_COLS, (u + 1) * SUB_COLS) for u in range(nsub)]

    @pl.when(i == 0)
    def _():
        for u in range(nsub):
            wb_ref[u, :, :SUB_COLS] = wa_ref[:, subs[u]].astype(BF16)
            wb_ref[u, :, SUB_COLS:] = wg_ref[:, subs[u]].astype(BF16)
        if cast_down:
            wdb_ref[...] = wd_ref[...].astype(BF16)

    keep = SUBLANES - (CONV_W - 1)

    @pl.when(lax.rem(i, tiles_per_seq) == 0)
    def _():
        carry_ref[...] = jnp.zeros(carry_ref.shape, F32)
        carry_ref[0, :, keep:, :] = sa_ref[...]
        carry_ref[1, :, keep:, :] = sg_ref[...]

    cws = [jnp.concatenate([cwa_ref[:, subs[u]], cwg_ref[:, subs[u]]], axis=1) for u in range(nsub)]
    piece = min(EPI_ROWS, rows_per_group)
    prev = [None] * nsub
    after = [None] * nsub
    for c in range(bm // rc):
        xc = xn_ref[c * rc:(c + 1) * rc, :]
        for u in range(nsub):
            h = jnp.dot(xc, wb_ref[u], preferred_element_type=F32)
            for q in range(rc // piece):
                row = c * rc + q * piece
                grp = row // rows_per_group
                hcur = h[q * piece:(q + 1) * piece]
                if row % rows_per_group == 0:
                    prev8 = jnp.concatenate([carry_ref[0, grp, :, subs[u]], carry_ref[1, grp, :, subs[u]]], axis=1)
                else:
                    prev8 = prev[u]
                hext = jnp.concatenate([prev8, hcur], axis=0)
                s1 = pltpu.roll(hext, 1, 0)[SUBLANES:]
                s2 = pltpu.roll(hext, 2, 0)[SUBLANES:]
                cw = cws[u] if after[u] is None else cws[u] + jnp.concatenate([after[u]] * 2, axis=1)
                conv = cw[3:4] + s2 * cw[0:1] + s1 * cw[1:2] + hcur * cw[2:3]
                act = jax.nn.gelu(conv[:, SUB_COLS:], approximate=True) * conv[:, :SUB_COLS]
                act_ref[row:row + piece, subs[u]] = act.astype(BF16)
                after[u] = _zero_after(act[piece - SUBLANES:])
                prev[u] = hcur[piece - SUBLANES:]
                if (row + piece) % rows_per_group == 0:
                    la_ref[grp, :, subs[u]] = prev[u][keep:, :SUB_COLS]
                    lg_ref[grp, :, subs[u]] = prev[u][keep:, SUB_COLS:]
                    carry_ref[0, grp, :, subs[u]] = prev[u][:, :SUB_COLS]
                    carry_ref[1, grp, :, subs[u]] = prev[u][:, SUB_COLS:]


def _ffn_up(xn, w_up, layer, conv_w, conv_b, state, w_down, *, groups, rows_per_group, bn, name):
    m, k = xn.shape
    nseq = state.shape[0]
    bm = groups * rows_per_group
    ni = m // bm
    tiles_per_seq = ni * groups // nseq
    nj = D_FF // bn
    rc = min(bm, 256)
    cw = jnp.concatenate([conv_w, conv_b[None], jnp.zeros((SUBLANES - CONV_W - 1, 2 * D_FF), F32)], axis=0)
    cast_down = w_down is not None
    d_out = w_down.shape[2] if cast_down else 0
    out_shape = [jax.ShapeDtypeStruct((m, D_FF), BF16),
                 jax.ShapeDtypeStruct((nseq, CONV_W - 1, D_FF), F32),
                 jax.ShapeDtypeStruct((nseq, CONV_W - 1, D_FF), F32)]
    in_specs = [
        pl.BlockSpec((bm, k), lambda j, i: (i, 0)),
        pl.BlockSpec((None, k, bn), lambda j, i: (layer, 0, j)),
        pl.BlockSpec((None, k, bn), lambda j, i: (layer, 0, nj + j)),
        pl.BlockSpec((SUBLANES, bn), lambda j, i: (0, j)),
        pl.BlockSpec((SUBLANES, bn), lambda j, i: (0, nj + j)),
        pl.BlockSpec((groups, CONV_W - 1, bn), lambda j, i: (i // tiles_per_seq, 0, j)),
        pl.BlockSpec((groups, CONV_W - 1, bn), lambda j, i: (i // tiles_per_seq, 0, nj + j)),
    ]
    out_specs = [
        pl.BlockSpec((bm, bn), lambda j, i: (i, j)),
        pl.BlockSpec((groups, CONV_W - 1, bn), lambda j, i: (i // tiles_per_seq, 0, j)),
        pl.BlockSpec((groups, CONV_W - 1, bn), lambda j, i: (i // tiles_per_seq, 0, j)),
    ]
    operands = [xn, w_up, w_up, cw, cw, state, state]
    if cast_down:
        in_specs.append(pl.BlockSpec((None, bn, d_out), lambda j, i: (layer, j, 0)))
        out_specs.append(pl.BlockSpec((None, bn, d_out), lambda j, i: (0, j, 0)))
        out_shape.append(jax.ShapeDtypeStruct((1, D_FF, d_out), BF16))
        operands.append(w_down)
    outs = pl.pallas_call(
        functools.partial(_ffn_up_kernel, groups=groups, rows_per_group=rows_per_group, rc=rc,
                          tiles_per_seq=tiles_per_seq, cast_down=cast_down),
        out_shape=tuple(out_shape),
        grid=(nj, ni),
        in_specs=in_specs,
        out_specs=tuple(out_specs),
        scratch_shapes=[pltpu.VMEM((bn // SUB_COLS, k, 2 * SUB_COLS), BF16),
                        pltpu.VMEM((2, groups, SUBLANES, bn), F32)],
        compiler_params=_params("arbitrary", "arbitrary"),
        name=name,
    )(*operands)
    act, la, lg = outs[:3]
    return act, jnp.concatenate([la, lg], axis=-1), (outs[3] if cast_down else None)


def _attn_head(q, k, v, sink_ref, h, bias):
    nq = q.shape[0]
    kh = k[:, h * HEAD_DIM:(h + 1) * HEAD_DIM].astype(BF16)
    vh = v[:, h * HEAD_DIM:(h + 1) * HEAD_DIM].astype(BF16)
    qs, sk = [], []
    for gq in range(Q_PER_KV):
        c0 = (h * Q_PER_KV + gq) * HEAD_DIM
        qs.append(q[:, c0:c0 + HEAD_DIM])
        sk.append(jnp.full((nq, 1), sink_ref[h * Q_PER_KV + gq], F32))
    qh = (jnp.concatenate(qs, axis=0) * (HEAD_DIM ** -0.5)).astype(BF16)
    sk = jnp.concatenate(sk, axis=0)
    s = lax.dot_general(qh, kh, (((1,), (1,)), ((), ())), preferred_element_type=F32)
    if bias is not None:
        s = s + bias
    mx = jnp.maximum(jnp.max(s, axis=-1, keepdims=True), sk)
    p = jnp.exp(s - mx)
    den = jnp.sum(p, axis=-1, keepdims=True) + jnp.exp(sk - mx)
    o = jnp.dot((p / den).astype(BF16), vh, preferred_element_type=F32)
    return jnp.concatenate([o[gq * nq:(gq + 1) * nq] for gq in range(Q_PER_KV)], axis=1)


ATTN_UNIT = 2 * CHUNK


def _attn_prompt_kernel(sink_ref, q_ref, kv_ref, kvp_ref, o_ref, bias_ref, *, tq):
    i = pl.program_id(1)
    nk = ATTN_UNIT + WINDOW
    cols = Q_PER_KV * ATTN_UNIT
    kvw = N_KV_HEADS * HEAD_DIM
    @pl.when(i == 0)
    def _():
        r_k = lax.broadcasted_iota(jnp.int32, (nk, cols), 0)
        r_q = lax.broadcasted_iota(jnp.int32, (nk, cols), 1)
        lo = (r_q & (ATTN_UNIT - 1)) & ~(CHUNK - 1)
        band = (r_k >= lo) & (r_k < lo + WINDOW + CHUNK)
        bias_ref[1] = jnp.where(band, 0.0, -jnp.inf)
        bias_ref[0] = jnp.where(band & (r_k >= WINDOW), 0.0, -jnp.inf)

    first_unit_bias = jnp.where(i == 0, 0, 1)

    kv_all = jnp.concatenate([kvp_ref[0], kv_ref[0]], axis=0)
    k_all = kv_all[:, :kvw].astype(BF16)
    vt_all = jnp.transpose(kv_all[:, kvw:]).astype(BF16)
    for u in range(tq // ATTN_UNIT):
        rs = slice(u * ATTN_UNIT, (u + 1) * ATTN_UNIT)
        keys = slice(u * ATTN_UNIT, u * ATTN_UNIT + nk)
        k_u = k_all[keys]
        qt = jnp.transpose(q_ref[0, rs, :] * (HEAD_DIM ** -0.5)).astype(BF16)
        bias = bias_ref[first_unit_bias] if u == 0 else bias_ref[1]
        pieces = []
        for h in range(N_KV_HEADS):
            heads = [h * Q_PER_KV + gq for gq in range(Q_PER_KV)]
            qt_h = jnp.concatenate([qt[n * HEAD_DIM:(n + 1) * HEAD_DIM] for n in heads], axis=1)
            parts = [qt_h]
            if h > 0:
                parts.insert(0, jnp.zeros((h * HEAD_DIM, cols), BF16))
            if h < N_KV_HEADS - 1:
                parts.append(jnp.zeros(((N_KV_HEADS - 1 - h) * HEAD_DIM, cols), BF16))
            st = jnp.dot(k_u, jnp.concatenate(parts, axis=0), preferred_element_type=F32) + bias
            sk = jnp.concatenate([jnp.full((1, ATTN_UNIT), sink_ref[n], F32) for n in heads], axis=1)
            mx = jnp.maximum(jnp.max(st, axis=0, keepdims=True), sk)
            p = jnp.exp(st - mx)
            den = jnp.sum(p, axis=0, keepdims=True) + jnp.exp(sk - mx)
            ot = jnp.dot(vt_all[h * HEAD_DIM:(h + 1) * HEAD_DIM, keys], (p / den).astype(BF16),
                         preferred_element_type=F32)
            pieces += [ot[:, gq * ATTN_UNIT:(gq + 1) * ATTN_UNIT] for gq in range(Q_PER_KV)]
        o_ref[0, rs, :] = jnp.transpose(jnp.concatenate(pieces, axis=0)).astype(BF16)


def _attn_prompt(z, sinks, *, tq):
    b, t, _ = z.shape
    per = tq // WINDOW
    return pl.pallas_call(
        functools.partial(_attn_prompt_kernel, tq=tq),
        out_shape=jax.ShapeDtypeStruct((b, t, N_Q_HEADS * HEAD_DIM), BF16),
        grid=(b, t // tq),
        in_specs=[
            pl.BlockSpec(memory_space=pltpu.SMEM),
            pl.BlockSpec((1, tq, 1024), lambda bb, i: (bb, i, 0)),
            pl.BlockSpec((1, tq, COL_BLK), lambda bb, i: (bb, i, KV_BLK)),
            pl.BlockSpec((1, WINDOW, COL_BLK), lambda bb, i: (bb, jnp.maximum(i * per - 1, 0), KV_BLK)),
        ],
        out_specs=pl.BlockSpec((1, tq, 1024), lambda bb, i: (bb, i, 0)),
        scratch_shapes=[pltpu.VMEM((2, ATTN_UNIT + WINDOW, Q_PER_KV * ATTN_UNIT), F32)],
        compiler_params=_params("arbitrary", "arbitrary"),
        name="attn_prompt",
    )(sinks, z, z, z)


def _attn_sample_kernel(sink_ref, q_ref, kv_ref, ck_ref, cv_ref, o_ref):
    for s in range(q_ref.shape[0]):
        kv = kv_ref[s]
        k = jnp.concatenate([ck_ref[s], kv[:, :256]], axis=0)
        v = jnp.concatenate([cv_ref[s], kv[:, 256:]], axis=0)
        q = q_ref[s]
        for h in range(N_KV_HEADS):
            o_ref[s, :, h * 256:(h + 1) * 256] = _attn_head(q, k, v, sink_ref, h, None).astype(BF16)


def _attn_sample(z, cache_k, cache_v, sinks):
    b, t, _ = z.shape
    nc = cache_k.shape[1]
    nb = ATTN_SAMPLE_SEQS_PER_STEP
    return pl.pallas_call(
        _attn_sample_kernel,
        out_shape=jax.ShapeDtypeStruct((b, t, N_Q_HEADS * HEAD_DIM), BF16),
        grid=(b // nb,),
        in_specs=[
            pl.BlockSpec(memory_space=pltpu.SMEM),
            pl.BlockSpec((nb, t, 1024), lambda bb: (bb, 0, 0)),
            pl.BlockSpec((nb, t, COL_BLK), lambda bb: (bb, 0, KV_BLK)),
            pl.BlockSpec((nb, nc, 256), lambda bb: (bb, 0, 0)),
            pl.BlockSpec((nb, nc, 256), lambda bb: (bb, 0, 0)),
        ],
        out_specs=pl.BlockSpec((nb, t, 1024), lambda bb: (bb, 0, 0)),
        compiler_params=_params("arbitrary"),
        name="attn_sample",
    )(sinks, z, z, cache_k.reshape(b, nc, 256), cache_v.reshape(b, nc, 256))


def _retention_kernel(rq_ref, rk_ref, rv_ref, rg_ref, cs_ref, sn_ref, intra_ref, qd_ref, kd_ref, cd_ref,
                      ng_ref, s0_ref, o_ref, st_ref):
    c = pl.program_id(2)

    @pl.when(c == 0)
    def _():
        st_ref[...] = s0_ref[...]

    cs = cs_ref[...]
    sn = sn_ref[...]
    for s, hh in [(s, hh) for s in range(rq_ref.shape[0]) for hh in range(RET_HEADS_PER_BLK)]:
        lanes = slice(hh * RET_DK, (hh + 1) * RET_DK)
        q = rq_ref[s, :, lanes]
        k = rk_ref[s, :, lanes]
        qr = q * cs + pltpu.roll(q, RET_DK // 2, 1) * sn
        kr = (k * cs + pltpu.roll(k, RET_DK // 2, 1) * sn) * (RET_DK ** -0.5)
        qb = qr.astype(BF16)
        kb = kr.astype(BF16)
        vb = rv_ref[s, :, lanes].astype(BF16)
        sc = lax.dot_general(qb, kb, (((1,), (1,)), ((), ())), preferred_element_type=F32) * intra_ref[hh]
        inner = jnp.dot(sc.astype(BF16), vb, preferred_element_type=F32)
        state = st_ref[s, hh]
        cross = jnp.dot(qb, state.astype(BF16), preferred_element_type=F32) * qd_ref[hh]
        kdt = jnp.transpose(kr * kd_ref[hh]).astype(BF16)
        st_ref[s, hh] = cd_ref[hh, 0:1, :] * state + jnp.dot(kdt, vb, preferred_element_type=F32)
        r = inner + cross
        mu = jnp.mean(r, axis=-1, keepdims=True)
        yc = r - mu
        yn = yc * lax.rsqrt(jnp.mean(yc * yc, axis=-1, keepdims=True) + EPS)
        o_ref[s, :, lanes] = ((yn * ng_ref[:, lanes]) * jax.nn.silu(rg_ref[s, :, lanes])).astype(BF16)


def _ret_log_gamma():
    return jnp.log1p(-jnp.exp2(-5.0 - jnp.arange(N_RET_HEADS, dtype=F32)))


def _retention(z, pos, state0, norm_g, *, chunk, nbatch):
    b, t, _ = z.shape
    log_g = _ret_log_gamma()
    idx = jnp.arange(chunk, dtype=F32)
    diff = idx[:, None] - idx[None, :]
    intra = jnp.where(diff[None] >= 0.0,
                      jnp.exp(log_g[:, None, None] * jnp.maximum(diff, 0.0)[None]), 0.0)
    ones = jnp.ones((1, 1, RET_DV), F32)
    q_decay = jnp.exp(log_g[:, None] * (idx[None, :] + 1.0))[:, :, None] * ones
    k_decay = jnp.exp(log_g[:, None] * (chunk - 1.0 - idx)[None, :])[:, :, None] * ones
    c_decay = jnp.exp(log_g * chunk)[:, None, None] * jnp.ones((1, SUBLANES, RET_DV), F32)
    half = RET_DK // 2
    freq = 1.0 / (ROPE_BASE ** (jnp.arange(half, dtype=F32) / half))
    ang = pos.astype(F32)[:, None] * freq[None, :]
    cos, sin = jnp.cos(ang), jnp.sin(ang)
    cs = jnp.concatenate([cos, cos], axis=-1)
    sn = jnp.concatenate([-sin, sin], axis=-1)
    nhb = N_RET_HEADS // RET_HEADS_PER_BLK
    hb = RET_HEADS_PER_BLK

    def zspec(blk):
        return pl.BlockSpec((nbatch, chunk, COL_BLK), lambda bb, g, c: (bb, c, blk + g))

    def tab(rows):
        return pl.BlockSpec((hb, rows, RET_DV), lambda bb, g, c: (g, 0, 0))

    state_spec = pl.BlockSpec((nbatch, hb, RET_DK, RET_DV), lambda bb, g, c: (bb, g, 0, 0))
    out, st = pl.pallas_call(
        _retention_kernel,
        out_shape=(jax.ShapeDtypeStruct((b, t, N_RET_HEADS * RET_DV), BF16),
                   jax.ShapeDtypeStruct((b, N_RET_HEADS, RET_DK, RET_DV), F32)),
        grid=(b // nbatch, nhb, t // chunk),
        in_specs=[
            zspec(RQ_BLK), zspec(RK_BLK), zspec(RV_BLK), zspec(RG_BLK),
            pl.BlockSpec((chunk, RET_DK), lambda bb, g, c: (c, 0)),
            pl.BlockSpec((chunk, RET_DK), lambda bb, g, c: (c, 0)),
            pl.BlockSpec((hb, chunk, chunk), lambda bb, g, c: (g, 0, 0)),
            tab(chunk), tab(chunk), tab(SUBLANES),
            pl.BlockSpec((1, COL_BLK), lambda bb, g, c: (0, g)),
            state_spec,
        ],
        out_specs=(pl.BlockSpec((nbatch, chunk, COL_BLK), lambda bb, g, c: (bb, c, g)), state_spec),
        compiler_params=_params("arbitrary", "arbitrary", "arbitrary"),
        name="retention",
    )(z, z, z, z, cs, sn, intra, q_decay, k_decay, c_decay, norm_g.reshape(1, -1), state0)
    return out, st


def _gate_kernel(u_ref, v_ref, lg_ref, lb_ref, ws_ref, bs_ref, y_ref, *maybe_vn_ref, rows, span):
    ri = lax.broadcasted_iota(jnp.int32, (span, span), 0)
    ci = lax.broadcasted_iota(jnp.int32, (span, span), 1)
    wt = [jnp.where(ri >= ci, ws_ref[g], 0.0).astype(BF16) for g in range(GM_GROUPS)]
    bs = bs_ref[...]
    lg = lg_ref[...]
    lb = lb_ref[...]
    for s, c in [(s, c) for s in range(u_ref.shape[0]) for c in range(rows // span)]:
        rs = slice(c * span, (c + 1) * span)
        v = v_ref[s, rs, :]
        mu = jnp.mean(v, axis=-1, keepdims=True)
        xc = v - mu
        vn = (xc * lax.rsqrt(jnp.mean(xc * xc, axis=-1, keepdims=True) + EPS)) * lg + lb
        for vn_ref in maybe_vn_ref:
            vn_ref[s, rs, :] = vn
        vb = vn.astype(BF16)
        for g in range(GM_GROUPS):
            cols = slice(g * GM_GROUP_DIM, (g + 1) * GM_GROUP_DIM)
            mixed = jnp.dot(wt[g], vb[:, cols], preferred_element_type=F32) + bs[:, g:g + 1]
            y_ref[s, rs, cols] = (u_ref[s, rs, cols] * mixed).astype(BF16)


def _spatial_gate(uv, ln_g, ln_b, ws, bs, *, rows, nbatch, emit_vn):
    b, t, _ = uv.shape
    span = min(t, GM_CHUNK)
    ws_l = ws[:, :span, :span]
    bs_t = jnp.transpose(bs[:, :span])
    tile = pl.BlockSpec((nbatch, rows, D_MODEL), lambda bb, i: (bb, i, 0))
    outs = pl.pallas_call(
        functools.partial(_gate_kernel, rows=rows, span=span),
        out_shape=(jax.ShapeDtypeStruct((b, t, D_MODEL), BF16),)
        + ((jax.ShapeDtypeStruct((b, t, D_MODEL), F32),) if emit_vn else ()),
        grid=(b // nbatch, t // rows),
        in_specs=[
            pl.BlockSpec((nbatch, rows, D_MODEL), lambda bb, i: (bb, i, 0)),
            pl.BlockSpec((nbatch, rows, D_MODEL), lambda bb, i: (bb, i, 1)),
            pl.BlockSpec((1, D_MODEL), lambda bb, i: (0, 0)),
            pl.BlockSpec((1, D_MODEL), lambda bb, i: (0, 0)),
            pl.BlockSpec((GM_GROUPS, span, span), lambda bb, i: (0, 0, 0)),
            pl.BlockSpec((span, GM_GROUPS), lambda bb, i: (0, 0)),
        ],
        out_specs=(tile, tile) if emit_vn else (tile,),
        compiler_params=_params("arbitrary", "arbitrary"),
        name="spatial_gate",
    )(uv, uv, ln_g.reshape(1, -1), ln_b.reshape(1, -1), ws_l, bs_t)
    return outs if emit_vn else (outs[0], None)


ROW_TILE = 1024
FFN_ROW_TILE = 2048
DOWN_ROW_TILE = 256
OUT_ROW_TILE = 512
OUT_K_BLK = 1024
IN_EVEN_BLK = 1408
IN_ODD_BLK = 1024
IN_ODD_ROW_TILE = 2048
FFN_BLK = 512
ATTN_ROW_TILE = 1024


def _conv_ffn(hp, xp, hs, xs, g3, g_next, w_up, layer, conv_w, conv_b, w_down, state_p, state_s):
    nseq_s = state_s.shape[0]
    act_p, c_p, wdb = _ffn_up(xp, w_up, layer, conv_w, conv_b, state_p, w_down, groups=1,
                              rows_per_group=FFN_ROW_TILE, bn=FFN_BLK, name="ffn_up_p")
    act_s, c_s, _ = _ffn_up(xs, w_up, layer, conv_w, conv_b, state_s, None, groups=nseq_s,
                            rows_per_group=xs.shape[0] // nseq_s, bn=FFN_BLK, name="ffn_up_s")
    wdb = wdb.reshape(wdb.shape[1:])
    hp, xp = _down_norm_res(act_p, wdb, g3, hp, g_next, bm=DOWN_ROW_TILE, name="ffn_down_p")
    hs, xs = _down_norm_res(act_s, wdb, g3, hs, g_next, bm=DOWN_ROW_TILE, name="ffn_down_s")
    return hp, xp, c_p, hs, xs, c_s


def kernel(x_prompt, x_sample, cache_swa_k, cache_swa_v, state_ret, state_ffn_conv, norm_g, w_in_even,
           w_out_even, attn_sinks, ret_norm_g, w_in_odd, w_out_odd, gm_ln_g, gm_ln_b, gm_ws, gm_bs,
           ffn_w_up, ffn_conv_w, ffn_conv_b, ffn_w_down):
    nb, seq, d = x_prompt.shape
    db, dseq, _ = x_sample.shape
    depth = norm_g.shape[0]
    hp = x_prompt.reshape(nb * seq, d)
    hs = x_sample.reshape(db * dseq, d)
    bm_p, bm_s = ROW_TILE, db * dseq
    pos_p = jnp.arange(seq)
    pos_s = PAST_LEN + jnp.arange(dseq)
    kp_l, vp_l, rp_l, cp_l = [], [], [], []
    ks_l, vs_l, rs_l, cs_l, gv_l = [], [], [], [], []
    xp = _rmsnorm(hp, norm_g[0, 0], bm=ROW_TILE, name="norm_in_p")
    xs = _rmsnorm(hs, norm_g[0, 0], bm=bm_s, name="norm_in_s")
    for layer in range(depth):
        g = norm_g[layer]
        g_next = norm_g[layer + 1, 0] if layer + 1 < depth else None
        if layer % 2 == 0:
            e = layer // 2
            zp = _xw(xp, w_in_even, e, bm=bm_p, bn=IN_EVEN_BLK, gelu=False, name="in_even_p")
            zs = _xw(xs, w_in_even, e, bm=bm_s, bn=IN_EVEN_BLK, gelu=False, name="in_even_s")
            zp = zp.reshape(nb, seq, EVEN_IN)
            zs = zs.reshape(db, dseq, EVEN_IN)
            attn_p = _attn_prompt(zp, attn_sinks[e], tq=ATTN_ROW_TILE)
            attn_s = _attn_sample(zs, cache_swa_k[e], cache_swa_v[e], attn_sinks[e])
            ret_p, r_p = _retention(zp, pos_p, jnp.zeros((nb, N_RET_HEADS, RET_DK, RET_DV), F32),
                                    ret_norm_g[e], chunk=RET_CHUNK, nbatch=1)
            ret_s, r_s = _retention(zs, pos_s, state_ret[e].astype(F32), ret_norm_g[e], chunk=dseq,
                                    nbatch=RET_SAMPLE_SEQS_PER_STEP)
            mixed_p = [attn_p.reshape(nb * seq, -1), ret_p.reshape(nb * seq, -1)]
            mixed_s = [attn_s.reshape(db * dseq, -1), ret_s.reshape(db * dseq, -1)]
            hp, xp = _matmul_norm_res(mixed_p, w_out_even, e, g[1], hp, g[2], bm=OUT_ROW_TILE, bk=OUT_K_BLK,
                                      resident=True, name="out_even_p")
            hs, xs = _matmul_norm_res(mixed_s, w_out_even, e, g[1], hs, g[2], bm=bm_s, bk=OUT_K_BLK,
                                      resident=False, name="out_even_s")
            k_new = zs[:, :, 1024:1280].reshape(db, dseq, N_KV_HEADS, HEAD_DIM)
            v_new = zs[:, :, 1280:1536].reshape(db, dseq, N_KV_HEADS, HEAD_DIM)
            n_keep = cache_swa_k.shape[2]
            kp_l.append(zp[:, seq - WINDOW:, 1024:1280].reshape(nb, WINDOW, N_KV_HEADS, HEAD_DIM))
            vp_l.append(zp[:, seq - WINDOW:, 1280:1536].reshape(nb, WINDOW, N_KV_HEADS, HEAD_DIM))
            ks_l.append(jnp.concatenate([cache_swa_k[e], k_new], axis=1)[:, -n_keep:])
            vs_l.append(jnp.concatenate([cache_swa_v[e], v_new], axis=1)[:, -n_keep:])
            rp_l.append(r_p)
            rs_l.append(r_s.astype(state_ret.dtype))
        else:
            o = layer // 2
            uvp = _xw(xp, w_in_odd, o, bm=IN_ODD_ROW_TILE, bn=IN_ODD_BLK, gelu=True, name="in_odd_p")
            uvs = _xw(xs, w_in_odd, o, bm=bm_s, bn=IN_ODD_BLK, gelu=True, name="in_odd_s")
            yp, _ = _spatial_gate(uvp.reshape(nb, seq, -1), gm_ln_g[o], gm_ln_b[o], gm_ws[o], gm_bs[o], rows=1024,
                                  nbatch=1, emit_vn=False)
            ys, gv = _spatial_gate(uvs.reshape(db, dseq, -1), gm_ln_g[o], gm_ln_b[o], gm_ws[o], gm_bs[o],
                                   rows=dseq, nbatch=GATE_SAMPLE_SEQS_PER_STEP, emit_vn=True)
            hp, xp = _matmul_norm_res([yp.reshape(nb * seq, -1)], w_out_odd, o, g[1], hp, g[2], bm=OUT_ROW_TILE,
                                      bk=OUT_K_BLK, resident=True, name="out_odd_p")
            hs, xs = _matmul_norm_res([ys.reshape(db * dseq, -1)], w_out_odd, o, g[1], hs, g[2], bm=bm_s,
                                      bk=OUT_K_BLK, resident=False, name="out_odd_s")
            gv_l.append(gv)
        zero_state = jnp.zeros((nb, CONV_W - 1, 2 * D_FF), F32)
        hp, xp, c_p, hs, xs, c_s = _conv_ffn(hp, xp, hs, xs, g[3], g_next, ffn_w_up, layer, ffn_conv_w[layer],
                                             ffn_conv_b[layer], ffn_w_down, zero_state, state_ffn_conv[layer])
        cp_l.append(c_p)
        cs_l.append(c_s)
    return (hp.reshape(nb, seq, d), hs.reshape(db, dseq, d),
            jnp.stack(kp_l), jnp.stack(vp_l), jnp.stack(rp_l), jnp.stack(cp_l),
            jnp.stack(ks_l), jnp.stack(vs_l), jnp.stack(rs_l), jnp.stack(cs_l), jnp.stack(gv_l))
```

```python
import functools

import jax
import jax.numpy as jnp
from jax import lax
from jax.experimental import pallas as pl
from jax.experimental.pallas import tpu as pltpu

F32 = jnp.float32
BF16 = jnp.bfloat16

D_MODEL = 2048
CHUNK = 64
HEAD_DIM = 64
N_Q_HEADS = 16
N_KV_HEADS = 4
Q_PER_KV = N_Q_HEADS // N_KV_HEADS
WINDOW = 128
N_RET_HEADS = 8
RET_DK = 128
RET_DV = 128
ROPE_BASE = 10000.0
GM_CHUNK = 128
GM_GROUPS = 8
GM_GROUP_DIM = D_MODEL // GM_GROUPS
D_FF = 5632
CONV_W = 3
EPS = 1e-6
PAST_LEN = 1024
EVEN_IN = 5632

SUBLANES = 8
VMEM_LIMIT_BYTES = 56 * 1024 * 1024

COL_BLK = 512
KV_BLK = 2
RQ_BLK, RK_BLK, RV_BLK, RG_BLK = 3, 5, 7, 9
RET_HEADS_PER_BLK = COL_BLK // RET_DK
RET_CHUNK = 256
SUB_COLS = 256
EPI_ROWS = 16
NORM_ROWS = 64
GELU_C = 0.7978845608028654
RET_SAMPLE_SEQS_PER_STEP = 4
GATE_SAMPLE_SEQS_PER_STEP = 8
ATTN_SAMPLE_SEQS_PER_STEP = 1


def _params(*sem):
    return pltpu.CompilerParams(dimension_semantics=sem, vmem_limit_bytes=VMEM_LIMIT_BYTES)


def _rmsnorm_rows(x_ref, g_ref, out_ref, rows, chunk):
    g = g_ref[...]

    def body(c, carry):
        r0 = pl.multiple_of(c * chunk, chunk)
        x = x_ref[pl.ds(r0, chunk), :]
        ms = jnp.mean(x * x, axis=-1, keepdims=True)
        out_ref[pl.ds(r0, chunk), :] = ((x * lax.rsqrt(ms + EPS)) * g).astype(out_ref.dtype)
        return carry

    lax.fori_loop(0, rows // chunk, body, 0)


def _zero_after(x):
    z = pltpu.bitcast(x, jnp.uint32)
    z = lax.shift_right_logical(lax.shift_right_logical(z, jnp.uint32(16)), jnp.uint32(16))
    return pltpu.bitcast(z, F32)


def _rmsnorm_kernel(x_ref, g_ref, o_ref, *, bm):
    _rmsnorm_rows(x_ref, g_ref, o_ref, bm, min(bm, 128))


def _rmsnorm(x, g, *, bm, name):
    m, k = x.shape
    return pl.pallas_call(
        functools.partial(_rmsnorm_kernel, bm=bm),
        out_shape=jax.ShapeDtypeStruct((m, k), BF16),
        grid=(m // bm,),
        in_specs=[pl.BlockSpec((bm, k), lambda i: (i, 0)), pl.BlockSpec((1, k), lambda i: (0, 0))],
        out_specs=pl.BlockSpec((bm, k), lambda i: (i, 0)),
        compiler_params=_params("arbitrary"),
        name=name,
    )(x, g.reshape(1, k))


def _xw_kernel(xn_ref, w_ref, o_ref, wb_ref, *, bm, rc, gelu):
    @pl.when(pl.program_id(1) == 0)
    def _():
        wb_ref[...] = w_ref[...].astype(BF16)

    for c in range(bm // rc):
        y = jnp.dot(xn_ref[c * rc:(c + 1) * rc, :], wb_ref[...], preferred_element_type=F32)
        if gelu:
            y = jax.nn.gelu(y, approximate=True)
        o_ref[c * rc:(c + 1) * rc, :] = y.astype(o_ref.dtype)


def _xw(xn, w, layer, *, bm, bn, gelu, name):
    m, k = xn.shape
    n = w.shape[2]
    rc = min(bm, 256)
    return pl.pallas_call(
        functools.partial(_xw_kernel, bm=bm, rc=rc, gelu=gelu),
        out_shape=jax.ShapeDtypeStruct((m, n), F32),
        grid=(n // bn, m // bm),
        in_specs=[
            pl.BlockSpec((bm, k), lambda j, i: (i, 0)),
            pl.BlockSpec((None, k, bn), lambda j, i: (layer, 0, j)),
        ],
        out_specs=pl.BlockSpec((bm, bn), lambda j, i: (i, j)),
        scratch_shapes=[pltpu.VMEM((k, bn), BF16)],
        compiler_params=_params("arbitrary", "arbitrary"),
        name=name,
    )(xn, w)


def _matmul_norm_res_kernel(*refs, splits, nk, bm, rc, emit_next, weights):
    na = len(splits)
    a_refs = refs[:na]
    rest = list(refs[na:])
    wb_ref = rest.pop() if weights != "bf16" else None
    if emit_next:
        w_ref, g_ref, r_ref, gn_ref, o_ref, xn_ref = rest
    else:
        w_ref, g_ref, r_ref, o_ref = rest
    i = pl.program_id(0)
    k = pl.program_id(1)
    if weights == "stream":
        wb_ref[...] = w_ref[...].astype(BF16)
        w_blk = wb_ref
    elif weights == "resident":
        @pl.when(i == 0)
        def _():
            wb_ref[k] = w_ref[...].astype(BF16)

        w_blk = wb_ref.at[k]
    else:
        w_blk = w_ref

    def finish(r0, y):
        rows = slice(r0, r0 + NORM_ROWS)
        ms = jnp.mean(y * y, axis=-1, keepdims=True)
        h = r_ref[rows, :] + (y * lax.rsqrt(ms + EPS)) * g_ref[...]
        o_ref[rows, :] = h
        if emit_next:
            ms2 = jnp.mean(h * h, axis=-1, keepdims=True)
            xn_ref[rows, :] = ((h * lax.rsqrt(ms2 + EPS)) * gn_ref[...]).astype(BF16)

    def step(a_ref, first, last):
        for c in range(bm // rc):
            rows = slice(c * rc, (c + 1) * rc)
            y = jnp.dot(a_ref[rows, :], w_blk[...], preferred_element_type=F32)
            if not first:
                y = o_ref[rows, :] + y
            if last:
                for r in range(0, rc, NORM_ROWS):
                    finish(c * rc + r, y[r:r + NORM_ROWS])
            else:
                o_ref[rows, :] = y

    for a_ref, (k0, k1) in zip(a_refs, splits):
        for first, last in sorted({(kk == 0, kk == nk - 1) for kk in range(k0, k1)}):
            ks = [kk for kk in range(k0, k1) if (kk == 0, kk == nk - 1) == (first, last)]
            pl.when((k >= ks[0]) & (k <= ks[-1]))(functools.partial(step, a_ref, first, last))


def _matmul_norm_res(a_list, w, layer, g, resid, g_next, *, bm, bk, resident, name):
    m = a_list[0].shape[0]
    n = w.shape[2]
    splits, k0 = [], 0
    for a in a_list:
        splits.append((k0, k0 + a.shape[1] // bk))
        k0 = splits[-1][1]
    nk = k0
    rc = min(bm, 256)
    emit_next = g_next is not None
    weights = "bf16" if w.dtype == BF16 else ("resident" if resident else "stream")

    def a_spec(k0, k1):
        return pl.BlockSpec((bm, bk), lambda i, k: (i, jnp.clip(k - k0, 0, k1 - k0 - 1)))

    if weights == "resident":
        w_spec = pl.BlockSpec((None, bk, n), lambda i, k: (layer, jnp.where(i == 0, k, nk - 1), 0))
        scratch = [pltpu.VMEM((nk, bk, n), BF16)]
    else:
        w_spec = pl.BlockSpec((None, bk, n), lambda i, k: (layer, k, 0))
        scratch = [pltpu.VMEM((bk, n), BF16)] if weights == "stream" else []
    row_vec = pl.BlockSpec((1, n), lambda i, k: (0, 0))
    tile = pl.BlockSpec((bm, n), lambda i, k: (i, 0))
    out = pl.pallas_call(
        functools.partial(_matmul_norm_res_kernel, splits=tuple(splits), nk=nk, bm=bm, rc=rc,
                          emit_next=emit_next, weights=weights),
        out_shape=((jax.ShapeDtypeStruct((m, n), F32), jax.ShapeDtypeStruct((m, n), BF16)) if emit_next
                   else jax.ShapeDtypeStruct((m, n), F32)),
        grid=(m // bm, nk),
        in_specs=[a_spec(*sp) for sp in splits] + [w_spec, row_vec, tile] + ([row_vec] if emit_next else []),
        out_specs=(tile, tile) if emit_next else tile,
        scratch_shapes=scratch,
        compiler_params=_params("arbitrary", "arbitrary"),
        name=name,
    )(*a_list, w, g.reshape(1, n), resid, *([g_next.reshape(1, n)] if emit_next else []))
    return out if emit_next else (out, None)


def _down_kernel(*refs, bm, emit_next):
    if emit_next:
        a_ref, w_hbm_ref, g_ref, r_ref, gn_ref, o_ref, xn_ref, w_ref, sem = refs
    else:
        a_ref, w_hbm_ref, g_ref, r_ref, o_ref, w_ref, sem = refs

    @pl.when(pl.program_id(0) == 0)
    def _():
        copy = pltpu.make_async_copy(w_hbm_ref, w_ref, sem)
        copy.start()
        copy.wait()

    y = jnp.dot(a_ref[...], w_ref[...], preferred_element_type=F32)
    for r in range(0, bm, NORM_ROWS):
        rows = slice(r, r + NORM_ROWS)
        yy = y[rows]
        ms = jnp.mean(yy * yy, axis=-1, keepdims=True)
        h = r_ref[rows, :] + (yy * lax.rsqrt(ms + EPS)) * g_ref[...]
        o_ref[rows, :] = h
        if emit_next:
            ms2 = jnp.mean(h * h, axis=-1, keepdims=True)
            xn_ref[rows, :] = ((h * lax.rsqrt(ms2 + EPS)) * gn_ref[...]).astype(BF16)


def _down_norm_res(a, w, g, resid, g_next, *, bm, name):
    m, kdim = a.shape
    n = w.shape[1]
    emit_next = g_next is not None
    row_vec = pl.BlockSpec((1, n), lambda i: (0, 0))
    tile = pl.BlockSpec((bm, n), lambda i: (i, 0))
    out = pl.pallas_call(
        functools.partial(_down_kernel, bm=bm, emit_next=emit_next),
        out_shape=((jax.ShapeDtypeStruct((m, n), F32), jax.ShapeDtypeStruct((m, n), BF16)) if emit_next
                   else jax.ShapeDtypeStruct((m, n), F32)),
        grid=(m // bm,),
        in_specs=[pl.BlockSpec((bm, kdim), lambda i: (i, 0)), pl.BlockSpec(memory_space=pl.ANY), row_vec, tile]
        + ([row_vec] if emit_next else []),
        out_specs=(tile, tile) if emit_next else tile,
        scratch_shapes=[pltpu.VMEM((kdim, n), BF16), pltpu.SemaphoreType.DMA(())],
        compiler_params=_params("arbitrary"),
        name=name,
    )(a, w, g.reshape(1, n), resid, *([g_next.reshape(1, n)] if emit_next else []))
    return out if emit_next else (out, None)


def _ffn_up_kernel(*refs, groups, rows_per_group, rc, tiles_per_seq, cast_down):
    if cast_down:
        (xn_ref, wa_ref, wg_ref, cwa_ref, cwg_ref, sa_ref, sg_ref, wd_ref,
         act_ref, la_ref, lg_ref, wdb_ref, wb_ref, carry_ref) = refs
    else:
        (xn_ref, wa_ref, wg_ref, cwa_ref, cwg_ref, sa_ref, sg_ref,
         act_ref, la_ref, lg_ref, wb_ref, carry_ref) = refs
    i = pl.program_id(1)
    bm = groups * rows_per_group
    bn = act_ref.shape[1]
    nsub = bn // SUB_COLS
    subs = [slice(u * SUB_COLS, (u + 1) * SUB_COLS) for u in range(nsub)]

    @pl.when(i == 0)
    def _():
        for u in range(nsub):
            wb_ref[u, :, :SUB_COLS] = wa_ref[:, subs[u]].astype(BF16)
            wb_ref[u, :, SUB_COLS:] = wg_ref[:, subs[u]].astype(BF16)
        if cast_down:
            wdb_ref[...] = wd_ref[...].astype(BF16)

    keep = SUBLANES - (CONV_W - 1)

    @pl.when(lax.rem(i, tiles_per_seq) == 0)
    def _():
        carry_ref[...] = jnp.zeros(carry_ref.shape, F32)
        carry_ref[0, :, keep:, :] = sa_ref[...]
        carry_ref[1, :, keep:, :] = sg_ref[...]

    cws = [jnp.concatenate([cwa_ref[:, subs[u]], cwg_ref[:, subs[u]]], axis=1) for u in range(nsub)]
    piece = min(EPI_ROWS, rows_per_group)
    prev = [None] * nsub
    after = [None] * nsub
    for c in range(bm // rc):
        xc = xn_ref[c * rc:(c + 1) * rc, :]
        for u in range(nsub):
            h = jnp.dot(xc, wb_ref[u], preferred_element_type=F32)
            for q in range(rc // piece):
                row = c * rc + q * piece
                grp = row // rows_per_group
                hcur = h[q * piece:(q + 1) * piece]
                if row % rows_per_group == 0:
                    prev8 = jnp.concatenate([carry_ref[0, grp, :, subs[u]], carry_ref[1, grp, :, subs[u]]], axis=1)
                else:
                    prev8 = prev[u]
                hext = jnp.concatenate([prev8, hcur], axis=0)
                s1 = pltpu.roll(hext, 1, 0)[SUBLANES:]
                s2 = pltpu.roll(hext, 2, 0)[SUBLANES:]
                cw = cws[u] if after[u] is None else cws[u] + jnp.concatenate([after[u]] * 2, axis=1)
                conv = cw[3:4] + s2 * cw[0:1] + s1 * cw[1:2] + hcur * cw[2:3]
                gate, lin = conv[:, SUB_COLS:], conv[:, :SUB_COLS]
                inner = gate * (GELU_C + (GELU_C * 0.044715) * (gate * gate))
                act = (gate * lin) * (0.5 + 0.5 * jnp.tanh(inner))
                act_ref[row:row + piece, subs[u]] = act.astype(BF16)
                after[u] = _zero_after(act[piece - SUBLANES:])
                prev[u] = hcur[piece - SUBLANES:]
                if (row + piece) % rows_per_group == 0:
                    la_ref[grp, :, subs[u]] = prev[u][keep:, :SUB_COLS]
                    lg_ref[grp, :, subs[u]] = prev[u][keep:, SUB_COLS:]
                    carry_ref[0, grp, :, subs[u]] = prev[u][:, :SUB_COLS]
                    carry_ref[1, grp, :, subs[u]] = prev[u][:, SUB_COLS:]


def _ffn_up(xn, w_up, layer, conv_w, conv_b, state, w_down, *, groups, rows_per_group, bn, name):
    m, k = xn.shape
    nseq = state.shape[0]
    bm = groups * rows_per_group
    ni = m // bm
    tiles_per_seq = ni * groups // nseq
    nj = D_FF // bn
    rc = min(bm, 256)
    cw = jnp.concatenate([conv_w, conv_b[None], jnp.zeros((SUBLANES - CONV_W - 1, 2 * D_FF), F32)], axis=0)
    cast_down = w_down is not None
    d_out = w_down.shape[2] if cast_down else 0
    out_shape = [jax.ShapeDtypeStruct((m, D_FF), BF16),
                 jax.ShapeDtypeStruct((nseq, CONV_W - 1, D_FF), F32),
                 jax.ShapeDtypeStruct((nseq, CONV_W - 1, D_FF), F32)]
    in_specs = [
        pl.BlockSpec((bm, k), lambda j, i: (i, 0)),
        pl.BlockSpec((None, k, bn), lambda j, i: (layer, 0, j)),
        pl.BlockSpec((None, k, bn), lambda j, i: (layer, 0, nj + j)),
        pl.BlockSpec((SUBLANES, bn), lambda j, i: (0, j)),
        pl.BlockSpec((SUBLANES, bn), lambda j, i: (0, nj + j)),
        pl.BlockSpec((groups, CONV_W - 1, bn), lambda j, i: (i // tiles_per_seq, 0, j)),
        pl.BlockSpec((groups, CONV_W - 1, bn), lambda j, i: (i // tiles_per_seq, 0, nj + j)),
    ]
    out_specs = [
        pl.BlockSpec((bm, bn), lambda j, i: (i, j)),
        pl.BlockSpec((groups, CONV_W - 1, bn), lambda j, i: (i // tiles_per_seq, 0, j)),
        pl.BlockSpec((groups, CONV_W - 1, bn), lambda j, i: (i // tiles_per_seq, 0, j)),
    ]
    operands = [xn, w_up, w_up, cw, cw, state, state]
    if cast_down:
        in_specs.append(pl.BlockSpec((None, bn, d_out), lambda j, i: (layer, j, 0)))
        out_specs.append(pl.BlockSpec((None, bn, d_out), lambda j, i: (0, j, 0)))
        out_shape.append(jax.ShapeDtypeStruct((1, D_FF, d_out), BF16))
        operands.append(w_down)
    outs = pl.pallas_call(
        functools.partial(_ffn_up_kernel, groups=groups, rows_per_group=rows_per_group, rc=rc,
                          tiles_per_seq=tiles_per_seq, cast_down=cast_down),
        out_shape=tuple(out_shape),
        grid=(nj, ni),
        in_specs=in_specs,
        out_specs=tuple(out_specs),
        scratch_shapes=[pltpu.VMEM((bn // SUB_COLS, k, 2 * SUB_COLS), BF16),
                        pltpu.VMEM((2, groups, SUBLANES, bn), F32)],
        compiler_params=_params("arbitrary", "arbitrary"),
        name=name,
    )(*operands)
    act, la, lg = outs[:3]
    return act, jnp.concatenate([la, lg], axis=-1), (outs[3] if cast_down else None)


def _attn_head(q, k, v, sink_ref, h, bias):
    nq = q.shape[0]
    kh = k[:, h * HEAD_DIM:(h + 1) * HEAD_DIM].astype(BF16)
    vh = v[:, h * HEAD_DIM:(h + 1) * HEAD_DIM].astype(BF16)
    qs, sk = [], []
    for gq in range(Q_PER_KV):
        c0 = (h * Q_PER_KV + gq) * HEAD_DIM
        qs.append(q[:, c0:c0 + HEAD_DIM])
        sk.append(jnp.full((nq, 1), sink_ref[h * Q_PER_KV + gq], F32))
    qh = (jnp.concatenate(qs, axis=0) * (HEAD_DIM ** -0.5)).astype(BF16)
    sk = jnp.concatenate(sk, axis=0)
    s = lax.dot_general(qh, kh, (((1,), (1,)), ((), ())), preferred_element_type=F32)
    if bias is not None:
        s = s + bias
    mx = jnp.maximum(jnp.max(s, axis=-1, keepdims=True), sk)
    p = jnp.exp(s - mx)
    den = jnp.sum(p, axis=-1, keepdims=True) + jnp.exp(sk - mx)
    o = jnp.dot((p / den).astype(BF16), vh, preferred_element_type=F32)
    return jnp.concatenate([o[gq * nq:(gq + 1) * nq] for gq in range(Q_PER_KV)], axis=1)


ATTN_UNIT = 2 * CHUNK


def _attn_prompt_kernel(sink_ref, q_ref, kv_ref, kvp_ref, o_ref, bias_ref, *, tq):
    i = pl.program_id(1)
    nk = ATTN_UNIT + WINDOW
    cols = Q_PER_KV * ATTN_UNIT
    kvw = N_KV_HEADS * HEAD_DIM
    @pl.when(i == 0)
    def _():
        r_k = lax.broadcasted_iota(jnp.int32, (nk, cols), 0)
        r_q = lax.broadcasted_iota(jnp.int32, (nk, cols), 1)
        lo = (r_q & (ATTN_UNIT - 1)) & ~(CHUNK - 1)
        band = (r_k >= lo) & (r_k < lo + WINDOW + CHUNK)
        bias_ref[1] = jnp.where(band, 0.0, -jnp.inf)
        bias_ref[0] = jnp.where(band & (r_k >= WINDOW), 0.0, -jnp.inf)

    first_unit_bias = jnp.where(i == 0, 0, 1)

    kv_all = jnp.concatenate([kvp_ref[0], kv_ref[0]], axis=0)
    k_all = kv_all[:, :kvw].astype(BF16)
    vt_all = jnp.transpose(kv_all[:, kvw:]).astype(BF16)
    for u in range(tq // ATTN_UNIT):
        rs = slice(u * ATTN_UNIT, (u + 1) * ATTN_UNIT)
        keys = slice(u * ATTN_UNIT, u * ATTN_UNIT + nk)
        k_u = k_all[keys]
        qt = jnp.transpose(q_ref[0, rs, :] * (HEAD_DIM ** -0.5)).astype(BF16)
        bias = bias_ref[first_unit_bias] if u == 0 else bias_ref[1]
        pieces = []
        for h in range(N_KV_HEADS):
            heads = [h * Q_PER_KV + gq for gq in range(Q_PER_KV)]
            qt_h = jnp.concatenate([qt[n * HEAD_DIM:(n + 1) * HEAD_DIM] for n in heads], axis=1)
            parts = [qt_h]
            if h > 0:
                parts.insert(0, jnp.zeros((h * HEAD_DIM, cols), BF16))
            if h < N_KV_HEADS - 1:
                parts.append(jnp.zeros(((N_KV_HEADS - 1 - h) * HEAD_DIM, cols), BF16))
            st = jnp.dot(k_u, jnp.concatenate(parts, axis=0), preferred_element_type=F32) + bias
            sk = jnp.concatenate([jnp.full((1, ATTN_UNIT), sink_ref[n], F32) for n in heads], axis=1)
            mx = jnp.maximum(jnp.max(st, axis=0, keepdims=True), sk)
            p = jnp.exp(st - mx)
            den = jnp.sum(p, axis=0, keepdims=True) + jnp.exp(sk - mx)
            ot = jnp.dot(vt_all[h * HEAD_DIM:(h + 1) * HEAD_DIM, keys], (p / den).astype(BF16),
                         preferred_element_type=F32)
            pieces += [ot[:, gq * ATTN_UNIT:(gq + 1) * ATTN_UNIT] for gq in range(Q_PER_KV)]
        o_ref[0, rs, :] = jnp.transpose(jnp.concatenate(pieces, axis=0)).astype(BF16)


def _attn_prompt(z, sinks, *, tq):
    b, t, _ = z.shape
    per = tq // WINDOW
    return pl.pallas_call(
        functools.partial(_attn_prompt_kernel, tq=tq),
        out_shape=jax.ShapeDtypeStruct((b, t, N_Q_HEADS * HEAD_DIM), BF16),
        grid=(b, t // tq),
        in_specs=[
            pl.BlockSpec(memory_space=pltpu.SMEM),
            pl.BlockSpec((1, tq, 1024), lambda bb, i: (bb, i, 0)),
            pl.BlockSpec((1, tq, COL_BLK), lambda bb, i: (bb, i, KV_BLK)),
            pl.BlockSpec((1, WINDOW, COL_BLK), lambda bb, i: (bb, jnp.maximum(i * per - 1, 0), KV_BLK)),
        ],
        out_specs=pl.BlockSpec((1, tq, 1024), lambda bb, i: (bb, i, 0)),
        scratch_shapes=[pltpu.VMEM((2, ATTN_UNIT + WINDOW, Q_PER_KV * ATTN_UNIT), F32)],
        compiler_params=_params("arbitrary", "arbitrary"),
        name="attn_prompt",
    )(sinks, z, z, z)


def _attn_sample_kernel(sink_ref, q_ref, kv_ref, ck_ref, cv_ref, o_ref):
    for s in range(q_ref.shape[0]):
        kv = kv_ref[s]
        k = jnp.concatenate([ck_ref[s], kv[:, :256]], axis=0)
        v = jnp.concatenate([cv_ref[s], kv[:, 256:]], axis=0)
        q = q_ref[s]
        for h in range(N_KV_HEADS):
            o_ref[s, :, h * 256:(h + 1) * 256] = _attn_head(q, k, v, sink_ref, h, None).astype(BF16)


def _attn_sample(z, cache_k, cache_v, sinks):
    b, t, _ = z.shape
    nc = cache_k.shape[1]
    nb = ATTN_SAMPLE_SEQS_PER_STEP
    return pl.pallas_call(
        _attn_sample_kernel,
        out_shape=jax.ShapeDtypeStruct((b, t, N_Q_HEADS * HEAD_DIM), BF16),
        grid=(b // nb,),
        in_specs=[
            pl.BlockSpec(memory_space=pltpu.SMEM),
            pl.BlockSpec((nb, t, 1024), lambda bb: (bb, 0, 0)),
            pl.BlockSpec((nb, t, COL_BLK), lambda bb: (bb, 0, KV_BLK)),
            pl.BlockSpec((nb, nc, 256), lambda bb: (bb, 0, 0)),
            pl.BlockSpec((nb, nc, 256), lambda bb: (bb, 0, 0)),
        ],
        out_specs=pl.BlockSpec((nb, t, 1024), lambda bb: (bb, 0, 0)),
        compiler_params=_params("arbitrary"),
        name="attn_sample",
    )(sinks, z, z, cache_k.reshape(b, nc, 256), cache_v.reshape(b, nc, 256))


def _retention_kernel(rq_ref, rk_ref, rv_ref, rg_ref, cs_ref, sn_ref, intra_ref, qd_ref, kd_ref, cd_ref,
                      ng_ref, s0_ref, o_ref, st_ref):
    c = pl.program_id(2)

    @pl.when(c == 0)
    def _():
        st_ref[...] = s0_ref[...]

    cs = cs_ref[...]
    sn = sn_ref[...]
    for s, hh in [(s, hh) for s in range(rq_ref.shape[0]) for hh in range(RET_HEADS_PER_BLK)]:
        lanes = slice(hh * RET_DK, (hh + 1) * RET_DK)
        q = rq_ref[s, :, lanes]
        k = rk_ref[s, :, lanes]
        qr = q * cs + pltpu.roll(q, RET_DK // 2, 1) * sn
        kr = (k * cs + pltpu.roll(k, RET_DK // 2, 1) * sn) * (RET_DK ** -0.5)
        qb = qr.astype(BF16)
        kb = kr.astype(BF16)
        vb = rv_ref[s, :, lanes].astype(BF16)
        sc = lax.dot_general(qb, kb, (((1,), (1,)), ((), ())), preferred_element_type=F32) * intra_ref[hh]
        inner = jnp.dot(sc.astype(BF16), vb, preferred_element_type=F32)
        state = st_ref[s, hh]
        cross = jnp.dot(qb, state.astype(BF16), preferred_element_type=F32) * qd_ref[hh]
        kdt = jnp.transpose(kr * kd_ref[hh]).astype(BF16)
        st_ref[s, hh] = cd_ref[hh, 0:1, :] * state + jnp.dot(kdt, vb, preferred_element_type=F32)
        r = inner + cross
        mu = jnp.mean(r, axis=-1, keepdims=True)
        yc = r - mu
        yn = yc * lax.rsqrt(jnp.mean(yc * yc, axis=-1, keepdims=True) + EPS)
        o_ref[s, :, lanes] = ((yn * ng_ref[:, lanes]) * jax.nn.silu(rg_ref[s, :, lanes])).astype(BF16)


def _ret_log_gamma():
    return jnp.log1p(-jnp.exp2(-5.0 - jnp.arange(N_RET_HEADS, dtype=F32)))


def _retention(z, pos, state0, norm_g, *, chunk, nbatch):
    b, t, _ = z.shape
    log_g = _ret_log_gamma()
    idx = jnp.arange(chunk, dtype=F32)
    diff = idx[:, None] - idx[None, :]
    intra = jnp.where(diff[None] >= 0.0,
                      jnp.exp(log_g[:, None, None] * jnp.maximum(diff, 0.0)[None]), 0.0)
    ones = jnp.ones((1, 1, RET_DV), F32)
    q_decay = jnp.exp(log_g[:, None] * (idx[None, :] + 1.0))[:, :, None] * ones
    k_decay = jnp.exp(log_g[:, None] * (chunk - 1.0 - idx)[None, :])[:, :, None] * ones
    c_decay = jnp.exp(log_g * chunk)[:, None, None] * jnp.ones((1, SUBLANES, RET_DV), F32)
    half = RET_DK // 2
    freq = 1.0 / (ROPE_BASE ** (jnp.arange(half, dtype=F32) / half))
    ang = pos.astype(F32)[:, None] * freq[None, :]
    cos, sin = jnp.cos(ang), jnp.sin(ang)
    cs = jnp.concatenate([cos, cos], axis=-1)
    sn = jnp.concatenate([-sin, sin], axis=-1)
    nhb = N_RET_HEADS // RET_HEADS_PER_BLK
    hb = RET_HEADS_PER_BLK

    def zspec(blk):
        return pl.BlockSpec((nbatch, chunk, COL_BLK), lambda bb, g, c: (bb, c, blk + g))

    def tab(rows):
        return pl.BlockSpec((hb, rows, RET_DV), lambda bb, g, c: (g, 0, 0))

    state_spec = pl.BlockSpec((nbatch, hb, RET_DK, RET_DV), lambda bb, g, c: (bb, g, 0, 0))
    out, st = pl.pallas_call(
        _retention_kernel,
        out_shape=(jax.ShapeDtypeStruct((b, t, N_RET_HEADS * RET_DV), BF16),
                   jax.ShapeDtypeStruct((b, N_RET_HEADS, RET_DK, RET_DV), F32)),
        grid=(b // nbatch, nhb, t // chunk),
        in_specs=[
            zspec(RQ_BLK), zspec(RK_BLK), zspec(RV_BLK), zspec(RG_BLK),
            pl.BlockSpec((chunk, RET_DK), lambda bb, g, c: (c, 0)),
            pl.BlockSpec((chunk, RET_DK), lambda bb, g, c: (c, 0)),
            pl.BlockSpec((hb, chunk, chunk), lambda bb, g, c: (g, 0, 0)),
            tab(chunk), tab(chunk), tab(SUBLANES),
            pl.BlockSpec((1, COL_BLK), lambda bb, g, c: (0, g)),
            state_spec,
        ],
        out_specs=(pl.BlockSpec((nbatch, chunk, COL_BLK), lambda bb, g, c: (bb, c, g)), state_spec),
        compiler_params=_params("arbitrary", "arbitrary", "arbitrary"),
        name="retention",
    )(z, z, z, z, cs, sn, intra, q_decay, k_decay, c_decay, norm_g.reshape(1, -1), state0)
    return out, st


def _gate_kernel(u_ref, v_ref, lg_ref, lb_ref, ws_ref, bs_ref, y_ref, *maybe_vn_ref, rows, span):
    ri = lax.broadcasted_iota(jnp.int32, (span, span), 0)
    ci = lax.broadcasted_iota(jnp.int32, (span, span), 1)
    wt = [jnp.where(ri >= ci, ws_ref[g], 0.0).astype(BF16) for g in range(GM_GROUPS)]
    bs = bs_ref[...]
    lg = lg_ref[...]
    lb = lb_ref[...]
    for s, c in [(s, c) for s in range(u_ref.shape[0]) for c in range(rows // span)]:
        rs = slice(c * span, (c + 1) * span)
        v = v_ref[s, rs, :]
        mu = jnp.mean(v, axis=-1, keepdims=True)
        xc = v - mu
        vn = (xc * lax.rsqrt(jnp.mean(xc * xc, axis=-1, keepdims=True) + EPS)) * lg + lb
        for vn_ref in maybe_vn_ref:
            vn_ref[s, rs, :] = vn
        vb = vn.astype(BF16)
        for g in range(GM_GROUPS):
            cols = slice(g * GM_GROUP_DIM, (g + 1) * GM_GROUP_DIM)
            mixed = jnp.dot(wt[g], vb[:, cols], preferred_element_type=F32) + bs[:, g:g + 1]
            y_ref[s, rs, cols] = (u_ref[s, rs, cols] * mixed).astype(BF16)


def _spatial_gate(uv, ln_g, ln_b, ws, bs, *, rows, nbatch, emit_vn):
    b, t, _ = uv.shape
    span = min(t, GM_CHUNK)
    ws_l = ws[:, :span, :span]
    bs_t = jnp.transpose(bs[:, :span])
    tile = pl.BlockSpec((nbatch, rows, D_MODEL), lambda bb, i: (bb, i, 0))
    outs = pl.pallas_call(
        functools.partial(_gate_kernel, rows=rows, span=span),
        out_shape=(jax.ShapeDtypeStruct((b, t, D_MODEL), BF16),)
        + ((jax.ShapeDtypeStruct((b, t, D_MODEL), F32),) if emit_vn else ()),
        grid=(b // nbatch, t // rows),
        in_specs=[
            pl.BlockSpec((nbatch, rows, D_MODEL), lambda bb, i: (bb, i, 0)),
            pl.BlockSpec((nbatch, rows, D_MODEL), lambda bb, i: (bb, i, 1)),
            pl.BlockSpec((1, D_MODEL), lambda bb, i: (0, 0)),
            pl.BlockSpec((1, D_MODEL), lambda bb, i: (0, 0)),
            pl.BlockSpec((GM_GROUPS, span, span), lambda bb, i: (0, 0, 0)),
            pl.BlockSpec((span, GM_GROUPS), lambda bb, i: (0, 0)),
        ],
        out_specs=(tile, tile) if emit_vn else (tile,),
        compiler_params=_params("arbitrary", "arbitrary"),
        name="spatial_gate",
    )(uv, uv, ln_g.reshape(1, -1), ln_b.reshape(1, -1), ws_l, bs_t)
    return outs if emit_vn else (outs[0], None)


ROW_TILE = 1024
FFN_ROW_TILE = 2048
DOWN_ROW_TILE = 256
OUT_ROW_TILE = 512
OUT_K_BLK = 1024
IN_EVEN_BLK = 1408
IN_ODD_BLK = 1024
FFN_BLK = 512
ATTN_ROW_TILE = 1024


def _conv_ffn(hp, xp, hs, xs, g3, g_next, w_up, layer, conv_w, conv_b, w_down, state_p, state_s):
    nseq_s = state_s.shape[0]
    act_p, c_p, wdb = _ffn_up(xp, w_up, layer, conv_w, conv_b, state_p, w_down, groups=1,
                              rows_per_group=FFN_ROW_TILE, bn=FFN_BLK, name="ffn_up_p")
    act_s, c_s, _ = _ffn_up(xs, w_up, layer, conv_w, conv_b, state_s, None, groups=nseq_s,
                            rows_per_group=xs.shape[0] // nseq_s, bn=FFN_BLK, name="ffn_up_s")
    wdb = wdb.reshape(wdb.shape[1:])
    hp, xp = _down_norm_res(act_p, wdb, g3, hp, g_next, bm=DOWN_ROW_TILE, name="ffn_down_p")
    hs, xs = _down_norm_res(act_s, wdb, g3, hs, g_next, bm=DOWN_ROW_TILE, name="ffn_down_s")
    return hp, xp, c_p, hs, xs, c_s


def kernel(x_prompt, x_sample, cache_swa_k, cache_swa_v, state_ret, state_ffn_conv, norm_g, w_in_even,
           w_out_even, attn_sinks, ret_norm_g, w_in_odd, w_out_odd, gm_ln_g, gm_ln_b, gm_ws, gm_bs,
           ffn_w_up, ffn_conv_w, ffn_conv_b, ffn_w_down):
    nb, seq, d = x_prompt.shape
    db, dseq, _ = x_sample.shape
    depth = norm_g.shape[0]
    hp = x_prompt.reshape(nb * seq, d)
    hs = x_sample.reshape(db * dseq, d)
    bm_p, bm_s = ROW_TILE, db * dseq
    pos_p = jnp.arange(seq)
    pos_s = PAST_LEN + jnp.arange(dseq)
    kp_l, vp_l, rp_l, cp_l = [], [], [], []
    ks_l, vs_l, rs_l, cs_l, gv_l = [], [], [], [], []
    xp = _rmsnorm(hp, norm_g[0, 0], bm=ROW_TILE, name="norm_in_p")
    xs = _rmsnorm(hs, norm_g[0, 0], bm=bm_s, name="norm_in_s")
    for layer in range(depth):
        g = norm_g[layer]
        g_next = norm_g[layer + 1, 0] if layer + 1 < depth else None
        if layer % 2 == 0:
            e = layer // 2
            zp = _xw(xp, w_in_even, e, bm=bm_p, bn=IN_EVEN_BLK, gelu=False, name="in_even_p")
            zs = _xw(xs, w_in_even, e, bm=bm_s, bn=IN_EVEN_BLK, gelu=False, name="in_even_s")
            zp = zp.reshape(nb, seq, EVEN_IN)
            zs = zs.reshape(db, dseq, EVEN_IN)
            attn_p = _attn_prompt(zp, attn_sinks[e], tq=ATTN_ROW_TILE)
            attn_s = _attn_sample(zs, cache_swa_k[e], cache_swa_v[e], attn_sinks[e])
            ret_p, r_p = _retention(zp, pos_p, jnp.zeros((nb, N_RET_HEADS, RET_DK, RET_DV), F32),
                                    ret_norm_g[e], chunk=RET_CHUNK, nbatch=1)
            ret_s, r_s = _retention(zs, pos_s, state_ret[e].astype(F32), ret_norm_g[e], chunk=dseq,
                                    nbatch=RET_SAMPLE_SEQS_PER_STEP)
            mixed_p = [attn_p.reshape(nb * seq, -1), ret_p.reshape(nb * seq, -1)]
            mixed_s = [attn_s.reshape(db * dseq, -1), ret_s.reshape(db * dseq, -1)]
            hp, xp = _matmul_norm_res(mixed_p, w_out_even, e, g[1], hp, g[2], bm=OUT_ROW_TILE, bk=OUT_K_BLK,
                                      resident=True, name="out_even_p")
            hs, xs = _matmul_norm_res(mixed_s, w_out_even, e, g[1], hs, g[2], bm=bm_s, bk=OUT_K_BLK,
                                      resident=False, name="out_even_s")
            k_new = zs[:, :, 1024:1280].reshape(db, dseq, N_KV_HEADS, HEAD_DIM)
            v_new = zs[:, :, 1280:1536].reshape(db, dseq, N_KV_HEADS, HEAD_DIM)
            n_keep = cache_swa_k.shape[2]
            kp_l.append(zp[:, seq - WINDOW:, 1024:1280].reshape(nb, WINDOW, N_KV_HEADS, HEAD_DIM))
            vp_l.append(zp[:, seq - WINDOW:, 1280:1536].reshape(nb, WINDOW, N_KV_HEADS, HEAD_DIM))
            ks_l.append(jnp.concatenate([cache_swa_k[e], k_new], axis=1)[:, -n_keep:])
            vs_l.append(jnp.concatenate([cache_swa_v[e], v_new], axis=1)[:, -n_keep:])
            rp_l.append(r_p)
            rs_l.append(r_s.astype(state_ret.dtype))
        else:
            o = layer // 2
            uvp = _xw(xp, w_in_odd, o, bm=bm_p, bn=IN_ODD_BLK, gelu=True, name="in_odd_p")
            uvs = _xw(xs, w_in_odd, o, bm=bm_s, bn=IN_ODD_BLK, gelu=True, name="in_odd_s")
            yp, _ = _spatial_gate(uvp.reshape(nb, seq, -1), gm_ln_g[o], gm_ln_b[o], gm_ws[o], gm_bs[o], rows=1024,
                                  nbatch=1, emit_vn=False)
            ys, gv = _spatial_gate(uvs.reshape(db, dseq, -1), gm_ln_g[o], gm_ln_b[o], gm_ws[o], gm_bs[o],
                                   rows=dseq, nbatch=GATE_SAMPLE_SEQS_PER_STEP, emit_vn=True)
            hp, xp = _matmul_norm_res([yp.reshape(nb * seq, -1)], w_out_odd, o, g[1], hp, g[2], bm=OUT_ROW_TILE,
                                      bk=OUT_K_BLK, resident=True, name="out_odd_p")
            hs, xs = _matmul_norm_res([ys.reshape(db * dseq, -1)], w_out_odd, o, g[1], hs, g[2], bm=bm_s,
                                      bk=OUT_K_BLK, resident=False, name="out_odd_s")
            gv_l.append(gv)
        zero_state = jnp.zeros((nb, CONV_W - 1, 2 * D_FF), F32)
        hp, xp, c_p, hs, xs, c_s = _conv_ffn(hp, xp, hs, xs, g[3], g_next, ffn_w_up, layer, ffn_conv_w[layer],
                                             ffn_conv_b[layer], ffn_w_down, zero_state, state_ffn_conv[layer])
        cp_l.append(c_p)
        cs_l.append(c_s)
    return (hp.reshape(nb, seq, d), hs.reshape(db, dseq, d),
            jnp.stack(kp_l), jnp.stack(vp_l), jnp.stack(rp_l), jnp.stack(cp_l),
            jnp.stack(ks_l), jnp.stack(vs_l), jnp.stack(rs_l), jnp.stack(cs_l), jnp.stack(gv_l))
```
